```python
import math
import jax
import jax.numpy as jnp
from jax import lax
import numpy as np

D_MODEL = 2048
BATCH = 2
SEQ = 8192
DEPTH = 2

N_META = 16
BLOCK = 128
PAD = BLOCK - N_META
RMS_EPS = 1e-6
GN_EPS = 1e-5
NEG = -1e30

DSA_HEADS = 8
DSA_HEAD_DIM = 128
DSA_LATENT = 256
IDX_HEADS = 16
IDX_DIM = 64
INDEX_TOPK = 256

FOX_HEADS = 8
FOX_HEAD_DIM = 128

RET_HEADS = 8
RET_QK_DIM = 64
RET_V_DIM = 128
ROPE_BASE = 10000.0

BRANCH_WIDTH = 1024
N_BRANCH = 3

T5_BUCKETS = 32
T5_MAX_EXACT = 16
T5_MAX_DIST = 128

D_FF = 5632

IN_WIDTHS = (
    DSA_HEADS * DSA_HEAD_DIM,
    DSA_LATENT,
    IDX_HEADS * IDX_DIM,
    IDX_DIM,
    IDX_HEADS,
    FOX_HEADS * FOX_HEAD_DIM,
    FOX_HEADS * FOX_HEAD_DIM,
    FOX_HEADS * FOX_HEAD_DIM,
    FOX_HEADS,
    RET_HEADS * RET_QK_DIM,
    RET_HEADS * RET_QK_DIM,
    RET_HEADS * RET_V_DIM,
    RET_HEADS * RET_V_DIM,
    N_BRANCH * D_MODEL,
)
D_IN = sum(IN_WIDTHS)

kernel_name = "hybrid_dsa_fox_retention_macaron"


def rmsnorm(x, g):
    xf = x.astype(jnp.float32)
    y = xf * lax.rsqrt(jnp.mean(xf * xf, axis=-1, keepdims=True) + RMS_EPS)
    return (y * g.astype(jnp.float32)).astype(x.dtype)


def swiglu(h, w1, w3, w2):
    return (jax.nn.silu(h @ w1) * (h @ w3)) @ w2


def split_cols(a):
    parts, start = [], 0
    for w in IN_WIDTHS:
        parts.append(a[..., start:start + w])
        start += w
    return parts


def to_blocks(a):
    b, t = a.shape[:2]
    return jnp.moveaxis(a.reshape((b, t // BLOCK, BLOCK) + a.shape[2:]), 1, 0)


def from_blocks(a):
    a = jnp.moveaxis(a, 0, 1)
    return a.reshape((a.shape[0], a.shape[1] * a.shape[2]) + a.shape[3:])


def admissible(tq, tk):
    return (tk <= tq) & ((tk >= PAD) | (tk == tq))


def t5_bucket(dist):
    d = jnp.maximum(dist, 1).astype(jnp.float32)
    large = T5_MAX_EXACT + (jnp.log(d / T5_MAX_EXACT) / math.log(T5_MAX_DIST / T5_MAX_EXACT)
                            * (T5_BUCKETS - T5_MAX_EXACT)).astype(jnp.int32)
    large = jnp.minimum(large, T5_BUCKETS - 1)
    return jnp.where(dist < T5_MAX_EXACT, dist, large)


def rope(x, pos):
    half = x.shape[-1] // 2
    inv = ROPE_BASE ** (-jnp.arange(half, dtype=jnp.float32) / half)
    ang = pos.astype(jnp.float32)[:, None] * inv[None, :]
    cos = jnp.cos(ang)[None, :, None, :].astype(x.dtype)
    sin = jnp.sin(ang)[None, :, None, :].astype(x.dtype)
    x1, x2 = x[..., :half], x[..., half:]
    return jnp.concatenate([x1 * cos - x2 * sin, x1 * sin + x2 * cos], axis=-1)


def dsa_branch(h_q, h_c, h_iq, h_ik, h_iw, kv_norm, w_uk, w_uv, t5_table, topk):
    b, t = h_q.shape[:2]
    pos = jnp.arange(t, dtype=jnp.int32)
    q = h_q.reshape(b, t, DSA_HEADS, DSA_HEAD_DIM)
    c = rmsnorm(h_c, kv_norm)
    q_lat = jnp.einsum("bthd,chd->bthc", q, w_uk) * DSA_HEAD_DIM ** -0.5
    qi = h_iq.reshape(b, t, IDX_HEADS, IDX_DIM) * IDX_DIM ** -0.5
    wi = h_iw * IDX_HEADS ** -0.5

    def block(args):
        ql, qib, wib, tq = args
        rel = jnp.einsum("bqhd,bsd->bqhs", qib, h_ik)
        score = jnp.einsum("bqhs,bqh->bqs", jax.nn.relu(rel), wib).astype(jnp.float32)
        score = jnp.where(admissible(tq[:, None], pos[None, :])[None], score, NEG)
        _, idx = lax.top_k(score, topk)
        c_sel = jax.vmap(lambda cb, ib: cb[ib])(c, idx)
        logit = jnp.einsum("bqhc,bqkc->bqhk", ql, c_sel).astype(jnp.float32)
        bias = t5_table[t5_bucket(jnp.maximum(tq[None, :, None] - idx, 0))]
        logit = logit + jnp.moveaxis(bias, -1, 2).astype(jnp.float32)
        ok = admissible(tq[None, :, None], idx)[:, :, None, :]
        p = jax.nn.softmax(jnp.where(ok, logit, NEG), axis=-1).astype(c.dtype)
        o_lat = jnp.einsum("bqhk,bqkc->bqhc", p, c_sel)
        o = jnp.einsum("bqhc,chd->bqhd", o_lat, w_uv)
        return o.reshape(b, BLOCK, DSA_HEADS * DSA_HEAD_DIM)

    out = lax.map(block, (to_blocks(q_lat), to_blocks(qi), to_blocks(wi), pos.reshape(-1, BLOCK)))
    return from_blocks(out)


def fox_branch(h_q, h_k, h_v, h_f, f_bias):
    b, t = h_q.shape[:2]
    pos = jnp.arange(t, dtype=jnp.int32)
    q = h_q.reshape(b, t, FOX_HEADS, FOX_HEAD_DIM) * FOX_HEAD_DIM ** -0.5
    k = h_k.reshape(b, t, FOX_HEADS, FOX_HEAD_DIM)
    v = h_v.reshape(b, t, FOX_HEADS, FOX_HEAD_DIM)
    log_f = jax.nn.log_sigmoid((h_f + f_bias).astype(jnp.float32))
    cum = jnp.cumsum(log_f, axis=1)
    cum_k = jnp.transpose(cum, (0, 2, 1))

    def block(args):
        qb, cq, tq = args
        logit = jnp.einsum("bqhd,bshd->bhqs", qb, k).astype(jnp.float32)
        logit = logit + jnp.transpose(cq, (0, 2, 1))[..., None] - cum_k[:, :, None, :]
        ok = admissible(tq[:, None], pos[None, :])
        p = jax.nn.softmax(jnp.where(ok, logit, NEG), axis=-1).astype(v.dtype)
        return jnp.einsum("bhqs,bshd->bqhd", p, v).reshape(b, BLOCK, FOX_HEADS * FOX_HEAD_DIM)

    out = lax.map(block, (to_blocks(q), to_blocks(cum), pos.reshape(-1, BLOCK)))
    return from_blocks(out)


def retention_branch(h_q, h_k, h_v, h_g, gn_w):
    b, t = h_q.shape[:2]
    n = t // BLOCK
    pos = jnp.arange(t, dtype=jnp.int32)
    q = rope(h_q.reshape(b, t, RET_HEADS, RET_QK_DIM), pos)
    k = rope(h_k.reshape(b, t, RET_HEADS, RET_QK_DIM), pos) * RET_QK_DIM ** -0.5
    k = jnp.where((pos >= PAD)[None, :, None, None], k, 0)
    v = h_v.reshape(b, t, RET_HEADS, RET_V_DIM)
    log_gamma = jnp.log1p(-(2.0 ** (-5.0 - jnp.arange(RET_HEADS, dtype=jnp.float32))))
    i = jnp.arange(BLOCK, dtype=jnp.float32)
    gap = i[:, None] - i[None, :]
    decay = jnp.where(gap[None] >= 0,
                      jnp.exp(jnp.maximum(gap, 0.0)[None] * log_gamma[:, None, None]),
                      0.0).astype(q.dtype)
    qc = q.reshape(b, n, BLOCK, RET_HEADS, RET_QK_DIM)
    kc = k.reshape(b, n, BLOCK, RET_HEADS, RET_QK_DIM)
    vc = v.reshape(b, n, BLOCK, RET_HEADS, RET_V_DIM)
    intra_s = jnp.einsum("bnihd,bnjhd->bnhij", qc, kc) * decay
    intra = jnp.einsum("bnhij,bnjhv->bnihv", intra_s, vc)
    k_dec = kc * jnp.exp((BLOCK - 1 - i)[:, None] * log_gamma[None, :])[..., None].astype(q.dtype)
    kv = jnp.einsum("bnjhd,bnjhv->nbhdv", k_dec, vc)
    chunk_decay = jnp.exp(BLOCK * log_gamma)[:, None, None].astype(kv.dtype)

    def step(state, kv_n):
        return chunk_decay * state + kv_n, state

    _, state_prev = lax.scan(step, jnp.zeros_like(kv[0]), kv)
    q_dec = qc * jnp.exp((i + 1.0)[:, None] * log_gamma[None, :])[..., None].astype(q.dtype)
    cross = jnp.einsum("bnihd,nbhdv->bnihv", q_dec, state_prev)
    o = (intra + cross).reshape(b, t, RET_HEADS, RET_V_DIM).astype(jnp.float32)
    mu = jnp.mean(o, axis=-1, keepdims=True)
    var = jnp.mean(jnp.square(o - mu), axis=-1, keepdims=True)
    o = ((o - mu) * lax.rsqrt(var + GN_EPS)).reshape(b, t, RET_HEADS * RET_V_DIM)
    o = (o * gn_w.astype(jnp.float32)).astype(h_v.dtype)
    return jax.nn.silu(h_g) * o


def hybrid_layer(x, ffn1_norm, ffn1_w1, ffn1_w3, ffn1_w2, mix_norm, w_in, dsa_kv_norm, dsa_w_uk,
                 dsa_w_uv, fox_f_bias, ret_gn_w, w_branch, w_out, ffn2_norm, ffn2_w1, ffn2_w3,
                 ffn2_w2, t5_table, topk):
    b, t = x.shape[:2]
    x = x + 0.5 * swiglu(rmsnorm(x, ffn1_norm), ffn1_w1, ffn1_w3, ffn1_w2)
    h = rmsnorm(x, mix_norm)
    (dq, dc, iq, ik, iw, fq, fk, fv, ff, rq, rk, rv, rg, gates) = split_cols(h @ w_in)
    branches = (
        dsa_branch(dq, dc, iq, ik, iw, dsa_kv_norm, dsa_w_uk, dsa_w_uv, t5_table, topk),
        fox_branch(fq, fk, fv, ff, fox_f_bias),
        retention_branch(rq, rk, rv, rg, ret_gn_w),
    )
    gates = jax.nn.sigmoid(gates.reshape(b, t, N_BRANCH, D_MODEL))
    merged = gates[:, :, 0] * (branches[0] @ w_branch[0])
    for j in range(1, N_BRANCH):
        merged = merged + gates[:, :, j] * (branches[j] @ w_branch[j])
    x = x + merged @ w_out
    x = x + 0.5 * swiglu(rmsnorm(x, ffn2_norm), ffn2_w1, ffn2_w3, ffn2_w2)
    return x


def setup_inputs(seed: int = 0) -> dict:
    key = jax.random.key(seed)
    ks = jax.random.split(key, 21)
    f32 = jnp.float32

    def nrm(k, shape, scale):
        return jax.random.normal(k, shape, f32) * scale

    def gain(k, shape):
        return 1.0 + 0.02 * jax.random.normal(k, shape, f32)

    return {
        "x": nrm(ks[0], (BATCH, SEQ, D_MODEL), 1.0),
        "meta_tokens": nrm(ks[1], (N_META, D_MODEL), 1.0),
        "t5_table": nrm(ks[2], (T5_BUCKETS, DSA_HEADS), 0.3),
        "ffn1_norm": gain(ks[3], (DEPTH, D_MODEL)),
        "ffn1_w1": nrm(ks[4], (DEPTH, D_MODEL, D_FF), D_MODEL ** -0.5),
        "ffn1_w3": nrm(ks[5], (DEPTH, D_MODEL, D_FF), D_MODEL ** -0.5),
        "ffn1_w2": nrm(ks[6], (DEPTH, D_FF, D_MODEL), D_FF ** -0.5),
        "mix_norm": gain(ks[7], (DEPTH, D_MODEL)),
        "w_in": nrm(ks[8], (DEPTH, D_MODEL, D_IN), D_MODEL ** -0.5),
        "dsa_kv_norm": gain(ks[9], (DEPTH, DSA_LATENT)),
        "dsa_w_uk": nrm(ks[10], (DEPTH, DSA_LATENT, DSA_HEADS, DSA_HEAD_DIM), DSA_LATENT ** -0.5),
        "dsa_w_uv": nrm(ks[11], (DEPTH, DSA_LATENT, DSA_HEADS, DSA_HEAD_DIM), DSA_LATENT ** -0.5),
        "fox_f_bias": 2.0 + 0.5 * jax.random.normal(ks[12], (DEPTH, FOX_HEADS), f32),
        "ret_gn_w": gain(ks[13], (DEPTH, RET_HEADS * RET_V_DIM)),
        "w_branch": nrm(ks[14], (DEPTH, N_BRANCH, BRANCH_WIDTH, D_MODEL), BRANCH_WIDTH ** -0.5),
        "w_out": nrm(ks[15], (DEPTH, D_MODEL, D_MODEL), D_MODEL ** -0.5),
        "ffn2_norm": gain(ks[16], (DEPTH, D_MODEL)),
        "ffn2_w1": nrm(ks[17], (DEPTH, D_MODEL, D_FF), D_MODEL ** -0.5),
        "ffn2_w3": nrm(ks[18], (DEPTH, D_MODEL, D_FF), D_MODEL ** -0.5),
        "ffn2_w2": nrm(ks[19], (DEPTH, D_FF, D_MODEL), D_FF ** -0.5),
        "final_norm": gain(ks[20], (D_MODEL,)),
    }


def reference(x, meta_tokens, t5_table, ffn1_norm, ffn1_w1, ffn1_w3, ffn1_w2, mix_norm, w_in,
              dsa_kv_norm, dsa_w_uk, dsa_w_uv, fox_f_bias, ret_gn_w, w_branch, w_out, ffn2_norm,
              ffn2_w1, ffn2_w3, ffn2_w2, final_norm):
    b, seq = x.shape[:2]
    topk = min(INDEX_TOPK, seq // 4)
    h = jnp.concatenate([
        jnp.zeros((b, PAD, D_MODEL), x.dtype),
        jnp.broadcast_to(meta_tokens.astype(x.dtype)[None], (b, N_META, D_MODEL)),
        x,
    ], axis=1)
    for l in range(DEPTH):
        h = hybrid_layer(h, ffn1_norm[l], ffn1_w1[l], ffn1_w3[l], ffn1_w2[l], mix_norm[l], w_in[l],
                         dsa_kv_norm[l], dsa_w_uk[l], dsa_w_uv[l], fox_f_bias[l], ret_gn_w[l],
                         w_branch[l], w_out[l], ffn2_norm[l], ffn2_w1[l], ffn2_w3[l], ffn2_w2[l],
                         t5_table, topk)
    return rmsnorm(h, final_norm)[:, BLOCK:]
```

```python
import functools
import math

import numpy as np
import jax
import jax.numpy as jnp
from jax import lax
from jax.experimental import pallas as pl
from jax.experimental.pallas import tpu as pltpu

F32 = jnp.float32
BF16 = jnp.bfloat16

LANE = 128
VMEM_LIMIT = 56 * 1024 * 1024

D_MODEL = 2048
N_META = 16
BLOCK = 128
PAD = BLOCK - N_META
RMS_EPS = 1e-6
GN_EPS = 1e-5
NEG = -1e30

DSA_HEADS = 8
DSA_HEAD_DIM = 128
DSA_LATENT = 256
IDX_HEADS = 16
IDX_DIM = 64
INDEX_TOPK = 256
FOX_HEADS = 8
FOX_HEAD_DIM = 128
RET_HEADS = 8
RET_QK_DIM = 64
RET_V_DIM = 128
ROPE_BASE = 10000.0
BRANCH_WIDTH = 1024
N_BRANCH = 3
T5_BUCKETS = 32
T5_MAX_EXACT = 16
T5_MAX_DIST = 128

IN_WIDTHS = (1024, 256, 1024, 64, 16, 1024, 1024, 1024, 8, 512, 512, 1024, 1024, 6144)
IN_NAMES = ("dq", "dc", "iq", "ik", "iw", "fq", "fk", "fv", "ff", "rq", "rk", "rv", "rg", "gates")
IN_OFF = dict(zip(IN_NAMES, np.cumsum((0,) + IN_WIDTHS[:-1]).tolist()))
IN_W = dict(zip(IN_NAMES, IN_WIDTHS))

INT_MIN = -2 ** 31
SCORE_GROUP = 4

BF_COLS = 6 * 1024 + LANE
MISC_COLS = 2560
SMALL_IW = 0
SMALL_FF = IDX_HEADS


def _dot(a, b):
    return jnp.dot(a, b, preferred_element_type=F32)


def _dot_nt(a, b):
    return lax.dot_general(a, b, (((1,), (1,)), ((), ())), preferred_element_type=F32)


def _dot_tn(a, b):
    return lax.dot_general(a, b, (((0,), (0,)), ((), ())), preferred_element_type=F32)


def _pick_tile(n, target, mult=8):
    best = None
    for t in range(mult, min(n, target) + 1, mult):
        if n % t == 0:
            best = t
    assert best is not None, (n, target, mult)
    return best


def _params(sem):
    return pltpu.CompilerParams(dimension_semantics=sem, vmem_limit_bytes=VMEM_LIMIT)


def _rms(x, g):
    return x * lax.rsqrt(jnp.mean(x * x, axis=-1, keepdims=True) + RMS_EPS) * g


def _ffn_kernel(*refs, nj, final):
    if final:
        x_ref, g_ref, w1_ref, w3_ref, w2_ref, fg_ref, o_ref, xn_ref, acc_ref = refs
    else:
        x_ref, g_ref, w1_ref, w3_ref, w2_ref, o_ref, xn_ref, acc_ref = refs
    j = pl.program_id(1)

    @pl.when(j == 0)
    def _():
        xn_ref[...] = _rms(x_ref[...], g_ref[...]).astype(BF16)
        acc_ref[...] = jnp.zeros_like(acc_ref)

    xn = xn_ref[...]
    h1 = _dot(xn, w1_ref[...])
    h3 = _dot(xn, w3_ref[...])
    g = (h1 * jax.nn.sigmoid(h1) * h3).astype(BF16)
    acc_ref[...] += _dot(g, w2_ref[...])

    @pl.when(j == nj - 1)
    def _():
        y = x_ref[...] + 0.5 * acc_ref[...]
        if final:
            y = _rms(y, fg_ref[...])
        o_ref[...] = y


def _ffn(x, g, w1, w3, w2, final_g=None):
    n, d = x.shape
    f = w1.shape[1]
    tm = _pick_tile(n, 640)
    tf = _pick_tile(f, 512, LANE)
    nj = f // tf
    final = final_g is not None
    in_specs = [
        pl.BlockSpec((tm, d), lambda i, j: (i, 0)),
        pl.BlockSpec((1, d), lambda i, j: (0, 0)),
        pl.BlockSpec((d, tf), lambda i, j: (0, j)),
        pl.BlockSpec((d, tf), lambda i, j: (0, j)),
        pl.BlockSpec((tf, d), lambda i, j: (j, 0)),
    ]
    args = [x, g.reshape(1, d), w1, w3, w2]
    if final:
        in_specs.append(pl.BlockSpec((1, d), lambda i, j: (0, 0)))
        args.append(final_g.reshape(1, d))
    return pl.pallas_call(
        functools.partial(_ffn_kernel, nj=nj, final=final),
        grid=(n // tm, nj),
        in_specs=in_specs,
        out_specs=pl.BlockSpec((tm, d), lambda i, j: (i, 0)),
        out_shape=jax.ShapeDtypeStruct((n, d), F32),
        scratch_shapes=[pltpu.VMEM((tm, d), BF16), pltpu.VMEM((tm, d), F32)],
        compiler_params=_params(("parallel", "arbitrary")),
    )(*args)


def _proj_kernel(x_ref, g_ref, w_ref, o_ref, xn_ref):
    @pl.when(pl.program_id(1) == 0)
    def _():
        xn_ref[...] = _rms(x_ref[...], g_ref[...]).astype(BF16)

    o_ref[...] = _dot(xn_ref[...], w_ref[...]).astype(o_ref.dtype)


def _proj(x, g, w, out_dtype):
    n, d = x.shape
    c = w.shape[1]
    tm = _pick_tile(n, 640)
    tn = _pick_tile(c, 1024, LANE)
    return pl.pallas_call(
        _proj_kernel,
        grid=(n // tm, c // tn),
        in_specs=[
            pl.BlockSpec((tm, d), lambda i, j: (i, 0)),
            pl.BlockSpec((1, d), lambda i, j: (0, 0)),
            pl.BlockSpec((d, tn), lambda i, j: (0, j)),
        ],
        out_specs=pl.BlockSpec((tm, tn), lambda i, j: (i, j)),
        out_shape=jax.ShapeDtypeStruct((n, c), out_dtype),
        scratch_shapes=[pltpu.VMEM((tm, d), BF16)],
        compiler_params=_params(("parallel", "arbitrary")),
    )(x, g.reshape(1, d), w)


def _latent_kernel(dc_ref, g_ref, c_ref):
    c_ref[...] = _rms(dc_ref[...], g_ref[...]).astype(BF16)


def _latent(misc, kv_norm):
    n = misc.shape[0]
    tm = _pick_tile(n, 1280)
    blk = (IN_W["rq"] + IN_W["rk"] + IN_W["rg"]) // DSA_LATENT
    return pl.pallas_call(
        _latent_kernel,
        grid=(n // tm,),
        in_specs=[
            pl.BlockSpec((tm, DSA_LATENT), lambda i: (i, blk)),
            pl.BlockSpec((1, DSA_LATENT), lambda i: (0, 0)),
        ],
        out_specs=pl.BlockSpec((tm, DSA_LATENT), lambda i: (i, 0)),
        out_shape=jax.ShapeDtypeStruct((n, DSA_LATENT), BF16),
        compiler_params=_params(("parallel",)),
    )(misc, kv_norm.reshape(1, DSA_LATENT))


def _split3(x):
    hi = x.astype(BF16)
    r = x - hi.astype(F32)
    mid = r.astype(BF16)
    lo = (r - mid.astype(F32)).astype(BF16)
    return hi, mid, lo


def _cum_kernel(s_ref, b_ref, o_ref, carry_ref):
    @pl.when(pl.program_id(1) == 0)
    def _():
        carry_ref[...] = jnp.zeros_like(carry_ref)

    z = s_ref[...] + b_ref[...]
    lf = jnp.minimum(z, 0.0) - jnp.log1p(jnp.exp(-jnp.abs(z)))
    r = lax.broadcasted_iota(jnp.int32, (BLOCK, BLOCK), 0)
    c = lax.broadcasted_iota(jnp.int32, (BLOCK, BLOCK), 1)
    tri = jnp.where(c <= r, 1.0, 0.0).astype(BF16)
    hi, mid, lo = _split3(lf)
    cum = (_dot(tri, lo) + _dot(tri, mid)) + _dot(tri, hi) + carry_ref[...]
    o_ref[...] = cum
    carry_ref[...] = cum[BLOCK - 1:BLOCK, :]


def _cum_logf(misc3, f_bias):
    b, t, _ = misc3.shape
    blk = (IN_W["rq"] + IN_W["rk"] + IN_W["rg"] + IN_W["dc"]) // LANE
    bias = jnp.zeros((1, LANE), F32).at[0, SMALL_FF:SMALL_FF + FOX_HEADS].set(f_bias)
    return pl.pallas_call(
        _cum_kernel,
        grid=(b, t // BLOCK),
        in_specs=[
            pl.BlockSpec((None, BLOCK, LANE), lambda bi, i: (bi, i, blk)),
            pl.BlockSpec((1, LANE), lambda bi, i: (0, 0)),
        ],
        out_specs=pl.BlockSpec((None, BLOCK, LANE), lambda bi, i: (bi, i, 0)),
        out_shape=jax.ShapeDtypeStruct((b, t, LANE), F32),
        scratch_shapes=[pltpu.VMEM((1, LANE), F32)],
        compiler_params=_params(("parallel", "arbitrary")),
    )(misc3, bias)


def _dsa_kernel(dq_ref, iq_ref, sm_ref, ik_ref, c_ref, wuk_ref, wuv_ref, bdiag_ref, bprev_ref, bfar_ref,
                o_ref, s_ref, qlat_ref, wb_ref, m_ref, l_ref, acc_ref, *, topk):
    i = pl.program_id(1)
    row = lax.broadcasted_iota(jnp.int32, (BLOCK, BLOCK), 0)
    col = lax.broadcasted_iota(jnp.int32, (BLOCK, BLOCK), 1)

    for h in range(DSA_HEADS):
        ql = _dot(dq_ref[:, h * DSA_HEAD_DIM:(h + 1) * DSA_HEAD_DIM], wuk_ref[h]) * DSA_HEAD_DIM ** -0.5
        qlat_ref[h * BLOCK:(h + 1) * BLOCK, :] = ql.astype(BF16)
    sm = sm_ref[...]
    for h in range(IDX_HEADS):
        w = sm[:, SMALL_IW + h:SMALL_IW + h + 1] * (IDX_HEADS ** -0.5 * IDX_DIM ** -0.5)
        wb_ref[h] = jnp.broadcast_to(w, (BLOCK, BLOCK))

    def score_tile(j, _):
        kt = ik_ref[pl.ds(pl.multiple_of(j * BLOCK, BLOCK), BLOCK), :]
        r = _dot_nt(iq_ref[...], kt)
        acc = jnp.zeros((BLOCK, BLOCK), F32)
        for h in range(IDX_HEADS):
            acc = acc + jnp.maximum(r[h * BLOCK:(h + 1) * BLOCK, :], 0.0) * wb_ref[h]
        t = i * BLOCK + row
        s = j * BLOCK + col
        ok = (s <= t) & ((s >= PAD) | (s == t))
        bits = lax.bitcast_convert_type(acc, jnp.int32)
        bits = jnp.where(bits == INT_MIN, 0, bits)
        key = bits ^ ((bits >> 31) & 0x7FFFFFFF)
        s_ref[j] = jnp.where(ok, key, INT_MIN)
        return 0

    lax.fori_loop(0, i + 1, score_tile, 0)
    for u in range(1, SCORE_GROUP):
        s_ref[i + u] = jnp.full((BLOCK, BLOCK), INT_MIN, jnp.int32)

    ngrp = (i + SCORE_GROUP) // SCORE_GROUP

    def count_ge(cand):
        cb = jnp.broadcast_to(cand, (BLOCK, BLOCK))

        def body(g, acc):
            for u in range(SCORE_GROUP):
                acc = acc + jnp.where(s_ref[g * SCORE_GROUP + u] >= cb, 1.0, 0.0)
            return acc

        acc = lax.fori_loop(0, ngrp, body, jnp.zeros((BLOCK, BLOCK), F32))
        return jnp.sum(acc, axis=1, keepdims=True)

    zero = jnp.zeros((BLOCK, 1), jnp.int32)
    thr0 = jnp.where(count_ge(zero) >= topk, zero, INT_MIN)

    def bit_step(it, thr):
        cand = thr + (jnp.int32(1) << (30 - it))
        return jnp.where(count_ge(cand) >= topk, cand, thr)

    thr = lax.fori_loop(0, 31, bit_step, thr0)
    thr_b = jnp.broadcast_to(jnp.maximum(thr, INT_MIN + 1), (BLOCK, BLOCK))

    m_ref[...] = jnp.full(m_ref.shape, NEG, F32)
    l_ref[...] = jnp.zeros_like(l_ref)
    acc_ref[...] = jnp.zeros_like(acc_ref)

    def attend(j, bias_of_head):
        ct = c_ref[pl.ds(pl.multiple_of(j * BLOCK, BLOCK), BLOCK), :]
        lg = _dot_nt(qlat_ref[...], ct)
        sel = s_ref[j] >= thr_b
        for h in range(DSA_HEADS):
            x = jnp.where(sel, lg[h * BLOCK:(h + 1) * BLOCK, :] + bias_of_head(h), NEG)
            m_prev = m_ref[h]
            m_new = jnp.maximum(m_prev, jnp.max(x, axis=1, keepdims=True))
            alpha = jnp.exp(m_prev - m_new)
            p = jnp.exp(x - m_new)
            l_ref[h] = alpha * l_ref[h] + jnp.sum(p, axis=1, keepdims=True)
            acc_ref[h] = acc_ref[h] * alpha[:, :1] + _dot(p.astype(BF16), ct)
            m_ref[h] = m_new

    def far_tile(j, _):
        attend(j, lambda h: bfar_ref[h])
        return 0

    lax.fori_loop(0, jnp.maximum(i - 1, 0), far_tile, 0)

    @pl.when(i >= 1)
    def _():
        attend(i - 1, lambda h: bprev_ref[h])

    attend(i, lambda h: bdiag_ref[h])

    for h in range(DSA_HEADS):
        o_lat = (acc_ref[h] * (1.0 / l_ref[h][:, :1])).astype(BF16)
        o = _dot(o_lat, wuv_ref[h])
        o_ref[:, h * DSA_HEAD_DIM:(h + 1) * DSA_HEAD_DIM] = o.astype(o_ref.dtype)


def _dsa(pbf3, iq_hm, misc3, c3, wuk, wuv, bdiag, bprev, bfar, topk):
    b, t, _ = pbf3.shape
    nblk = t // BLOCK
    hq = DSA_HEADS * DSA_HEAD_DIM
    small_blk = (IN_W["rq"] + IN_W["rk"] + IN_W["rg"] + IN_W["dc"]) // LANE
    ik_blk = 6 * 1024 // LANE
    n_tiles = (nblk + SCORE_GROUP - 1) // SCORE_GROUP * SCORE_GROUP + SCORE_GROUP
    const3 = lambda bi, i: (0, 0, 0)
    return pl.pallas_call(
        functools.partial(_dsa_kernel, topk=topk),
        grid=(b, nblk),
        in_specs=[
            pl.BlockSpec((None, BLOCK, hq), lambda bi, i: (bi, i, 0)),
            pl.BlockSpec((None, None, IDX_HEADS * BLOCK, LANE), lambda bi, i: (bi, i, 0, 0)),
            pl.BlockSpec((None, BLOCK, LANE), lambda bi, i: (bi, i, small_blk)),
            pl.BlockSpec((None, t, LANE), lambda bi, i: (bi, 0, ik_blk)),
            pl.BlockSpec((None, t, DSA_LATENT), lambda bi, i: (bi, 0, 0)),
            pl.BlockSpec((DSA_HEADS, DSA_HEAD_DIM, DSA_LATENT), const3),
            pl.BlockSpec((DSA_HEADS, DSA_LATENT, DSA_HEAD_DIM), const3),
            pl.BlockSpec((DSA_HEADS, BLOCK, BLOCK), const3),
            pl.BlockSpec((DSA_HEADS, BLOCK, BLOCK), const3),
            pl.BlockSpec((DSA_HEADS, 1, BLOCK), const3),
        ],
        out_specs=pl.BlockSpec((None, BLOCK, hq), lambda bi, i: (bi, i, 0)),
        out_shape=jax.ShapeDtypeStruct((b, t, hq), BF16),
        scratch_shapes=[
            pltpu.VMEM((n_tiles, BLOCK, BLOCK), jnp.int32),
            pltpu.VMEM((DSA_HEADS * BLOCK, DSA_LATENT), BF16),
            pltpu.VMEM((IDX_HEADS, BLOCK, BLOCK), F32),
            pltpu.VMEM((DSA_HEADS, BLOCK, BLOCK), F32),
            pltpu.VMEM((DSA_HEADS, BLOCK, BLOCK), F32),
            pltpu.VMEM((DSA_HEADS, BLOCK, DSA_LATENT), F32),
        ],
        compiler_params=_params(("parallel", "arbitrary")),
    )(pbf3, iq_hm, misc3, pbf3, c3, wuk, wuv, bdiag, bprev, bfar)


def _fox_kernel(q_ref, k_ref, v_ref, cq_ref, ck_ref, o_ref, m_ref, l_ref, acc_ref, *, tq):
    i = pl.program_id(2)
    qs = (q_ref[...].astype(F32) * FOX_HEAD_DIM ** -0.5).astype(BF16)
    cq = cq_ref[...]
    m_ref[...] = jnp.full(m_ref.shape, NEG, F32)
    l_ref[...] = jnp.zeros_like(l_ref)
    acc_ref[...] = jnp.zeros_like(acc_ref)

    def tile(j, mode):
        off = pl.multiple_of(j * tq, tq)
        kt = k_ref[pl.ds(off, tq), :]
        vt = v_ref[pl.ds(off, tq), :]
        x = _dot_nt(qs, kt) + cq - ck_ref[:, pl.ds(off, tq)]
        if mode == "pad":
            s = lax.broadcasted_iota(jnp.int32, (1, tq), 1)
            x = jnp.where(s >= PAD, x, NEG)
        elif mode == "diag":
            r = lax.broadcasted_iota(jnp.int32, (tq, tq), 0)
            c = lax.broadcasted_iota(jnp.int32, (tq, tq), 1)
            x = jnp.where((c <= r) & ((c + j * tq >= PAD) | (c == r)), x, NEG)
        m_prev = m_ref[...]
        m_new = jnp.maximum(m_prev, jnp.max(x, axis=1, keepdims=True))
        alpha = jnp.exp(m_prev - m_new)
        p = jnp.exp(x - m_new[:, :1])
        l_ref[...] = alpha * l_ref[...] + jnp.sum(p, axis=1, keepdims=True)
        acc_ref[...] = acc_ref[...] * alpha + _dot(p.astype(BF16), vt)
        m_ref[...] = m_new

    @pl.when(i > 0)
    def _():
        tile(0, "pad")

    def mid(j, _):
        tile(j, "none")
        return 0

    lax.fori_loop(1, i, mid, 0)
    tile(i, "diag")
    o_ref[...] = (acc_ref[...] * (1.0 / l_ref[...])).astype(o_ref.dtype)


def _fox(pbf3, cq, ck):
    b, t, _ = pbf3.shape
    nblk = t // BLOCK
    tq = BLOCK * max(k for k in (1, 5) if nblk % k == 0)
    d = FOX_HEAD_DIM
    qb, kb, vb = 2 * 1024 // d, 3 * 1024 // d, 4 * 1024 // d
    return pl.pallas_call(
        functools.partial(_fox_kernel, tq=tq),
        grid=(b, FOX_HEADS, t // tq),
        in_specs=[
            pl.BlockSpec((None, tq, d), lambda bi, h, i: (bi, i, qb + h)),
            pl.BlockSpec((None, t, d), lambda bi, h, i: (bi, 0, kb + h)),
            pl.BlockSpec((None, t, d), lambda bi, h, i: (bi, 0, vb + h)),
            pl.BlockSpec((None, None, tq, 1), lambda bi, h, i: (bi, h, i, 0)),
            pl.BlockSpec((None, None, 1, t), lambda bi, h, i: (bi, h, 0, 0)),
        ],
        out_specs=pl.BlockSpec((None, tq, d), lambda bi, h, i: (bi, i, h)),
        out_shape=jax.ShapeDtypeStruct((b, t, FOX_HEADS * d), BF16),
        scratch_shapes=[pltpu.VMEM((tq, d), F32), pltpu.VMEM((tq, d), F32), pltpu.VMEM((tq, d), F32)],
        compiler_params=_params(("parallel", "parallel", "arbitrary")),
    )(pbf3, pbf3, pbf3, cq, ck)


def _ret_kernel(rq_ref, rk_ref, rv_ref, rg_ref, cos_ref, sin_ref, dq_ref, dk_ref, decay_ref, cd_ref, gn_ref,
                o_ref, state_ref):
    i = pl.program_id(1)

    @pl.when(i == 0)
    def _():
        state_ref[...] = jnp.zeros_like(state_ref)

    half = RET_HEADS * RET_QK_DIM // 2
    cos = cos_ref[...]
    sin = sin_ref[...]

    def rope(x):
        x1, x2 = x[:, :half], x[:, half:]
        return jnp.concatenate([x1 * cos - x2 * sin, x1 * sin + x2 * cos], axis=1)

    q = rope(rq_ref[...])
    k = rope(rk_ref[...]) * RET_QK_DIM ** -0.5
    pos = i * BLOCK + lax.broadcasted_iota(jnp.int32, (BLOCK, 1), 0)
    k = jnp.where(pos >= PAD, k, 0.0)
    qb = q.astype(BF16)
    kb = k.astype(BF16)
    q_dec = (q * dq_ref[...]).astype(BF16)
    k_dec = (k * dk_ref[...]).astype(BF16)
    head_of_col = (lax.broadcasted_iota(jnp.int32, (1, 2 * half), 1) % half) // (RET_QK_DIM // 2)
    zero = jnp.zeros((), BF16)
    for h in range(RET_HEADS):
        mine = head_of_col == h
        vh = rv_ref[:, h * RET_V_DIM:(h + 1) * RET_V_DIM]
        s = (_dot_nt(jnp.where(mine, qb, zero), kb) * decay_ref[h]).astype(BF16)
        st = state_ref[h]
        o = _dot(s, vh) + _dot(jnp.where(mine, q_dec, zero), st.astype(BF16))
        state_ref[h] = cd_ref[h] * st + _dot_tn(k_dec, vh)
        mu = jnp.mean(o, axis=1, keepdims=True)
        var = jnp.mean(jnp.square(o - mu), axis=1, keepdims=True)
        sl = slice(h * RET_V_DIM, (h + 1) * RET_V_DIM)
        o = (o - mu) * lax.rsqrt(var + GN_EPS) * gn_ref[:, sl]
        g = rg_ref[:, sl]
        o_ref[:, sl] = (g * jax.nn.sigmoid(g) * o).astype(o_ref.dtype)


def _retention(misc3, pbf3, cos, sin, dq_tab, dk_tab, decay, cd, gn_w):
    b, t, _ = misc3.shape
    qk = RET_HEADS * RET_QK_DIM
    vw = RET_HEADS * RET_V_DIM
    const2 = lambda bi, i: (0, 0)
    const3 = lambda bi, i: (0, 0, 0)
    return pl.pallas_call(
        _ret_kernel,
        grid=(b, t // BLOCK),
        in_specs=[
            pl.BlockSpec((None, BLOCK, qk), lambda bi, i: (bi, i, 0)),
            pl.BlockSpec((None, BLOCK, qk), lambda bi, i: (bi, i, 1)),
            pl.BlockSpec((None, BLOCK, vw), lambda bi, i: (bi, i, 5)),
            pl.BlockSpec((None, BLOCK, vw), lambda bi, i: (bi, i, 1)),
            pl.BlockSpec((BLOCK, qk // 2), lambda bi, i: (i, 0)),
            pl.BlockSpec((BLOCK, qk // 2), lambda bi, i: (i, 0)),
            pl.BlockSpec((BLOCK, qk), const2),
            pl.BlockSpec((BLOCK, qk), const2),
            pl.BlockSpec((RET_HEADS, BLOCK, BLOCK), const3),
            pl.BlockSpec((RET_HEADS, 1, RET_V_DIM), const3),
            pl.BlockSpec((1, vw), const2),
        ],
        out_specs=pl.BlockSpec((None, BLOCK, vw), lambda bi, i: (bi, i, 0)),
        out_shape=jax.ShapeDtypeStruct((b, t, vw), BF16),
        scratch_shapes=[pltpu.VMEM((RET_HEADS, qk, RET_V_DIM), F32)],
        compiler_params=_params(("parallel", "arbitrary")),
    )(misc3, misc3, pbf3, misc3, cos, sin, dq_tab, dk_tab, decay, cd, gn_w.reshape(1, vw))


def _merge_kernel(b0_ref, b1_ref, b2_ref, g0_ref, g1_ref, g2_ref, w_ref, o_ref):
    acc = jax.nn.sigmoid(g0_ref[...]) * _dot(b0_ref[...], w_ref[0])
    acc = acc + jax.nn.sigmoid(g1_ref[...]) * _dot(b1_ref[...], w_ref[1])
    acc = acc + jax.nn.sigmoid(g2_ref[...]) * _dot(b2_ref[...], w_ref[2])
    o_ref[...] = acc.astype(o_ref.dtype)


def _merge(b0, b1, b2, gates, w_branch):
    n, bw = b0.shape
    d = w_branch.shape[2]
    tm = _pick_tile(n, 640)
    tn = _pick_tile(d, 512, LANE)
    nj = d // tn
    bspec = pl.BlockSpec((tm, bw), lambda i, j: (i, 0))
    gspec = lambda r: pl.BlockSpec((tm, tn), lambda i, j: (i, r * nj + j))
    return pl.pallas_call(
        _merge_kernel,
        grid=(n // tm, nj),
        in_specs=[bspec, bspec, bspec, gspec(0), gspec(1), gspec(2),
                  pl.BlockSpec((N_BRANCH, bw, tn), lambda i, j: (0, 0, j))],
        out_specs=pl.BlockSpec((tm, tn), lambda i, j: (i, j)),
        out_shape=jax.ShapeDtypeStruct((n, d), BF16),
        compiler_params=_params(("parallel", "arbitrary")),
    )(b0, b1, b2, gates, gates, gates, w_branch)


def _resproj_kernel(a_ref, w_ref, r_ref, o_ref):
    o_ref[...] = r_ref[...] + _dot(a_ref[...], w_ref[...])


def _resproj(a, w, res):
    n, k = a.shape
    d = w.shape[1]
    tm = _pick_tile(n, 640)
    tn = _pick_tile(d, 512, LANE)
    return pl.pallas_call(
        _resproj_kernel,
        grid=(n // tm, d // tn),
        in_specs=[
            pl.BlockSpec((tm, k), lambda i, j: (i, 0)),
            pl.BlockSpec((k, tn), lambda i, j: (0, j)),
            pl.BlockSpec((tm, tn), lambda i, j: (i, j)),
        ],
        out_specs=pl.BlockSpec((tm, tn), lambda i, j: (i, j)),
        out_shape=jax.ShapeDtypeStruct((n, d), F32),
        compiler_params=_params(("parallel", "arbitrary")),
    )(a, w, res)


def _t5_bucket(dist):
    d = jnp.maximum(dist, 1).astype(F32)
    large = T5_MAX_EXACT + (jnp.log(d / T5_MAX_EXACT) / math.log(T5_MAX_DIST / T5_MAX_EXACT)
                            * (T5_BUCKETS - T5_MAX_EXACT)).astype(jnp.int32)
    large = jnp.minimum(large, T5_BUCKETS - 1)
    return jnp.where(dist < T5_MAX_EXACT, dist, large)


def _t5_bias_tiles(t5_table):
    r = jnp.arange(BLOCK, dtype=jnp.int32)[:, None]
    c = jnp.arange(BLOCK, dtype=jnp.int32)[None, :]
    table = t5_table.astype(F32)
    bdiag = jnp.moveaxis(table[_t5_bucket(jnp.maximum(r - c, 0))], -1, 0)
    bprev = jnp.moveaxis(table[_t5_bucket(BLOCK + r - c)], -1, 0)
    bfar = jnp.broadcast_to(table[T5_BUCKETS - 1][:, None, None], (DSA_HEADS, 1, BLOCK))
    return bdiag, bprev, bfar


def _rope_tables(t):
    half = RET_QK_DIM // 2
    inv = ROPE_BASE ** (-jnp.arange(half, dtype=F32) / half)
    ang = jnp.arange(t, dtype=jnp.int32).astype(F32)[:, None] * inv[None, :]
    return jnp.tile(jnp.cos(ang), (1, RET_HEADS)), jnp.tile(jnp.sin(ang), (1, RET_HEADS))


def _decay_tables():
    log_gamma = jnp.log1p(-(2.0 ** (-5.0 - jnp.arange(RET_HEADS, dtype=F32))))
    i = jnp.arange(BLOCK, dtype=F32)
    gap = i[:, None] - i[None, :]
    decay = jnp.where(gap[None] >= 0, jnp.exp(jnp.maximum(gap, 0.0)[None] * log_gamma[:, None, None]), 0.0)
    head_of_col = (np.arange(RET_HEADS * RET_QK_DIM) % (RET_HEADS * RET_QK_DIM // 2)) // (RET_QK_DIM // 2)
    lg_col = log_gamma[head_of_col]
    dk_tab = jnp.exp((BLOCK - 1 - i)[:, None] * lg_col[None, :])
    dq_tab = jnp.exp((i + 1.0)[:, None] * lg_col[None, :])
    cd = jnp.broadcast_to(jnp.exp(BLOCK * log_gamma)[:, None, None], (RET_HEADS, 1, RET_V_DIM))
    return decay.astype(F32), dq_tab, dk_tab, cd


def _rope_perm():
    half = RET_QK_DIM // 2
    h = np.arange(RET_HEADS)[:, None]
    f = np.arange(half)[None, :]
    first = (h * RET_QK_DIM + f).reshape(-1)
    return np.concatenate([first, first + half])


def _split_w_in(w_in):
    cols = lambda name: w_in[:, IN_OFF[name]:IN_OFF[name] + IN_W[name]]
    zeros = lambda w: jnp.zeros((w_in.shape[0], w), w_in.dtype)
    perm = _rope_perm()
    w_bf = jnp.concatenate([cols("dq"), cols("iq"), cols("fq"), cols("fk"), cols("fv"), cols("rv"),
                            cols("ik"), zeros(LANE - IDX_DIM)], axis=1).astype(BF16)
    w_misc = jnp.concatenate([cols("rq")[:, perm], cols("rk")[:, perm], cols("rg"), cols("dc"),
                              cols("iw"), cols("ff"), zeros(LANE - IDX_HEADS - FOX_HEADS),
                              zeros(LANE)], axis=1).astype(BF16)
    assert w_bf.shape[1] == BF_COLS and w_misc.shape[1] == MISC_COLS
    return w_bf, w_misc, cols("gates").astype(BF16)


def _mixer(x, b, t, mix_norm, w_in, kv_norm, w_uk, w_uv, f_bias, gn_w, w_branch, w_out, tables, topk):
    n = b * t
    nblk = t // BLOCK
    bias_tiles, rope_tabs, decay_tabs = tables
    w_bf, w_misc, w_gates = _split_w_in(w_in)
    pbf = _proj(x, mix_norm, w_bf, BF16)
    misc = _proj(x, mix_norm, w_misc, F32)
    gates = _proj(x, mix_norm, w_gates, F32)
    pbf3 = pbf.reshape(b, t, BF_COLS)
    misc3 = misc.reshape(b, t, MISC_COLS)

    c3 = _latent(misc, kv_norm).reshape(b, t, DSA_LATENT)
    iq = pbf3[:, :, IN_W["dq"]:IN_W["dq"] + IN_W["iq"]].reshape(b, nblk, BLOCK, IDX_HEADS, IDX_DIM)
    iq_hm = jnp.pad(jnp.swapaxes(iq, 2, 3), ((0, 0),) * 4 + ((0, LANE - IDX_DIM),))
    iq_hm = iq_hm.reshape(b, nblk, IDX_HEADS * BLOCK, LANE)
    wuk = jnp.transpose(w_uk, (1, 2, 0)).astype(BF16)
    wuv = jnp.transpose(w_uv, (1, 0, 2)).astype(BF16)
    dsa_o = _dsa(pbf3, iq_hm, misc3, c3, wuk, wuv, *bias_tiles, topk)

    cum = _cum_logf(misc3, f_bias)[:, :, SMALL_FF:SMALL_FF + FOX_HEADS]
    cum_h = jnp.swapaxes(cum, 1, 2)
    fox_o = _fox(pbf3, cum_h[..., None], cum_h[:, :, None, :])

    ret_o = _retention(misc3, pbf3, *rope_tabs, *decay_tabs, gn_w)

    merged = _merge(dsa_o.reshape(n, -1), fox_o.reshape(n, -1), ret_o.reshape(n, -1), gates,
                    w_branch.astype(BF16))
    return _resproj(merged, w_out.astype(BF16), x)


def kernel(x, meta_tokens, t5_table, ffn1_norm, ffn1_w1, ffn1_w3, ffn1_w2, mix_norm, w_in, dsa_kv_norm,
           dsa_w_uk, dsa_w_uv, fox_f_bias, ret_gn_w, w_branch, w_out, ffn2_norm, ffn2_w1, ffn2_w3, ffn2_w2,
           final_norm):
    b, seq, d = x.shape
    assert d == D_MODEL and seq % BLOCK == 0
    t = seq + BLOCK
    depth = ffn1_norm.shape[0]
    topk = min(INDEX_TOPK, seq // 4)
    h = jnp.concatenate([
        jnp.zeros((b, PAD, d), x.dtype),
        jnp.broadcast_to(meta_tokens.astype(x.dtype)[None], (b, N_META, d)),
        x,
    ], axis=1).reshape(b * t, d)
    cos, sin = _rope_tables(t)
    decay, dq_tab, dk_tab, cd = _decay_tables()
    tables = (_t5_bias_tiles(t5_table), (cos, sin), (dq_tab, dk_tab, decay, cd))
    for l in range(depth):
        h = _ffn(h, ffn1_norm[l], ffn1_w1[l].astype(BF16), ffn1_w3[l].astype(BF16), ffn1_w2[l].astype(BF16))
        h = _mixer(h, b, t, mix_norm[l], w_in[l], dsa_kv_norm[l], dsa_w_uk[l], dsa_w_uv[l], fox_f_bias[l],
                   ret_gn_w[l], w_branch[l], w_out[l], tables, topk)
        h = _ffn(h, ffn2_norm[l], ffn2_w1[l].astype(BF16), ffn2_w3[l].astype(BF16), ffn2_w2[l].astype(BF16),
                 final_g=final_norm if l == depth - 1 else None)
    return h.reshape(b, t, d)[:, BLOCK:]
```

```python
import functools
import math

import numpy as np
import jax
import jax.numpy as jnp
from jax import lax
from jax.experimental import pallas as pl
from jax.experimental.pallas import tpu as pltpu

F32 = jnp.float32
BF16 = jnp.bfloat16

LANE = 128
VMEM_LIMIT = 56 * 1024 * 1024

D_MODEL = 2048
N_META = 16
BLOCK = 128
PAD = BLOCK - N_META
RMS_EPS = 1e-6
GN_EPS = 1e-5
NEG = -1e30

DSA_HEADS = 8
DSA_HEAD_DIM = 128
DSA_LATENT = 256
IDX_HEADS = 16
IDX_DIM = 64
INDEX_TOPK = 256
FOX_HEADS = 8
FOX_HEAD_DIM = 128
RET_HEADS = 8
RET_QK_DIM = 64
RET_V_DIM = 128
ROPE_BASE = 10000.0
BRANCH_WIDTH = 1024
N_BRANCH = 3
T5_BUCKETS = 32
T5_MAX_EXACT = 16
T5_MAX_DIST = 128

IN_WIDTHS = (1024, 256, 1024, 64, 16, 1024, 1024, 1024, 8, 512, 512, 1024, 1024, 6144)
IN_NAMES = ("dq", "dc", "iq", "ik", "iw", "fq", "fk", "fv", "ff", "rq", "rk", "rv", "rg", "gates")
IN_OFF = dict(zip(IN_NAMES, np.cumsum((0,) + IN_WIDTHS[:-1]).tolist()))
IN_W = dict(zip(IN_NAMES, IN_WIDTHS))

INT_MIN = -2 ** 31
SCORE_GROUP = 4

BF_COLS = 6 * 1024 + LANE
MISC_COLS = 2560
SMALL_IW = 0
SMALL_FF = IDX_HEADS


def _dot(a, b):
    return jnp.dot(a, b, preferred_element_type=F32)


def _dot_nt(a, b):
    return lax.dot_general(a, b, (((1,), (1,)), ((), ())), preferred_element_type=F32)


def _dot_tn(a, b):
    return lax.dot_general(a, b, (((0,), (0,)), ((), ())), preferred_element_type=F32)


def _pick_tile(n, target, mult=8):
    best = None
    for t in range(mult, min(n, target) + 1, mult):
        if n % t == 0:
            best = t
    assert best is not None, (n, target, mult)
    return best


def _params(sem):
    return pltpu.CompilerParams(dimension_semantics=sem, vmem_limit_bytes=VMEM_LIMIT)


def _rms(x, g):
    return x * lax.rsqrt(jnp.mean(x * x, axis=-1, keepdims=True) + RMS_EPS) * g


def _ffn_kernel(*refs, nj, final):
    if final:
        x_ref, g_ref, w1_ref, w3_ref, w2_ref, fg_ref, o_ref, xn_ref, acc_ref = refs
    else:
        x_ref, g_ref, w1_ref, w3_ref, w2_ref, o_ref, xn_ref, acc_ref = refs
    j = pl.program_id(1)

    @pl.when(j == 0)
    def _():
        xn_ref[...] = _rms(x_ref[...], g_ref[...]).astype(BF16)
        acc_ref[...] = jnp.zeros_like(acc_ref)

    xn = xn_ref[...]
    h1 = _dot(xn, w1_ref[...])
    h3 = _dot(xn, w3_ref[...])
    g = (h1 * jax.nn.sigmoid(h1) * h3).astype(BF16)
    acc_ref[...] += _dot(g, w2_ref[...])

    @pl.when(j == nj - 1)
    def _():
        y = x_ref[...] + 0.5 * acc_ref[...]
        if final:
            y = _rms(y, fg_ref[...])
        o_ref[...] = y


def _ffn(x, g, w1, w3, w2, final_g=None):
    n, d = x.shape
    f = w1.shape[1]
    tm = _pick_tile(n, 640)
    tf = _pick_tile(f, 512, LANE)
    nj = f // tf
    final = final_g is not None
    in_specs = [
        pl.BlockSpec((tm, d), lambda i, j: (i, 0)),
        pl.BlockSpec((1, d), lambda i, j: (0, 0)),
        pl.BlockSpec((d, tf), lambda i, j: (0, j)),
        pl.BlockSpec((d, tf), lambda i, j: (0, j)),
        pl.BlockSpec((tf, d), lambda i, j: (j, 0)),
    ]
    args = [x, g.reshape(1, d), w1, w3, w2]
    if final:
        in_specs.append(pl.BlockSpec((1, d), lambda i, j: (0, 0)))
        args.append(final_g.reshape(1, d))
    return pl.pallas_call(
        functools.partial(_ffn_kernel, nj=nj, final=final),
        grid=(n // tm, nj),
        in_specs=in_specs,
        out_specs=pl.BlockSpec((tm, d), lambda i, j: (i, 0)),
        out_shape=jax.ShapeDtypeStruct((n, d), F32),
        scratch_shapes=[pltpu.VMEM((tm, d), BF16), pltpu.VMEM((tm, d), F32)],
        compiler_params=_params(("parallel", "arbitrary")),
    )(*args)


def _proj_kernel(x_ref, g_ref, w_ref, o_ref, xn_ref):
    @pl.when(pl.program_id(1) == 0)
    def _():
        xn_ref[...] = _rms(x_ref[...], g_ref[...]).astype(BF16)

    o_ref[...] = _dot(xn_ref[...], w_ref[...]).astype(o_ref.dtype)


def _proj(x, g, w, out_dtype):
    n, d = x.shape
    c = w.shape[1]
    tm = _pick_tile(n, 640)
    tn = _pick_tile(c, 1024, LANE)
    return pl.pallas_call(
        _proj_kernel,
        grid=(n // tm, c // tn),
        in_specs=[
            pl.BlockSpec((tm, d), lambda i, j: (i, 0)),
            pl.BlockSpec((1, d), lambda i, j: (0, 0)),
            pl.BlockSpec((d, tn), lambda i, j: (0, j)),
        ],
        out_specs=pl.BlockSpec((tm, tn), lambda i, j: (i, j)),
        out_shape=jax.ShapeDtypeStruct((n, c), out_dtype),
        scratch_shapes=[pltpu.VMEM((tm, d), BF16)],
        compiler_params=_params(("parallel", "arbitrary")),
    )(x, g.reshape(1, d), w)


def _latent_kernel(dc_ref, g_ref, c_ref):
    c_ref[...] = _rms(dc_ref[...], g_ref[...]).astype(BF16)


def _latent(misc, kv_norm):
    n = misc.shape[0]
    tm = _pick_tile(n, 1280)
    blk = (IN_W["rq"] + IN_W["rk"] + IN_W["rg"]) // DSA_LATENT
    return pl.pallas_call(
        _latent_kernel,
        grid=(n // tm,),
        in_specs=[
            pl.BlockSpec((tm, DSA_LATENT), lambda i: (i, blk)),
            pl.BlockSpec((1, DSA_LATENT), lambda i: (0, 0)),
        ],
        out_specs=pl.BlockSpec((tm, DSA_LATENT), lambda i: (i, 0)),
        out_shape=jax.ShapeDtypeStruct((n, DSA_LATENT), BF16),
        compiler_params=_params(("parallel",)),
    )(misc, kv_norm.reshape(1, DSA_LATENT))


def _split3(x):
    hi = x.astype(BF16)
    r = x - hi.astype(F32)
    mid = r.astype(BF16)
    lo = (r - mid.astype(F32)).astype(BF16)
    return hi, mid, lo


def _cum_kernel(s_ref, b_ref, o_ref, carry_ref):
    @pl.when(pl.program_id(1) == 0)
    def _():
        carry_ref[...] = jnp.zeros_like(carry_ref)

    z = s_ref[...] + b_ref[...]
    lf = jnp.minimum(z, 0.0) - jnp.log1p(jnp.exp(-jnp.abs(z)))
    r = lax.broadcasted_iota(jnp.int32, (BLOCK, BLOCK), 0)
    c = lax.broadcasted_iota(jnp.int32, (BLOCK, BLOCK), 1)
    tri = jnp.where(c <= r, 1.0, 0.0).astype(BF16)
    hi, mid, lo = _split3(lf)
    cum = (_dot(tri, lo) + _dot(tri, mid)) + _dot(tri, hi) + carry_ref[...]
    o_ref[...] = cum
    carry_ref[...] = cum[BLOCK - 1:BLOCK, :]


def _cum_logf(misc3, f_bias):
    b, t, _ = misc3.shape
    blk = (IN_W["rq"] + IN_W["rk"] + IN_W["rg"] + IN_W["dc"]) // LANE
    bias = jnp.zeros((1, LANE), F32).at[0, SMALL_FF:SMALL_FF + FOX_HEADS].set(f_bias)
    return pl.pallas_call(
        _cum_kernel,
        grid=(b, t // BLOCK),
        in_specs=[
            pl.BlockSpec((None, BLOCK, LANE), lambda bi, i: (bi, i, blk)),
            pl.BlockSpec((1, LANE), lambda bi, i: (0, 0)),
        ],
        out_specs=pl.BlockSpec((None, BLOCK, LANE), lambda bi, i: (bi, i, 0)),
        out_shape=jax.ShapeDtypeStruct((b, t, LANE), F32),
        scratch_shapes=[pltpu.VMEM((1, LANE), F32)],
        compiler_params=_params(("parallel", "arbitrary")),
    )(misc3, bias)


FAR_BLOCKS = 4


def _dsa_kernel(dq_ref, iqt_ref, wt_ref, ik_ref, c_ref, ct_ref, wuk_ref, wuv_ref, bdiag_ref, bprev_ref,
                o_ref, s_ref, qlat_ref, p_ref, m_ref, l_ref, acc_ref, *, topk, nblk):
    i = pl.program_id(1)
    key_i = lax.broadcasted_iota(jnp.int32, (BLOCK, BLOCK), 0)
    qry_i = lax.broadcasted_iota(jnp.int32, (BLOCK, BLOCK), 1)
    hs = lambda h: slice(h * BLOCK, (h + 1) * BLOCK)

    for h in range(DSA_HEADS):
        ql = _dot_nt(wuk_ref[h], dq_ref[:, h * DSA_HEAD_DIM:(h + 1) * DSA_HEAD_DIM]) * DSA_HEAD_DIM ** -0.5
        qlat_ref[:, hs(h)] = ql.astype(BF16)

    ngrp = (i + SCORE_GROUP) // SCORE_GROUP
    for u in range(1, SCORE_GROUP):
        s_ref[i + u] = jnp.full((BLOCK, BLOCK), INT_MIN, jnp.int32)

    def score_group(g, _):
        j0 = jnp.minimum(g * SCORE_GROUP, nblk - SCORE_GROUP)
        kt = ik_ref[pl.ds(pl.multiple_of(j0 * BLOCK, BLOCK), SCORE_GROUP * BLOCK), :]
        r = _dot(kt, iqt_ref[...])
        t = i * BLOCK + qry_i
        for u in range(SCORE_GROUP):
            acc = jnp.zeros((BLOCK, BLOCK), F32)
            for h in range(IDX_HEADS):
                acc = acc + jnp.maximum(r[u * BLOCK:(u + 1) * BLOCK, hs(h)], 0.0) * wt_ref[h:h + 1, :]
            s = (j0 + u) * BLOCK + key_i
            ok = (s <= t) & ((s >= PAD) | (s == t))
            bits = lax.bitcast_convert_type(acc, jnp.int32)
            bits = jnp.where(bits == INT_MIN, 0, bits)
            key = bits ^ ((bits >> 31) & 0x7FFFFFFF)
            s_ref[j0 + u] = jnp.where(ok, key, INT_MIN)
        return 0

    lax.fori_loop(0, ngrp, score_group, 0)

    n_acc = 8

    def count_ge(cand):
        def body(g, acc):
            tiles = s_ref[pl.ds(g * SCORE_GROUP, SCORE_GROUP)]
            hit = jnp.where(tiles >= cand[None], 1.0, 0.0)
            return acc + jnp.sum(hit.reshape(-1, n_acc, 8, BLOCK), axis=0)

        acc = lax.fori_loop(0, ngrp, body, jnp.zeros((n_acc, 8, BLOCK), F32))
        return jnp.sum(acc, axis=(0, 1)).reshape(1, BLOCK)

    zero = jnp.zeros((1, BLOCK), jnp.int32)
    thr0 = jnp.where(count_ge(zero) >= topk, zero, INT_MIN)

    def bit_step(it, thr):
        cand = thr + (jnp.int32(1) << (30 - it))
        return jnp.where(count_ge(cand) >= topk, cand, thr)

    thr = lax.fori_loop(0, 31, bit_step, thr0)
    thr = jnp.maximum(thr, INT_MIN + 1)

    m_ref[...] = jnp.full(m_ref.shape, NEG, F32)
    l_ref[...] = jnp.zeros_like(l_ref)
    acc_ref[...] = jnp.zeros_like(acc_ref)

    def attend(j0, nb, bias_ref):
        w = nb * BLOCK
        off = pl.multiple_of(j0 * BLOCK, BLOCK)
        lg = _dot(c_ref[pl.ds(off, w), :], qlat_ref[...])
        sel = s_ref[pl.ds(j0, nb)].reshape(w, BLOCK) >= thr
        m_prev = m_ref[...]
        xs = []
        for h in range(DSA_HEADS):
            x = lg[:, hs(h)]
            if bias_ref is not None:
                x = x + bias_ref[h]
            xs.append(jnp.where(sel, x, NEG))
        m_new = jnp.maximum(m_prev, jnp.concatenate([jnp.max(x, axis=0, keepdims=True) for x in xs], axis=0))
        alpha = jnp.exp(m_prev - m_new)
        sums = []
        for h in range(DSA_HEADS):
            p = jnp.exp(xs[h] - m_new[h:h + 1, :])
            sums.append(jnp.sum(p, axis=0, keepdims=True))
            p_ref[0:w, hs(h)] = p.astype(BF16)
        l_ref[...] = alpha * l_ref[...] + jnp.concatenate(sums, axis=0)
        m_ref[...] = m_new
        pv = _dot(ct_ref[:, pl.ds(off, w)], p_ref[0:w, :])
        for h in range(DSA_HEADS):
            acc_ref[:, hs(h)] = acc_ref[:, hs(h)] * alpha[h:h + 1, :] + pv[:, hs(h)]

    n_far = jnp.maximum(i - 1, 0)
    n_big = n_far // FAR_BLOCKS

    def far_big(g, _):
        attend(g * FAR_BLOCKS, FAR_BLOCKS, None)
        return 0

    def far_one(j, _):
        attend(j, 1, None)
        return 0

    lax.fori_loop(0, n_big, far_big, 0)
    lax.fori_loop(n_big * FAR_BLOCKS, n_far, far_one, 0)

    @pl.when(i >= 1)
    def _():
        attend(i - 1, 1, bprev_ref)

    attend(i, 1, bdiag_ref)

    inv_l = 1.0 / l_ref[...]
    for h in range(DSA_HEADS):
        o_lat = (acc_ref[:, hs(h)] * inv_l[h:h + 1, :]).astype(BF16)
        o = _dot_tn(o_lat, wuv_ref[h])
        o_ref[:, h * DSA_HEAD_DIM:(h + 1) * DSA_HEAD_DIM] = o.astype(o_ref.dtype)


def _dsa(pbf3, iqt, wt, c3, ct3, wuk, wuv, bdiag, bprev, topk):
    b, t, _ = pbf3.shape
    nblk = t // BLOCK
    hq = DSA_HEADS * DSA_HEAD_DIM
    ik_blk = 6 * 1024 // LANE
    n_tiles = (nblk + SCORE_GROUP - 1) // SCORE_GROUP * SCORE_GROUP + SCORE_GROUP
    const3 = lambda bi, i: (0, 0, 0)
    return pl.pallas_call(
        functools.partial(_dsa_kernel, topk=topk, nblk=nblk),
        grid=(b, nblk),
        in_specs=[
            pl.BlockSpec((None, BLOCK, hq), lambda bi, i: (bi, i, 0)),
            pl.BlockSpec((None, None, LANE, IDX_HEADS * BLOCK), lambda bi, i: (bi, i, 0, 0)),
            pl.BlockSpec((None, None, IDX_HEADS, BLOCK), lambda bi, i: (bi, i, 0, 0)),
            pl.BlockSpec((None, t, LANE), lambda bi, i: (bi, 0, ik_blk)),
            pl.BlockSpec((None, t, DSA_LATENT), lambda bi, i: (bi, 0, 0)),
            pl.BlockSpec((None, DSA_LATENT, t), lambda bi, i: (bi, 0, 0)),
            pl.BlockSpec((DSA_HEADS, DSA_LATENT, DSA_HEAD_DIM), const3),
            pl.BlockSpec((DSA_HEADS, DSA_LATENT, DSA_HEAD_DIM), const3),
            pl.BlockSpec((DSA_HEADS, BLOCK, BLOCK), const3),
            pl.BlockSpec((DSA_HEADS, BLOCK, BLOCK), const3),
        ],
        out_specs=pl.BlockSpec((None, BLOCK, hq), lambda bi, i: (bi, i, 0)),
        out_shape=jax.ShapeDtypeStruct((b, t, hq), BF16),
        scratch_shapes=[
            pltpu.VMEM((n_tiles, BLOCK, BLOCK), jnp.int32),
            pltpu.VMEM((DSA_LATENT, DSA_HEADS * BLOCK), BF16),
            pltpu.VMEM((FAR_BLOCKS * BLOCK, DSA_HEADS * BLOCK), BF16),
            pltpu.VMEM((DSA_HEADS, BLOCK), F32),
            pltpu.VMEM((DSA_HEADS, BLOCK), F32),
            pltpu.VMEM((DSA_LATENT, DSA_HEADS * BLOCK), F32),
        ],
        compiler_params=_params(("parallel", "arbitrary")),
    )(pbf3, iqt, wt, pbf3, c3, ct3, wuk, wuv, bdiag, bprev)


def _dsa_inputs(pbf3, misc3, w_uk, w_uv):
    b, t, _ = pbf3.shape
    nblk = t // BLOCK
    iq = pbf3[:, :, IN_W["dq"]:IN_W["dq"] + IN_W["iq"]].reshape(b, nblk, BLOCK, IDX_HEADS, IDX_DIM)
    iqt = jnp.transpose(iq, (0, 1, 4, 3, 2)).reshape(b, nblk, IDX_DIM, IDX_HEADS * BLOCK)
    iqt = jnp.pad(iqt, ((0, 0), (0, 0), (0, LANE - IDX_DIM), (0, 0)))
    small = (IN_W["rq"] + IN_W["rk"] + IN_W["rg"] + IN_W["dc"])
    iw = misc3[:, :, small + SMALL_IW:small + SMALL_IW + IDX_HEADS] * (IDX_HEADS ** -0.5 * IDX_DIM ** -0.5)
    wt = jnp.swapaxes(iw.reshape(b, nblk, BLOCK, IDX_HEADS), 2, 3)
    wuk = jnp.transpose(w_uk, (1, 0, 2)).astype(BF16)
    wuv = jnp.transpose(w_uv, (1, 0, 2)).astype(BF16)
    return iqt, wt, wuk, wuv


def _fox_kernel(q_ref, k_ref, v_ref, cq_ref, ck_ref, o_ref, m_ref, l_ref, acc_ref, *, tq):
    i = pl.program_id(2)
    qs = (q_ref[...].astype(F32) * FOX_HEAD_DIM ** -0.5).astype(BF16)
    cq = cq_ref[...]
    m_ref[...] = jnp.full(m_ref.shape, NEG, F32)
    l_ref[...] = jnp.zeros_like(l_ref)
    acc_ref[...] = jnp.zeros_like(acc_ref)

    def tile(j, mode):
        off = pl.multiple_of(j * tq, tq)
        kt = k_ref[pl.ds(off, tq), :]
        vt = v_ref[pl.ds(off, tq), :]
        x = _dot_nt(qs, kt) + cq - ck_ref[:, pl.ds(off, tq)]
        if mode == "pad":
            s = lax.broadcasted_iota(jnp.int32, (1, tq), 1)
            x = jnp.where(s >= PAD, x, NEG)
        elif mode == "diag":
            r = lax.broadcasted_iota(jnp.int32, (tq, tq), 0)
            c = lax.broadcasted_iota(jnp.int32, (tq, tq), 1)
            x = jnp.where((c <= r) & ((c + j * tq >= PAD) | (c == r)), x, NEG)
        m_prev = m_ref[...]
        m_new = jnp.maximum(m_prev, jnp.max(x, axis=1, keepdims=True))
        alpha = jnp.exp(m_prev - m_new)
        p = jnp.exp(x - m_new[:, :1])
        l_ref[...] = alpha * l_ref[...] + jnp.sum(p, axis=1, keepdims=True)
        acc_ref[...] = acc_ref[...] * alpha + _dot(p.astype(BF16), vt)
        m_ref[...] = m_new

    @pl.when(i > 0)
    def _():
        tile(0, "pad")

    def mid(j, _):
        tile(j, "none")
        return 0

    lax.fori_loop(1, i, mid, 0)
    tile(i, "diag")
    o_ref[...] = (acc_ref[...] * (1.0 / l_ref[...])).astype(o_ref.dtype)


def _fox(pbf3, cq, ck):
    b, t, _ = pbf3.shape
    nblk = t // BLOCK
    tq = BLOCK * max(k for k in (1, 5) if nblk % k == 0)
    d = FOX_HEAD_DIM
    qb, kb, vb = 2 * 1024 // d, 3 * 1024 // d, 4 * 1024 // d
    return pl.pallas_call(
        functools.partial(_fox_kernel, tq=tq),
        grid=(b, FOX_HEADS, t // tq),
        in_specs=[
            pl.BlockSpec((None, tq, d), lambda bi, h, i: (bi, i, qb + h)),
            pl.BlockSpec((None, t, d), lambda bi, h, i: (bi, 0, kb + h)),
            pl.BlockSpec((None, t, d), lambda bi, h, i: (bi, 0, vb + h)),
            pl.BlockSpec((None, None, tq, 1), lambda bi, h, i: (bi, h, i, 0)),
            pl.BlockSpec((None, None, 1, t), lambda bi, h, i: (bi, h, 0, 0)),
        ],
        out_specs=pl.BlockSpec((None, tq, d), lambda bi, h, i: (bi, i, h)),
        out_shape=jax.ShapeDtypeStruct((b, t, FOX_HEADS * d), BF16),
        scratch_shapes=[pltpu.VMEM((tq, d), F32), pltpu.VMEM((tq, d), F32), pltpu.VMEM((tq, d), F32)],
        compiler_params=_params(("parallel", "parallel", "arbitrary")),
    )(pbf3, pbf3, pbf3, cq, ck)


def _ret_kernel(rq_ref, rk_ref, rv_ref, rg_ref, cos_ref, sin_ref, dq_ref, dk_ref, decay_ref, cd_ref, gn_ref,
                o_ref, state_ref):
    i = pl.program_id(1)

    @pl.when(i == 0)
    def _():
        state_ref[...] = jnp.zeros_like(state_ref)

    half = RET_HEADS * RET_QK_DIM // 2
    cos = cos_ref[...]
    sin = sin_ref[...]

    def rope(x):
        x1, x2 = x[:, :half], x[:, half:]
        return jnp.concatenate([x1 * cos - x2 * sin, x1 * sin + x2 * cos], axis=1)

    q = rope(rq_ref[...])
    k = rope(rk_ref[...]) * RET_QK_DIM ** -0.5
    pos = i * BLOCK + lax.broadcasted_iota(jnp.int32, (BLOCK, 1), 0)
    k = jnp.where(pos >= PAD, k, 0.0)
    qb = q.astype(BF16)
    kb = k.astype(BF16)
    q_dec = (q * dq_ref[...]).astype(BF16)
    k_dec = (k * dk_ref[...]).astype(BF16)
    head_of_col = (lax.broadcasted_iota(jnp.int32, (1, 2 * half), 1) % half) // (RET_QK_DIM // 2)
    zero = jnp.zeros((), BF16)
    for h in range(RET_HEADS):
        mine = head_of_col == h
        vh = rv_ref[:, h * RET_V_DIM:(h + 1) * RET_V_DIM]
        s = (_dot_nt(jnp.where(mine, qb, zero), kb) * decay_ref[h]).astype(BF16)
        st = state_ref[h]
        o = _dot(s, vh) + _dot(jnp.where(mine, q_dec, zero), st.astype(BF16))
        state_ref[h] = cd_ref[h] * st + _dot_tn(k_dec, vh)
        mu = jnp.mean(o, axis=1, keepdims=True)
        var = jnp.mean(jnp.square(o - mu), axis=1, keepdims=True)
        sl = slice(h * RET_V_DIM, (h + 1) * RET_V_DIM)
        o = (o - mu) * lax.rsqrt(var + GN_EPS) * gn_ref[:, sl]
        g = rg_ref[:, sl]
        o_ref[:, sl] = (g * jax.nn.sigmoid(g) * o).astype(o_ref.dtype)


def _retention(misc3, pbf3, cos, sin, dq_tab, dk_tab, decay, cd, gn_w):
    b, t, _ = misc3.shape
    qk = RET_HEADS * RET_QK_DIM
    vw = RET_HEADS * RET_V_DIM
    const2 = lambda bi, i: (0, 0)
    const3 = lambda bi, i: (0, 0, 0)
    return pl.pallas_call(
        _ret_kernel,
        grid=(b, t // BLOCK),
        in_specs=[
            pl.BlockSpec((None, BLOCK, qk), lambda bi, i: (bi, i, 0)),
            pl.BlockSpec((None, BLOCK, qk), lambda bi, i: (bi, i, 1)),
            pl.BlockSpec((None, BLOCK, vw), lambda bi, i: (bi, i, 5)),
            pl.BlockSpec((None, BLOCK, vw), lambda bi, i: (bi, i, 1)),
            pl.BlockSpec((BLOCK, qk // 2), lambda bi, i: (i, 0)),
            pl.BlockSpec((BLOCK, qk // 2), lambda bi, i: (i, 0)),
            pl.BlockSpec((BLOCK, qk), const2),
            pl.BlockSpec((BLOCK, qk), const2),
            pl.BlockSpec((RET_HEADS, BLOCK, BLOCK), const3),
            pl.BlockSpec((RET_HEADS, 1, RET_V_DIM), const3),
            pl.BlockSpec((1, vw), const2),
        ],
        out_specs=pl.BlockSpec((None, BLOCK, vw), lambda bi, i: (bi, i, 0)),
        out_shape=jax.ShapeDtypeStruct((b, t, vw), BF16),
        scratch_shapes=[pltpu.VMEM((RET_HEADS, qk, RET_V_DIM), F32)],
        compiler_params=_params(("parallel", "arbitrary")),
    )(misc3, misc3, pbf3, misc3, cos, sin, dq_tab, dk_tab, decay, cd, gn_w.reshape(1, vw))


def _merge_kernel(b0_ref, b1_ref, b2_ref, g0_ref, g1_ref, g2_ref, w_ref, o_ref):
    acc = jax.nn.sigmoid(g0_ref[...]) * _dot(b0_ref[...], w_ref[0])
    acc = acc + jax.nn.sigmoid(g1_ref[...]) * _dot(b1_ref[...], w_ref[1])
    acc = acc + jax.nn.sigmoid(g2_ref[...]) * _dot(b2_ref[...], w_ref[2])
    o_ref[...] = acc.astype(o_ref.dtype)


def _merge(b0, b1, b2, gates, w_branch):
    n, bw = b0.shape
    d = w_branch.shape[2]
    tm = _pick_tile(n, 640)
    tn = _pick_tile(d, 512, LANE)
    nj = d // tn
    bspec = pl.BlockSpec((tm, bw), lambda i, j: (i, 0))
    gspec = lambda r: pl.BlockSpec((tm, tn), lambda i, j: (i, r * nj + j))
    return pl.pallas_call(
        _merge_kernel,
        grid=(n // tm, nj),
        in_specs=[bspec, bspec, bspec, gspec(0), gspec(1), gspec(2),
                  pl.BlockSpec((N_BRANCH, bw, tn), lambda i, j: (0, 0, j))],
        out_specs=pl.BlockSpec((tm, tn), lambda i, j: (i, j)),
        out_shape=jax.ShapeDtypeStruct((n, d), BF16),
        compiler_params=_params(("parallel", "arbitrary")),
    )(b0, b1, b2, gates, gates, gates, w_branch)


def _resproj_kernel(a_ref, w_ref, r_ref, o_ref):
    o_ref[...] = r_ref[...] + _dot(a_ref[...], w_ref[...])


def _resproj(a, w, res):
    n, k = a.shape
    d = w.shape[1]
    tm = _pick_tile(n, 640)
    tn = _pick_tile(d, 512, LANE)
    return pl.pallas_call(
        _resproj_kernel,
        grid=(n // tm, d // tn),
        in_specs=[
            pl.BlockSpec((tm, k), lambda i, j: (i, 0)),
            pl.BlockSpec((k, tn), lambda i, j: (0, j)),
            pl.BlockSpec((tm, tn), lambda i, j: (i, j)),
        ],
        out_specs=pl.BlockSpec((tm, tn), lambda i, j: (i, j)),
        out_shape=jax.ShapeDtypeStruct((n, d), F32),
        compiler_params=_params(("parallel", "arbitrary")),
    )(a, w, res)


def _t5_bucket(dist):
    d = jnp.maximum(dist, 1).astype(F32)
    large = T5_MAX_EXACT + (jnp.log(d / T5_MAX_EXACT) / math.log(T5_MAX_DIST / T5_MAX_EXACT)
                            * (T5_BUCKETS - T5_MAX_EXACT)).astype(jnp.int32)
    large = jnp.minimum(large, T5_BUCKETS - 1)
    return jnp.where(dist < T5_MAX_EXACT, dist, large)


def _t5_bias_tiles(t5_table):
    k = jnp.arange(BLOCK, dtype=jnp.int32)[:, None]
    q = jnp.arange(BLOCK, dtype=jnp.int32)[None, :]
    table = t5_table.astype(F32)
    far = table[T5_BUCKETS - 1]
    bdiag = jnp.moveaxis(table[_t5_bucket(jnp.maximum(q - k, 0))] - far, -1, 0)
    bprev = jnp.moveaxis(table[_t5_bucket(BLOCK + q - k)] - far, -1, 0)
    return bdiag, bprev


def _rope_tables(t):
    half = RET_QK_DIM // 2
    inv = ROPE_BASE ** (-jnp.arange(half, dtype=F32) / half)
    ang = jnp.arange(t, dtype=jnp.int32).astype(F32)[:, None] * inv[None, :]
    return jnp.tile(jnp.cos(ang), (1, RET_HEADS)), jnp.tile(jnp.sin(ang), (1, RET_HEADS))


def _decay_tables():
    log_gamma = jnp.log1p(-(2.0 ** (-5.0 - jnp.arange(RET_HEADS, dtype=F32))))
    i = jnp.arange(BLOCK, dtype=F32)
    gap = i[:, None] - i[None, :]
    decay = jnp.where(gap[None] >= 0, jnp.exp(jnp.maximum(gap, 0.0)[None] * log_gamma[:, None, None]), 0.0)
    head_of_col = (np.arange(RET_HEADS * RET_QK_DIM) % (RET_HEADS * RET_QK_DIM // 2)) // (RET_QK_DIM // 2)
    lg_col = log_gamma[head_of_col]
    dk_tab = jnp.exp((BLOCK - 1 - i)[:, None] * lg_col[None, :])
    dq_tab = jnp.exp((i + 1.0)[:, None] * lg_col[None, :])
    cd = jnp.broadcast_to(jnp.exp(BLOCK * log_gamma)[:, None, None], (RET_HEADS, 1, RET_V_DIM))
    return decay.astype(F32), dq_tab, dk_tab, cd


def _rope_perm():
    half = RET_QK_DIM // 2
    h = np.arange(RET_HEADS)[:, None]
    f = np.arange(half)[None, :]
    first = (h * RET_QK_DIM + f).reshape(-1)
    return np.concatenate([first, first + half])


def _split_w_in(w_in):
    cols = lambda name: w_in[:, IN_OFF[name]:IN_OFF[name] + IN_W[name]]
    zeros = lambda w: jnp.zeros((w_in.shape[0], w), w_in.dtype)
    perm = _rope_perm()
    w_bf = jnp.concatenate([cols("dq"), cols("iq"), cols("fq"), cols("fk"), cols("fv"), cols("rv"),
                            cols("ik"), zeros(LANE - IDX_DIM)], axis=1).astype(BF16)
    w_misc = jnp.concatenate([cols("rq")[:, perm], cols("rk")[:, perm], cols("rg"), cols("dc"),
                              cols("iw"), cols("ff"), zeros(LANE - IDX_HEADS - FOX_HEADS),
                              zeros(LANE)], axis=1).astype(BF16)
    assert w_bf.shape[1] == BF_COLS and w_misc.shape[1] == MISC_COLS
    return w_bf, w_misc, cols("gates").astype(BF16)


def _mixer(x, b, t, mix_norm, w_in, kv_norm, w_uk, w_uv, f_bias, gn_w, w_branch, w_out, tables, topk):
    n = b * t
    nblk = t // BLOCK
    bias_tiles, rope_tabs, decay_tabs = tables
    w_bf, w_misc, w_gates = _split_w_in(w_in)
    pbf = _proj(x, mix_norm, w_bf, BF16)
    misc = _proj(x, mix_norm, w_misc, F32)
    gates = _proj(x, mix_norm, w_gates, F32)
    pbf3 = pbf.reshape(b, t, BF_COLS)
    misc3 = misc.reshape(b, t, MISC_COLS)

    c3 = _latent(misc, kv_norm).reshape(b, t, DSA_LATENT)
    iqt, wt, wuk, wuv = _dsa_inputs(pbf3, misc3, w_uk, w_uv)
    dsa_o = _dsa(pbf3, iqt, wt, c3, jnp.swapaxes(c3, 1, 2), wuk, wuv, *bias_tiles, topk)

    cum = _cum_logf(misc3, f_bias)[:, :, SMALL_FF:SMALL_FF + FOX_HEADS]
    cum_h = jnp.swapaxes(cum, 1, 2)
    fox_o = _fox(pbf3, cum_h[..., None], cum_h[:, :, None, :])

    ret_o = _retention(misc3, pbf3, *rope_tabs, *decay_tabs, gn_w)

    merged = _merge(dsa_o.reshape(n, -1), fox_o.reshape(n, -1), ret_o.reshape(n, -1), gates,
                    w_branch.astype(BF16))
    return _resproj(merged, w_out.astype(BF16), x)


def kernel(x, meta_tokens, t5_table, ffn1_norm, ffn1_w1, ffn1_w3, ffn1_w2, mix_norm, w_in, dsa_kv_norm,
           dsa_w_uk, dsa_w_uv, fox_f_bias, ret_gn_w, w_branch, w_out, ffn2_norm, ffn2_w1, ffn2_w3, ffn2_w2,
           final_norm):
    b, seq, d = x.shape
    assert d == D_MODEL and seq % BLOCK == 0
    t = seq + BLOCK
    depth = ffn1_norm.shape[0]
    topk = min(INDEX_TOPK, seq // 4)
    h = jnp.concatenate([
        jnp.zeros((b, PAD, d), x.dtype),
        jnp.broadcast_to(meta_tokens.astype(x.dtype)[None], (b, N_META, d)),
        x,
    ], axis=1).reshape(b * t, d)
    cos, sin = _rope_tables(t)
    decay, dq_tab, dk_tab, cd = _decay_tables()
    tables = (_t5_bias_tiles(t5_table), (cos, sin), (dq_tab, dk_tab, decay, cd))
    for l in range(depth):
        h = _ffn(h, ffn1_norm[l], ffn1_w1[l].astype(BF16), ffn1_w3[l].astype(BF16), ffn1_w2[l].astype(BF16))
        h = _mixer(h, b, t, mix_norm[l], w_in[l], dsa_kv_norm[l], dsa_w_uk[l], dsa_w_uv[l], fox_f_bias[l],
                   ret_gn_w[l], w_branch[l], w_out[l], tables, topk)
        h = _ffn(h, ffn2_norm[l], ffn2_w1[l].astype(BF16), ffn2_w3[l].astype(BF16), ffn2_w2[l].astype(BF16),
                 final_g=final_norm if l == depth - 1 else None)
    return h.reshape(b, t, d)[:, BLOCK:]
```

```python
import functools
import math

import numpy as np
import jax
import jax.numpy as jnp
from jax import lax
from jax.experimental import pallas as pl
from jax.experimental.pallas import tpu as pltpu

F32 = jnp.float32
BF16 = jnp.bfloat16

LANE = 128
VMEM_LIMIT = 56 * 1024 * 1024

D_MODEL = 2048
N_META = 16
BLOCK = 128
PAD = BLOCK - N_META
RMS_EPS = 1e-6
GN_EPS = 1e-5
NEG = -1e30

DSA_HEADS = 8
DSA_HEAD_DIM = 128
DSA_LATENT = 256
IDX_HEADS = 16
IDX_DIM = 64
INDEX_TOPK = 256
FOX_HEADS = 8
FOX_HEAD_DIM = 128
RET_HEADS = 8
RET_QK_DIM = 64
RET_V_DIM = 128
ROPE_BASE = 10000.0
BRANCH_WIDTH = 1024
N_BRANCH = 3
T5_BUCKETS = 32
T5_MAX_EXACT = 16
T5_MAX_DIST = 128

IN_WIDTHS = (1024, 256, 1024, 64, 16, 1024, 1024, 1024, 8, 512, 512, 1024, 1024, 6144)
IN_NAMES = ("dq", "dc", "iq", "ik", "iw", "fq", "fk", "fv", "ff", "rq", "rk", "rv", "rg", "gates")
IN_OFF = dict(zip(IN_NAMES, np.cumsum((0,) + IN_WIDTHS[:-1]).tolist()))
IN_W = dict(zip(IN_NAMES, IN_WIDTHS))

INT_MIN = -2 ** 31
LOG2E = math.log2(math.e)
SCORE_GROUP = 4

BF_COLS = 6 * 1024 + LANE
MISC_COLS = 2560
SMALL_IW = 0
SMALL_FF = IDX_HEADS


def _dot(a, b):
    return jnp.dot(a, b, preferred_element_type=F32)


def _dot_nt(a, b):
    return lax.dot_general(a, b, (((1,), (1,)), ((), ())), preferred_element_type=F32)


def _dot_tn(a, b):
    return lax.dot_general(a, b, (((0,), (0,)), ((), ())), preferred_element_type=F32)


def _pick_tile(n, target, mult=8):
    best = None
    for t in range(mult, min(n, target) + 1, mult):
        if n % t == 0:
            best = t
    assert best is not None, (n, target, mult)
    return best


def _params(sem):
    return pltpu.CompilerParams(dimension_semantics=sem, vmem_limit_bytes=VMEM_LIMIT)


def _rms(x, g):
    return x * lax.rsqrt(jnp.mean(x * x, axis=-1, keepdims=True) + RMS_EPS) * g


def _ffn_kernel(*refs, nj, final):
    if final:
        x_ref, g_ref, w1_ref, w3_ref, w2_ref, fg_ref, o_ref, xn_ref, acc_ref = refs
    else:
        x_ref, g_ref, w1_ref, w3_ref, w2_ref, o_ref, xn_ref, acc_ref = refs
    j = pl.program_id(1)

    @pl.when(j == 0)
    def _():
        xn_ref[...] = _rms(x_ref[...], g_ref[...]).astype(BF16)
        acc_ref[...] = jnp.zeros_like(acc_ref)

    xn = xn_ref[...]
    h1 = _dot(xn, w1_ref[...])
    h3 = _dot(xn, w3_ref[...])
    g = (h1 * jax.nn.sigmoid(h1) * h3).astype(BF16)
    acc_ref[...] += _dot(g, w2_ref[...])

    @pl.when(j == nj - 1)
    def _():
        y = x_ref[...] + 0.5 * acc_ref[...]
        if final:
            y = _rms(y, fg_ref[...])
        o_ref[...] = y


def _ffn(x, g, w1, w3, w2, final_g=None):
    n, d = x.shape
    f = w1.shape[1]
    tm = _pick_tile(n, 640)
    tf = _pick_tile(f, 512, LANE)
    nj = f // tf
    final = final_g is not None
    in_specs = [
        pl.BlockSpec((tm, d), lambda i, j: (i, 0)),
        pl.BlockSpec((1, d), lambda i, j: (0, 0)),
        pl.BlockSpec((d, tf), lambda i, j: (0, j)),
        pl.BlockSpec((d, tf), lambda i, j: (0, j)),
        pl.BlockSpec((tf, d), lambda i, j: (j, 0)),
    ]
    args = [x, g.reshape(1, d), w1, w3, w2]
    if final:
        in_specs.append(pl.BlockSpec((1, d), lambda i, j: (0, 0)))
        args.append(final_g.reshape(1, d))
    return pl.pallas_call(
        functools.partial(_ffn_kernel, nj=nj, final=final),
        grid=(n // tm, nj),
        in_specs=in_specs,
        out_specs=pl.BlockSpec((tm, d), lambda i, j: (i, 0)),
        out_shape=jax.ShapeDtypeStruct((n, d), F32),
        scratch_shapes=[pltpu.VMEM((tm, d), BF16), pltpu.VMEM((tm, d), F32)],
        compiler_params=_params(("parallel", "arbitrary")),
    )(*args)


def _proj_kernel(x_ref, g_ref, w_ref, o_ref, xn_ref):
    @pl.when(pl.program_id(1) == 0)
    def _():
        xn_ref[...] = _rms(x_ref[...], g_ref[...]).astype(BF16)

    o_ref[...] = _dot(xn_ref[...], w_ref[...]).astype(o_ref.dtype)


def _proj(x, g, w, out_dtype):
    n, d = x.shape
    c = w.shape[1]
    tm = _pick_tile(n, 640)
    tn = _pick_tile(c, 1024, LANE)
    return pl.pallas_call(
        _proj_kernel,
        grid=(n // tm, c // tn),
        in_specs=[
            pl.BlockSpec((tm, d), lambda i, j: (i, 0)),
            pl.BlockSpec((1, d), lambda i, j: (0, 0)),
            pl.BlockSpec((d, tn), lambda i, j: (0, j)),
        ],
        out_specs=pl.BlockSpec((tm, tn), lambda i, j: (i, j)),
        out_shape=jax.ShapeDtypeStruct((n, c), out_dtype),
        scratch_shapes=[pltpu.VMEM((tm, d), BF16)],
        compiler_params=_params(("parallel", "arbitrary")),
    )(x, g.reshape(1, d), w)


def _latent_kernel(dc_ref, g_ref, c_ref):
    c_ref[...] = _rms(dc_ref[...], g_ref[...]).astype(BF16)


def _latent(misc, kv_norm):
    n = misc.shape[0]
    tm = _pick_tile(n, 1280)
    blk = (IN_W["rq"] + IN_W["rk"] + IN_W["rg"]) // DSA_LATENT
    return pl.pallas_call(
        _latent_kernel,
        grid=(n // tm,),
        in_specs=[
            pl.BlockSpec((tm, DSA_LATENT), lambda i: (i, blk)),
            pl.BlockSpec((1, DSA_LATENT), lambda i: (0, 0)),
        ],
        out_specs=pl.BlockSpec((tm, DSA_LATENT), lambda i: (i, 0)),
        out_shape=jax.ShapeDtypeStruct((n, DSA_LATENT), BF16),
        compiler_params=_params(("parallel",)),
    )(misc, kv_norm.reshape(1, DSA_LATENT))


def _split3(x):
    hi = x.astype(BF16)
    r = x - hi.astype(F32)
    mid = r.astype(BF16)
    lo = (r - mid.astype(F32)).astype(BF16)
    return hi, mid, lo


def _cum_kernel(s_ref, b_ref, o_ref, carry_ref):
    @pl.when(pl.program_id(1) == 0)
    def _():
        carry_ref[...] = jnp.zeros_like(carry_ref)

    z = s_ref[...] + b_ref[...]
    lf = jnp.minimum(z, 0.0) - jnp.log1p(jnp.exp(-jnp.abs(z)))
    r = lax.broadcasted_iota(jnp.int32, (BLOCK, BLOCK), 0)
    c = lax.broadcasted_iota(jnp.int32, (BLOCK, BLOCK), 1)
    tri = jnp.where(c <= r, 1.0, 0.0).astype(BF16)
    hi, mid, lo = _split3(lf)
    cum = (_dot(tri, lo) + _dot(tri, mid)) + _dot(tri, hi) + carry_ref[...]
    o_ref[...] = cum
    carry_ref[...] = cum[BLOCK - 1:BLOCK, :]


def _cum_logf(misc3, f_bias):
    b, t, _ = misc3.shape
    blk = (IN_W["rq"] + IN_W["rk"] + IN_W["rg"] + IN_W["dc"]) // LANE
    bias = jnp.zeros((1, LANE), F32).at[0, SMALL_FF:SMALL_FF + FOX_HEADS].set(f_bias)
    return pl.pallas_call(
        _cum_kernel,
        grid=(b, t // BLOCK),
        in_specs=[
            pl.BlockSpec((None, BLOCK, LANE), lambda bi, i: (bi, i, blk)),
            pl.BlockSpec((1, LANE), lambda bi, i: (0, 0)),
        ],
        out_specs=pl.BlockSpec((None, BLOCK, LANE), lambda bi, i: (bi, i, 0)),
        out_shape=jax.ShapeDtypeStruct((b, t, LANE), F32),
        scratch_shapes=[pltpu.VMEM((1, LANE), F32)],
        compiler_params=_params(("parallel", "arbitrary")),
    )(misc3, bias)


FAR_BLOCKS = 4


def _dsa_kernel(dq_ref, iqt_ref, wt_ref, ik_ref, c_ref, ct_ref, wuk_ref, wuv_ref, bias_ref,
                o_ref, s_ref, qlat_ref, p_ref, m_ref, l_ref, acc_ref, lg0_ref, lg1_ref, lgn_ref, *, topk, nblk):
    i = pl.program_id(1)
    key_i = lax.broadcasted_iota(jnp.int32, (BLOCK, BLOCK), 0)
    qry_i = lax.broadcasted_iota(jnp.int32, (BLOCK, BLOCK), 1)
    hs = lambda h: slice(h * BLOCK, (h + 1) * BLOCK)

    for h in range(DSA_HEADS):
        ql = _dot_nt(wuk_ref[h], dq_ref[:, h * DSA_HEAD_DIM:(h + 1) * DSA_HEAD_DIM]) * (DSA_HEAD_DIM ** -0.5 * LOG2E)
        qlat_ref[:, hs(h)] = ql.astype(BF16)

    ngrp = (i + SCORE_GROUP) // SCORE_GROUP
    for u in range(1, SCORE_GROUP):
        s_ref[i + u] = jnp.full((BLOCK, BLOCK), INT_MIN, jnp.int32)

    def score_group(g, _):
        j0 = jnp.minimum(g * SCORE_GROUP, nblk - SCORE_GROUP)
        kt = ik_ref[pl.ds(pl.multiple_of(j0 * BLOCK, BLOCK), SCORE_GROUP * BLOCK), :]
        r = _dot(kt, iqt_ref[...])
        t = i * BLOCK + qry_i
        for u in range(SCORE_GROUP):
            acc = jnp.zeros((BLOCK, BLOCK), F32)
            for h in range(IDX_HEADS):
                acc = acc + jnp.maximum(r[u * BLOCK:(u + 1) * BLOCK, hs(h)], 0.0) * wt_ref[h:h + 1, :]
            s = (j0 + u) * BLOCK + key_i
            ok = (s <= t) & ((s >= PAD) | (s == t))
            bits = lax.bitcast_convert_type(acc, jnp.int32)
            bits = jnp.where(bits == INT_MIN, 0, bits)
            key = bits ^ ((bits >> 31) & 0x7FFFFFFF)
            s_ref[j0 + u] = jnp.where(ok, key, INT_MIN)
        return 0

    lax.fori_loop(0, ngrp, score_group, 0)

    n_acc = 8

    def count_ge(cand):
        def body(g, acc):
            tiles = s_ref[pl.ds(g * SCORE_GROUP, SCORE_GROUP)]
            hit = jnp.where(tiles >= cand[None], 1.0, 0.0)
            return acc + jnp.sum(hit.reshape(-1, n_acc, 8, BLOCK), axis=0)

        acc = lax.fori_loop(0, ngrp, body, jnp.zeros((n_acc, 8, BLOCK), F32))
        return jnp.sum(acc, axis=(0, 1)).reshape(1, BLOCK)

    zero = jnp.zeros((1, BLOCK), jnp.int32)
    thr0 = jnp.where(count_ge(zero) >= topk, zero, INT_MIN)

    def bit_step(it, thr):
        cand = thr + (jnp.int32(1) << (30 - it))
        return jnp.where(count_ge(cand) >= topk, cand, thr)

    thr = lax.fori_loop(0, 31, bit_step, thr0)
    thr = jnp.maximum(thr, INT_MIN + 1)

    m_ref[...] = jnp.full(m_ref.shape, NEG, F32)
    l_ref[...] = jnp.zeros_like(l_ref)
    acc_ref[...] = jnp.zeros_like(acc_ref)

    def logits(j0, nb, dst_ref):
        w = nb * BLOCK
        dst_ref[0:w, :] = _dot(c_ref[pl.ds(pl.multiple_of(j0 * BLOCK, BLOCK), w), :], qlat_ref[...])

    def softmax_pv(j0, nb, lg_ref, bias_off=None):
        w = nb * BLOCK
        ckt = ct_ref[:, pl.ds(pl.multiple_of(j0 * BLOCK, BLOCK), w)]
        masked = jnp.where(s_ref[pl.ds(j0, nb)].reshape(w, BLOCK) >= thr, 0.0, NEG)
        for pair in range(DSA_HEADS // 2):
            cols = slice(2 * pair * BLOCK, 2 * (pair + 1) * BLOCK)
            alphas = []
            for h in (2 * pair, 2 * pair + 1):
                x = lg_ref[0:w, hs(h)] + masked
                if bias_off is not None:
                    x = x + bias_ref[h, pl.ds(bias_off, w), :]
                m_prev = m_ref[h:h + 1, :]
                m_new = jnp.maximum(m_prev, jnp.max(x, axis=0, keepdims=True))
                alpha = jnp.exp2(m_prev - m_new)
                p = jnp.exp2(x - m_new)
                l_ref[h:h + 1, :] = alpha * l_ref[h:h + 1, :] + jnp.sum(p, axis=0, keepdims=True)
                m_ref[h:h + 1, :] = m_new
                p_ref[0:w, hs(h)] = p.astype(BF16)
                alphas.append(alpha)
            pv = _dot(ckt, p_ref[0:w, cols])
            for h, alpha in zip((2 * pair, 2 * pair + 1), alphas):
                acc_ref[:, hs(h)] = acc_ref[:, hs(h)] * alpha + pv[:, (h % 2) * BLOCK:(h % 2 + 1) * BLOCK]

    near0 = jnp.maximum(i - 1, 0)
    logits(near0, 2, lgn_ref)

    n_far = jnp.maximum(i - 1, 0)
    n_big = n_far // FAR_BLOCKS
    big = lambda g: g * FAR_BLOCKS

    @pl.when(n_big > 0)
    def _():
        logits(0, FAR_BLOCKS, lg0_ref)

        def pair_of_tiles(g, _):
            logits(big(2 * g + 1), FAR_BLOCKS, lg1_ref)
            softmax_pv(big(2 * g), FAR_BLOCKS, lg0_ref)
            logits(big(jnp.minimum(2 * g + 2, n_big - 1)), FAR_BLOCKS, lg0_ref)
            softmax_pv(big(2 * g + 1), FAR_BLOCKS, lg1_ref)
            return 0

        lax.fori_loop(0, n_big // 2, pair_of_tiles, 0)

        @pl.when(n_big % 2 == 1)
        def _():
            softmax_pv(big(n_big - 1), FAR_BLOCKS, lg0_ref)

    def far_one(j, _):
        logits(j, 1, lg0_ref)
        softmax_pv(j, 1, lg0_ref)
        return 0

    lax.fori_loop(n_big * FAR_BLOCKS, n_far, far_one, 0)
    softmax_pv(near0, 2, lgn_ref, bias_off=pl.multiple_of(jnp.where(i >= 1, 0, BLOCK), BLOCK))

    inv_l = 1.0 / l_ref[...]
    for h in range(DSA_HEADS):
        o_lat = (acc_ref[:, hs(h)] * inv_l[h:h + 1, :]).astype(BF16)
        o = _dot_tn(o_lat, wuv_ref[h])
        o_ref[:, h * DSA_HEAD_DIM:(h + 1) * DSA_HEAD_DIM] = o.astype(o_ref.dtype)


def _dsa(pbf3, iqt, wt, c3, ct3, wuk, wuv, bias, topk):
    b, t, _ = pbf3.shape
    nblk = t // BLOCK
    hq = DSA_HEADS * DSA_HEAD_DIM
    ik_blk = 6 * 1024 // LANE
    n_tiles = (nblk + SCORE_GROUP - 1) // SCORE_GROUP * SCORE_GROUP + SCORE_GROUP
    const3 = lambda bi, i: (0, 0, 0)
    return pl.pallas_call(
        functools.partial(_dsa_kernel, topk=topk, nblk=nblk),
        grid=(b, nblk),
        in_specs=[
            pl.BlockSpec((None, BLOCK, hq), lambda bi, i: (bi, i, 0)),
            pl.BlockSpec((None, None, LANE, IDX_HEADS * BLOCK), lambda bi, i: (bi, i, 0, 0)),
            pl.BlockSpec((None, None, IDX_HEADS, BLOCK), lambda bi, i: (bi, i, 0, 0)),
            pl.BlockSpec((None, t, LANE), lambda bi, i: (bi, 0, ik_blk)),
            pl.BlockSpec((None, t, DSA_LATENT), lambda bi, i: (bi, 0, 0)),
            pl.BlockSpec((None, DSA_LATENT, t), lambda bi, i: (bi, 0, 0)),
            pl.BlockSpec((DSA_HEADS, DSA_LATENT, DSA_HEAD_DIM), const3),
            pl.BlockSpec((DSA_HEADS, DSA_LATENT, DSA_HEAD_DIM), const3),
            pl.BlockSpec((DSA_HEADS, 3 * BLOCK, BLOCK), const3),
        ],
        out_specs=pl.BlockSpec((None, BLOCK, hq), lambda bi, i: (bi, i, 0)),
        out_shape=jax.ShapeDtypeStruct((b, t, hq), BF16),
        scratch_shapes=[
            pltpu.VMEM((n_tiles, BLOCK, BLOCK), jnp.int32),
            pltpu.VMEM((DSA_LATENT, DSA_HEADS * BLOCK), BF16),
            pltpu.VMEM((FAR_BLOCKS * BLOCK, DSA_HEADS * BLOCK), BF16),
            pltpu.VMEM((DSA_HEADS, BLOCK), F32),
            pltpu.VMEM((DSA_HEADS, BLOCK), F32),
            pltpu.VMEM((DSA_LATENT, DSA_HEADS * BLOCK), F32),
            pltpu.VMEM((FAR_BLOCKS * BLOCK, DSA_HEADS * BLOCK), F32),
            pltpu.VMEM((FAR_BLOCKS * BLOCK, DSA_HEADS * BLOCK), F32),
            pltpu.VMEM((2 * BLOCK, DSA_HEADS * BLOCK), F32),
        ],
        compiler_params=_params(("parallel", "arbitrary")),
    )(pbf3, iqt, wt, pbf3, c3, ct3, wuk, wuv, bias)


def _dsa_inputs(pbf3, misc3, w_uk, w_uv):
    b, t, _ = pbf3.shape
    nblk = t // BLOCK
    iq = pbf3[:, :, IN_W["dq"]:IN_W["dq"] + IN_W["iq"]].reshape(b, nblk, BLOCK, IDX_HEADS, IDX_DIM)
    iqt = jnp.transpose(iq, (0, 1, 4, 3, 2)).reshape(b, nblk, IDX_DIM, IDX_HEADS * BLOCK)
    iqt = jnp.pad(iqt, ((0, 0), (0, 0), (0, LANE - IDX_DIM), (0, 0)))
    small = (IN_W["rq"] + IN_W["rk"] + IN_W["rg"] + IN_W["dc"])
    iw = misc3[:, :, small + SMALL_IW:small + SMALL_IW + IDX_HEADS] * (IDX_HEADS ** -0.5 * IDX_DIM ** -0.5)
    wt = jnp.swapaxes(iw.reshape(b, nblk, BLOCK, IDX_HEADS), 2, 3)
    wuk = jnp.transpose(w_uk, (1, 0, 2)).astype(BF16)
    wuv = jnp.transpose(w_uv, (1, 0, 2)).astype(BF16)
    return iqt, wt, wuk, wuv


def _fox_kernel(qt_ref, k_ref, vt_ref, cq_ref, ck_ref, o_ref, m_ref, l_ref, acc_ref, lg_ref, *, tq):
    i = pl.program_id(2)
    cq = cq_ref[...] * LOG2E
    m_ref[...] = jnp.full(m_ref.shape, NEG, F32)
    l_ref[...] = jnp.zeros_like(l_ref)
    acc_ref[...] = jnp.zeros_like(acc_ref)

    def qk(j, slot):
        lg_ref[slot] = _dot(k_ref[pl.ds(pl.multiple_of(j * tq, tq), tq), :], qt_ref[...])

    def softmax_pv(j, slot, mode):
        off = pl.multiple_of(j * tq, tq)
        ck = ck_ref[pl.ds(off, tq), :] * LOG2E
        x = (lg_ref[slot] + cq) - ck
        if mode == "pad":
            s = lax.broadcasted_iota(jnp.int32, (tq, 1), 0)
            x = jnp.where(s >= PAD, x, NEG)
        elif mode == "diag":
            s = lax.broadcasted_iota(jnp.int32, (tq, tq), 0)
            t = lax.broadcasted_iota(jnp.int32, (tq, tq), 1)
            x = jnp.where((s <= t) & ((s + j * tq >= PAD) | (s == t)), x, NEG)
        m_prev = m_ref[...]
        m_new = jnp.maximum(m_prev, jnp.max(x, axis=0, keepdims=True))
        alpha = jnp.exp2(m_prev - m_new)
        p = jnp.exp2(x - m_new)
        l_ref[...] = alpha * l_ref[...] + jnp.sum(p, axis=0, keepdims=True)
        acc_ref[...] = acc_ref[...] * alpha + _dot(vt_ref[:, pl.ds(off, tq)], p.astype(BF16))
        m_ref[...] = m_new

    qk(i, 2)

    @pl.when(i > 0)
    def _():
        qk(0, 0)
        qk(jnp.minimum(1, i), 1)
        softmax_pv(0, 0, "pad")
        n_mid = i - 1

        def pair(g, _):
            a = 2 * g + 1
            qk(jnp.minimum(a + 1, i), 0)
            softmax_pv(a, 1, "none")
            qk(jnp.minimum(a + 2, i), 1)
            softmax_pv(a + 1, 0, "none")
            return 0

        lax.fori_loop(0, n_mid // 2, pair, 0)

        @pl.when(n_mid % 2 == 1)
        def _():
            softmax_pv(i - 1, 1, "none")

    softmax_pv(i, 2, "diag")

    o_ref[...] = jnp.transpose(acc_ref[...] * (1.0 / l_ref[...])).astype(o_ref.dtype)


def _fox(qt, pbf3, vt, cq, ck):
    b, t, _ = pbf3.shape
    nblk = t // BLOCK
    tq = BLOCK * max(k for k in (1, 5) if nblk % k == 0)
    d = FOX_HEAD_DIM
    kb = 3 * 1024 // d
    return pl.pallas_call(
        functools.partial(_fox_kernel, tq=tq),
        grid=(b, FOX_HEADS, t // tq),
        in_specs=[
            pl.BlockSpec((None, None, d, tq), lambda bi, h, i: (bi, h, 0, i)),
            pl.BlockSpec((None, t, d), lambda bi, h, i: (bi, 0, kb + h)),
            pl.BlockSpec((None, None, d, t), lambda bi, h, i: (bi, h, 0, 0)),
            pl.BlockSpec((None, None, 1, tq), lambda bi, h, i: (bi, h, 0, i)),
            pl.BlockSpec((None, None, t, 1), lambda bi, h, i: (bi, h, 0, 0)),
        ],
        out_specs=pl.BlockSpec((None, tq, d), lambda bi, h, i: (bi, i, h)),
        out_shape=jax.ShapeDtypeStruct((b, t, FOX_HEADS * d), BF16),
        scratch_shapes=[pltpu.VMEM((1, tq), F32), pltpu.VMEM((1, tq), F32), pltpu.VMEM((d, tq), F32),
                        pltpu.VMEM((3, tq, tq), F32)],
        compiler_params=_params(("parallel", "parallel", "arbitrary")),
    )(qt, pbf3, vt, cq, ck)


def _fox_inputs(pbf3):
    b, t, _ = pbf3.shape
    heads = lambda c0: jnp.transpose(pbf3[:, :, c0:c0 + 1024].reshape(b, t, FOX_HEADS, FOX_HEAD_DIM), (0, 2, 3, 1))
    return heads(2 * 1024), heads(4 * 1024)


def _ret_kernel(rq_ref, rk_ref, rv_ref, rg_ref, cos_ref, sin_ref, dq_ref, dk_ref, decay_ref, cd_ref, gn_ref,
                o_ref, state_ref):
    i = pl.program_id(1)

    @pl.when(i == 0)
    def _():
        state_ref[...] = jnp.zeros_like(state_ref)

    half = RET_HEADS * RET_QK_DIM // 2
    cos = cos_ref[...]
    sin = sin_ref[...]

    def rope(x):
        x1, x2 = x[:, :half], x[:, half:]
        return jnp.concatenate([x1 * cos - x2 * sin, x1 * sin + x2 * cos], axis=1)

    q = rope(rq_ref[...])
    k = rope(rk_ref[...]) * RET_QK_DIM ** -0.5
    pos = i * BLOCK + lax.broadcasted_iota(jnp.int32, (BLOCK, 1), 0)
    k = jnp.where(pos >= PAD, k, 0.0)
    qb = q.astype(BF16)
    kb = k.astype(BF16)
    q_dec = (q * dq_ref[...]).astype(BF16)
    k_dec = (k * dk_ref[...]).astype(BF16)
    head_of_col = (lax.broadcasted_iota(jnp.int32, (1, 2 * half), 1) % half) // (RET_QK_DIM // 2)
    zero = jnp.zeros((), BF16)
    for h in range(RET_HEADS):
        mine = head_of_col == h
        vh = rv_ref[:, h * RET_V_DIM:(h + 1) * RET_V_DIM]
        s = (_dot_nt(jnp.where(mine, qb, zero), kb) * decay_ref[h]).astype(BF16)
        st = state_ref[h]
        o = _dot(s, vh) + _dot(jnp.where(mine, q_dec, zero), st.astype(BF16))
        state_ref[h] = cd_ref[h] * st + _dot_tn(k_dec, vh)
        mu = jnp.mean(o, axis=1, keepdims=True)
        var = jnp.mean(jnp.square(o - mu), axis=1, keepdims=True)
        sl = slice(h * RET_V_DIM, (h + 1) * RET_V_DIM)
        o = (o - mu) * lax.rsqrt(var + GN_EPS) * gn_ref[:, sl]
        g = rg_ref[:, sl]
        o_ref[:, sl] = (g * jax.nn.sigmoid(g) * o).astype(o_ref.dtype)


def _retention(misc3, pbf3, cos, sin, dq_tab, dk_tab, decay, cd, gn_w):
    b, t, _ = misc3.shape
    qk = RET_HEADS * RET_QK_DIM
    vw = RET_HEADS * RET_V_DIM
    const2 = lambda bi, i: (0, 0)
    const3 = lambda bi, i: (0, 0, 0)
    return pl.pallas_call(
        _ret_kernel,
        grid=(b, t // BLOCK),
        in_specs=[
            pl.BlockSpec((None, BLOCK, qk), lambda bi, i: (bi, i, 0)),
            pl.BlockSpec((None, BLOCK, qk), lambda bi, i: (bi, i, 1)),
            pl.BlockSpec((None, BLOCK, vw), lambda bi, i: (bi, i, 5)),
            pl.BlockSpec((None, BLOCK, vw), lambda bi, i: (bi, i, 1)),
            pl.BlockSpec((BLOCK, qk // 2), lambda bi, i: (i, 0)),
            pl.BlockSpec((BLOCK, qk // 2), lambda bi, i: (i, 0)),
            pl.BlockSpec((BLOCK, qk), const2),
            pl.BlockSpec((BLOCK, qk), const2),
            pl.BlockSpec((RET_HEADS, BLOCK, BLOCK), const3),
            pl.BlockSpec((RET_HEADS, 1, RET_V_DIM), const3),
            pl.BlockSpec((1, vw), const2),
        ],
        out_specs=pl.BlockSpec((None, BLOCK, vw), lambda bi, i: (bi, i, 0)),
        out_shape=jax.ShapeDtypeStruct((b, t, vw), BF16),
        scratch_shapes=[pltpu.VMEM((RET_HEADS, qk, RET_V_DIM), F32)],
        compiler_params=_params(("parallel", "arbitrary")),
    )(misc3, misc3, pbf3, misc3, cos, sin, dq_tab, dk_tab, decay, cd, gn_w.reshape(1, vw))


def _merge_kernel(b0_ref, b1_ref, b2_ref, g0_ref, g1_ref, g2_ref, w_ref, o_ref):
    acc = jax.nn.sigmoid(g0_ref[...]) * _dot(b0_ref[...], w_ref[0])
    acc = acc + jax.nn.sigmoid(g1_ref[...]) * _dot(b1_ref[...], w_ref[1])
    acc = acc + jax.nn.sigmoid(g2_ref[...]) * _dot(b2_ref[...], w_ref[2])
    o_ref[...] = acc.astype(o_ref.dtype)


def _merge(b0, b1, b2, gates, w_branch):
    n, bw = b0.shape
    d = w_branch.shape[2]
    tm = _pick_tile(n, 640)
    tn = _pick_tile(d, 512, LANE)
    nj = d // tn
    bspec = pl.BlockSpec((tm, bw), lambda i, j: (i, 0))
    gspec = lambda r: pl.BlockSpec((tm, tn), lambda i, j: (i, r * nj + j))
    return pl.pallas_call(
        _merge_kernel,
        grid=(n // tm, nj),
        in_specs=[bspec, bspec, bspec, gspec(0), gspec(1), gspec(2),
                  pl.BlockSpec((N_BRANCH, bw, tn), lambda i, j: (0, 0, j))],
        out_specs=pl.BlockSpec((tm, tn), lambda i, j: (i, j)),
        out_shape=jax.ShapeDtypeStruct((n, d), BF16),
        compiler_params=_params(("parallel", "arbitrary")),
    )(b0, b1, b2, gates, gates, gates, w_branch)


def _resproj_kernel(a_ref, w_ref, r_ref, o_ref):
    o_ref[...] = r_ref[...] + _dot(a_ref[...], w_ref[...])


def _resproj(a, w, res):
    n, k = a.shape
    d = w.shape[1]
    tm = _pick_tile(n, 640)
    tn = _pick_tile(d, 512, LANE)
    return pl.pallas_call(
        _resproj_kernel,
        grid=(n // tm, d // tn),
        in_specs=[
            pl.BlockSpec((tm, k), lambda i, j: (i, 0)),
            pl.BlockSpec((k, tn), lambda i, j: (0, j)),
            pl.BlockSpec((tm, tn), lambda i, j: (i, j)),
        ],
        out_specs=pl.BlockSpec((tm, tn), lambda i, j: (i, j)),
        out_shape=jax.ShapeDtypeStruct((n, d), F32),
        compiler_params=_params(("parallel", "arbitrary")),
    )(a, w, res)


def _t5_bucket(dist):
    d = jnp.maximum(dist, 1).astype(F32)
    large = T5_MAX_EXACT + (jnp.log(d / T5_MAX_EXACT) / math.log(T5_MAX_DIST / T5_MAX_EXACT)
                            * (T5_BUCKETS - T5_MAX_EXACT)).astype(jnp.int32)
    large = jnp.minimum(large, T5_BUCKETS - 1)
    return jnp.where(dist < T5_MAX_EXACT, dist, large)


def _t5_bias_tiles(t5_table):
    k = jnp.arange(BLOCK, dtype=jnp.int32)[:, None]
    q = jnp.arange(BLOCK, dtype=jnp.int32)[None, :]
    table = t5_table.astype(F32)
    far = table[T5_BUCKETS - 1]
    bdiag = jnp.moveaxis(table[_t5_bucket(jnp.maximum(q - k, 0))] - far, -1, 0)
    bprev = jnp.moveaxis(table[_t5_bucket(BLOCK + q - k)] - far, -1, 0)
    return jnp.concatenate([bprev, bdiag, jnp.zeros_like(bdiag)], axis=1) * LOG2E


def _rope_tables(t):
    half = RET_QK_DIM // 2
    inv = ROPE_BASE ** (-jnp.arange(half, dtype=F32) / half)
    ang = jnp.arange(t, dtype=jnp.int32).astype(F32)[:, None] * inv[None, :]
    return jnp.tile(jnp.cos(ang), (1, RET_HEADS)), jnp.tile(jnp.sin(ang), (1, RET_HEADS))


def _decay_tables():
    log_gamma = jnp.log1p(-(2.0 ** (-5.0 - jnp.arange(RET_HEADS, dtype=F32))))
    i = jnp.arange(BLOCK, dtype=F32)
    gap = i[:, None] - i[None, :]
    decay = jnp.where(gap[None] >= 0, jnp.exp(jnp.maximum(gap, 0.0)[None] * log_gamma[:, None, None]), 0.0)
    head_of_col = (np.arange(RET_HEADS * RET_QK_DIM) % (RET_HEADS * RET_QK_DIM // 2)) // (RET_QK_DIM // 2)
    lg_col = log_gamma[head_of_col]
    dk_tab = jnp.exp((BLOCK - 1 - i)[:, None] * lg_col[None, :])
    dq_tab = jnp.exp((i + 1.0)[:, None] * lg_col[None, :])
    cd = jnp.broadcast_to(jnp.exp(BLOCK * log_gamma)[:, None, None], (RET_HEADS, 1, RET_V_DIM))
    return decay.astype(F32), dq_tab, dk_tab, cd


def _rope_perm():
    half = RET_QK_DIM // 2
    h = np.arange(RET_HEADS)[:, None]
    f = np.arange(half)[None, :]
    first = (h * RET_QK_DIM + f).reshape(-1)
    return np.concatenate([first, first + half])


def _split_w_in(w_in):
    cols = lambda name: w_in[:, IN_OFF[name]:IN_OFF[name] + IN_W[name]]
    zeros = lambda w: jnp.zeros((w_in.shape[0], w), w_in.dtype)
    perm = _rope_perm()
    fq = cols("fq") * (FOX_HEAD_DIM ** -0.5 * LOG2E)
    w_bf = jnp.concatenate([cols("dq"), cols("iq"), fq, cols("fk"), cols("fv"), cols("rv"),
                            cols("ik"), zeros(LANE - IDX_DIM)], axis=1).astype(BF16)
    w_misc = jnp.concatenate([cols("rq")[:, perm], cols("rk")[:, perm], cols("rg"), cols("dc"),
                              cols("iw"), cols("ff"), zeros(LANE - IDX_HEADS - FOX_HEADS),
                              zeros(LANE)], axis=1).astype(BF16)
    assert w_bf.shape[1] == BF_COLS and w_misc.shape[1] == MISC_COLS
    return w_bf, w_misc, cols("gates").astype(BF16)


def _mixer(x, b, t, mix_norm, w_in, kv_norm, w_uk, w_uv, f_bias, gn_w, w_branch, w_out, tables, topk):
    n = b * t
    nblk = t // BLOCK
    bias_tiles, rope_tabs, decay_tabs = tables
    w_bf, w_misc, w_gates = _split_w_in(w_in)
    pbf = _proj(x, mix_norm, w_bf, BF16)
    misc = _proj(x, mix_norm, w_misc, F32)
    gates = _proj(x, mix_norm, w_gates, F32)
    pbf3 = pbf.reshape(b, t, BF_COLS)
    misc3 = misc.reshape(b, t, MISC_COLS)

    c3 = _latent(misc, kv_norm).reshape(b, t, DSA_LATENT)
    iqt, wt, wuk, wuv = _dsa_inputs(pbf3, misc3, w_uk, w_uv)
    dsa_o = _dsa(pbf3, iqt, wt, c3, jnp.swapaxes(c3, 1, 2), wuk, wuv, bias_tiles, topk)

    cum = _cum_logf(misc3, f_bias)[:, :, SMALL_FF:SMALL_FF + FOX_HEADS]
    cum_h = jnp.swapaxes(cum, 1, 2)
    qt, vt = _fox_inputs(pbf3)
    fox_o = _fox(qt, pbf3, vt, cum_h[:, :, None, :], cum_h[..., None])

    ret_o = _retention(misc3, pbf3, *rope_tabs, *decay_tabs, gn_w)

    merged = _merge(dsa_o.reshape(n, -1), fox_o.reshape(n, -1), ret_o.reshape(n, -1), gates,
                    w_branch.astype(BF16))
    return _resproj(merged, w_out.astype(BF16), x)


def kernel(x, meta_tokens, t5_table, ffn1_norm, ffn1_w1, ffn1_w3, ffn1_w2, mix_norm, w_in, dsa_kv_norm,
           dsa_w_uk, dsa_w_uv, fox_f_bias, ret_gn_w, w_branch, w_out, ffn2_norm, ffn2_w1, ffn2_w3, ffn2_w2,
           final_norm):
    b, seq, d = x.shape
    assert d == D_MODEL and seq % BLOCK == 0
    t = seq + BLOCK
    depth = ffn1_norm.shape[0]
    topk = min(INDEX_TOPK, seq // 4)
    h = jnp.concatenate([
        jnp.zeros((b, PAD, d), x.dtype),
        jnp.broadcast_to(meta_tokens.astype(x.dtype)[None], (b, N_META, d)),
        x,
    ], axis=1).reshape(b * t, d)
    cos, sin = _rope_tables(t)
    decay, dq_tab, dk_tab, cd = _decay_tables()
    tables = (_t5_bias_tiles(t5_table), (cos, sin), (dq_tab, dk_tab, decay, cd))
    for l in range(depth):
        h = _ffn(h, ffn1_norm[l], ffn1_w1[l].astype(BF16), ffn1_w3[l].astype(BF16), ffn1_w2[l].astype(BF16))
        h = _mixer(h, b, t, mix_norm[l], w_in[l], dsa_kv_norm[l], dsa_w_uk[l], dsa_w_uv[l], fox_f_bias[l],
                   ret_gn_w[l], w_branch[l], w_out[l], tables, topk)
        h = _ffn(h, ffn2_norm[l], ffn2_w1[l].astype(BF16), ffn2_w3[l].astype(BF16), ffn2_w2[l].astype(BF16),
                 final_g=final_norm if l == depth - 1 else None)
    return h.reshape(b, t, d)[:, BLOCK:]
```

```python
import functools
import math

import numpy as np
import jax
import jax.numpy as jnp
from jax import lax
from jax.experimental import pallas as pl
from jax.experimental.pallas import tpu as pltpu

F32 = jnp.float32
BF16 = jnp.bfloat16

LANE = 128
VMEM_LIMIT = 56 * 1024 * 1024

D_MODEL = 2048
N_META = 16
BLOCK = 128
PAD = BLOCK - N_META
RMS_EPS = 1e-6
GN_EPS = 1e-5
NEG = -1e30

DSA_HEADS = 8
DSA_HEAD_DIM = 128
DSA_LATENT = 256
IDX_HEADS = 16
IDX_DIM = 64
INDEX_TOPK = 256
FOX_HEADS = 8
FOX_HEAD_DIM = 128
RET_HEADS = 8
RET_QK_DIM = 64
RET_V_DIM = 128
ROPE_BASE = 10000.0
BRANCH_WIDTH = 1024
N_BRANCH = 3
T5_BUCKETS = 32
T5_MAX_EXACT = 16
T5_MAX_DIST = 128

IN_WIDTHS = (1024, 256, 1024, 64, 16, 1024, 1024, 1024, 8, 512, 512, 1024, 1024, 6144)
IN_NAMES = ("dq", "dc", "iq", "ik", "iw", "fq", "fk", "fv", "ff", "rq", "rk", "rv", "rg", "gates")
IN_OFF = dict(zip(IN_NAMES, np.cumsum((0,) + IN_WIDTHS[:-1]).tolist()))
IN_W = dict(zip(IN_NAMES, IN_WIDTHS))

INT_MIN = -2 ** 31
LOG2E = math.log2(math.e)
SCORE_GROUP = 4

BF_COLS = 6 * 1024 + LANE
MISC_COLS = 2560
SMALL_IW = 0
SMALL_FF = IDX_HEADS


def _dot(a, b):
    return jnp.dot(a, b, preferred_element_type=F32)


def _dot_nt(a, b):
    return lax.dot_general(a, b, (((1,), (1,)), ((), ())), preferred_element_type=F32)


def _dot_tn(a, b):
    return lax.dot_general(a, b, (((0,), (0,)), ((), ())), preferred_element_type=F32)


def _pick_tile(n, target, mult=8):
    best = None
    for t in range(mult, min(n, target) + 1, mult):
        if n % t == 0:
            best = t
    assert best is not None, (n, target, mult)
    return best


def _params(sem):
    return pltpu.CompilerParams(dimension_semantics=sem, vmem_limit_bytes=VMEM_LIMIT)


def _rms(x, g):
    return x * lax.rsqrt(jnp.mean(x * x, axis=-1, keepdims=True) + RMS_EPS) * g


def _ffn_kernel(*refs, nj, final):
    if final:
        x_ref, g_ref, w1_ref, w3_ref, w2_ref, fg_ref, o_ref, xn_ref, acc_ref = refs
    else:
        x_ref, g_ref, w1_ref, w3_ref, w2_ref, o_ref, xn_ref, acc_ref = refs
    j = pl.program_id(1)

    @pl.when(j == 0)
    def _():
        xn_ref[...] = _rms(x_ref[...], g_ref[...]).astype(BF16)
        acc_ref[...] = jnp.zeros_like(acc_ref)

    xn = xn_ref[...]
    h1 = _dot(xn, w1_ref[...])
    h3 = _dot(xn, w3_ref[...])
    g = (h1 * jax.nn.sigmoid(h1) * h3).astype(BF16)
    acc_ref[...] += _dot(g, w2_ref[...])

    @pl.when(j == nj - 1)
    def _():
        y = x_ref[...] + 0.5 * acc_ref[...]
        if final:
            y = _rms(y, fg_ref[...])
        o_ref[...] = y


def _ffn(x, g, w1, w3, w2, final_g=None):
    n, d = x.shape
    f = w1.shape[1]
    tm = _pick_tile(n, 640)
    tf = _pick_tile(f, 512, LANE)
    nj = f // tf
    final = final_g is not None
    in_specs = [
        pl.BlockSpec((tm, d), lambda i, j: (i, 0)),
        pl.BlockSpec((1, d), lambda i, j: (0, 0)),
        pl.BlockSpec((d, tf), lambda i, j: (0, j)),
        pl.BlockSpec((d, tf), lambda i, j: (0, j)),
        pl.BlockSpec((tf, d), lambda i, j: (j, 0)),
    ]
    args = [x, g.reshape(1, d), w1, w3, w2]
    if final:
        in_specs.append(pl.BlockSpec((1, d), lambda i, j: (0, 0)))
        args.append(final_g.reshape(1, d))
    return pl.pallas_call(
        functools.partial(_ffn_kernel, nj=nj, final=final),
        grid=(n // tm, nj),
        in_specs=in_specs,
        out_specs=pl.BlockSpec((tm, d), lambda i, j: (i, 0)),
        out_shape=jax.ShapeDtypeStruct((n, d), F32),
        scratch_shapes=[pltpu.VMEM((tm, d), BF16), pltpu.VMEM((tm, d), F32)],
        compiler_params=_params(("parallel", "arbitrary")),
    )(*args)


def _proj_kernel(x_ref, g_ref, w_ref, o_ref, xn_ref):
    @pl.when(pl.program_id(1) == 0)
    def _():
        xn_ref[...] = _rms(x_ref[...], g_ref[...]).astype(BF16)

    o_ref[...] = _dot(xn_ref[...], w_ref[...]).astype(o_ref.dtype)


def _proj(x, g, w, out_dtype):
    n, d = x.shape
    c = w.shape[1]
    tm = _pick_tile(n, 640)
    tn = _pick_tile(c, 1024, LANE)
    return pl.pallas_call(
        _proj_kernel,
        grid=(n // tm, c // tn),
        in_specs=[
            pl.BlockSpec((tm, d), lambda i, j: (i, 0)),
            pl.BlockSpec((1, d), lambda i, j: (0, 0)),
            pl.BlockSpec((d, tn), lambda i, j: (0, j)),
        ],
        out_specs=pl.BlockSpec((tm, tn), lambda i, j: (i, j)),
        out_shape=jax.ShapeDtypeStruct((n, c), out_dtype),
        scratch_shapes=[pltpu.VMEM((tm, d), BF16)],
        compiler_params=_params(("parallel", "arbitrary")),
    )(x, g.reshape(1, d), w)


def _latent_kernel(dc_ref, g_ref, c_ref):
    c_ref[...] = _rms(dc_ref[...], g_ref[...]).astype(BF16)


def _latent(misc, kv_norm):
    n = misc.shape[0]
    tm = _pick_tile(n, 1280)
    blk = (IN_W["rq"] + IN_W["rk"] + IN_W["rg"]) // DSA_LATENT
    return pl.pallas_call(
        _latent_kernel,
        grid=(n // tm,),
        in_specs=[
            pl.BlockSpec((tm, DSA_LATENT), lambda i: (i, blk)),
            pl.BlockSpec((1, DSA_LATENT), lambda i: (0, 0)),
        ],
        out_specs=pl.BlockSpec((tm, DSA_LATENT), lambda i: (i, 0)),
        out_shape=jax.ShapeDtypeStruct((n, DSA_LATENT), BF16),
        compiler_params=_params(("parallel",)),
    )(misc, kv_norm.reshape(1, DSA_LATENT))


def _split3(x):
    hi = x.astype(BF16)
    r = x - hi.astype(F32)
    mid = r.astype(BF16)
    lo = (r - mid.astype(F32)).astype(BF16)
    return hi, mid, lo


def _cum_kernel(s_ref, b_ref, o_ref, carry_ref):
    @pl.when(pl.program_id(1) == 0)
    def _():
        carry_ref[...] = jnp.zeros_like(carry_ref)

    z = s_ref[...] + b_ref[...]
    lf = jnp.minimum(z, 0.0) - jnp.log1p(jnp.exp(-jnp.abs(z)))
    r = lax.broadcasted_iota(jnp.int32, (BLOCK, BLOCK), 0)
    c = lax.broadcasted_iota(jnp.int32, (BLOCK, BLOCK), 1)
    tri = jnp.where(c <= r, 1.0, 0.0).astype(BF16)
    hi, mid, lo = _split3(lf)
    cum = (_dot(tri, lo) + _dot(tri, mid)) + _dot(tri, hi) + carry_ref[...]
    o_ref[...] = cum
    carry_ref[...] = cum[BLOCK - 1:BLOCK, :]


def _cum_logf(misc3, f_bias):
    b, t, _ = misc3.shape
    blk = (IN_W["rq"] + IN_W["rk"] + IN_W["rg"] + IN_W["dc"]) // LANE
    bias = jnp.zeros((1, LANE), F32).at[0, SMALL_FF:SMALL_FF + FOX_HEADS].set(f_bias)
    return pl.pallas_call(
        _cum_kernel,
        grid=(b, t // BLOCK),
        in_specs=[
            pl.BlockSpec((None, BLOCK, LANE), lambda bi, i: (bi, i, blk)),
            pl.BlockSpec((1, LANE), lambda bi, i: (0, 0)),
        ],
        out_specs=pl.BlockSpec((None, BLOCK, LANE), lambda bi, i: (bi, i, 0)),
        out_shape=jax.ShapeDtypeStruct((b, t, LANE), F32),
        scratch_shapes=[pltpu.VMEM((1, LANE), F32)],
        compiler_params=_params(("parallel", "arbitrary")),
    )(misc3, bias)


FAR_BLOCKS = 4


def _dsa_kernel(dq_ref, iq_ref, sm_ref, ik_ref, c_ref, ct_ref, wuk_ref, wuv_ref, bias_ref,
                o_ref, s_ref, qlat_ref, p_ref, m_ref, l_ref, acc_ref, lg0_ref, lg1_ref, lgn_ref, iqt_ref, wt_ref,
                *, topk, nblk):
    i = pl.program_id(1)
    key_i = lax.broadcasted_iota(jnp.int32, (BLOCK, BLOCK), 0)
    qry_i = lax.broadcasted_iota(jnp.int32, (BLOCK, BLOCK), 1)
    hs = lambda h: slice(h * BLOCK, (h + 1) * BLOCK)

    iq_t = jnp.transpose(iq_ref[...].astype(F32)).astype(BF16)
    for h in range(IDX_HEADS):
        iqt_ref[0:IDX_DIM, hs(h)] = iq_t[h * IDX_DIM:(h + 1) * IDX_DIM, :]
    iqt_ref[IDX_DIM:LANE, :] = jnp.zeros((LANE - IDX_DIM, IDX_HEADS * BLOCK), BF16)
    wt_ref[...] = jnp.transpose(sm_ref[...])[SMALL_IW:SMALL_IW + IDX_HEADS, :] * (IDX_HEADS ** -0.5 * IDX_DIM ** -0.5)

    for h in range(DSA_HEADS):
        ql = _dot_nt(wuk_ref[h], dq_ref[:, h * DSA_HEAD_DIM:(h + 1) * DSA_HEAD_DIM]) * (DSA_HEAD_DIM ** -0.5 * LOG2E)
        qlat_ref[:, hs(h)] = ql.astype(BF16)

    ngrp = (i + SCORE_GROUP) // SCORE_GROUP
    for u in range(1, SCORE_GROUP):
        s_ref[i + u] = jnp.full((BLOCK, BLOCK), INT_MIN, jnp.int32)

    def score_group(g, _):
        j0 = jnp.minimum(g * SCORE_GROUP, nblk - SCORE_GROUP)
        kt = ik_ref[pl.ds(pl.multiple_of(j0 * BLOCK, BLOCK), SCORE_GROUP * BLOCK), :]
        r = _dot(kt, iqt_ref[...])
        t = i * BLOCK + qry_i
        for u in range(SCORE_GROUP):
            acc = jnp.zeros((BLOCK, BLOCK), F32)
            for h in range(IDX_HEADS):
                acc = acc + jnp.maximum(r[u * BLOCK:(u + 1) * BLOCK, hs(h)], 0.0) * wt_ref[h:h + 1, :]
            s = (j0 + u) * BLOCK + key_i
            ok = (s <= t) & ((s >= PAD) | (s == t))
            bits = lax.bitcast_convert_type(acc, jnp.int32)
            bits = jnp.where(bits == INT_MIN, 0, bits)
            key = bits ^ ((bits >> 31) & 0x7FFFFFFF)
            s_ref[j0 + u] = jnp.where(ok, key, INT_MIN)
        return 0

    lax.fori_loop(0, ngrp, score_group, 0)

    n_acc = 8

    def count_ge(cand):
        def body(g, acc):
            tiles = s_ref[pl.ds(g * SCORE_GROUP, SCORE_GROUP)]
            hit = jnp.where(tiles >= cand[None], 1.0, 0.0)
            return acc + jnp.sum(hit.reshape(-1, n_acc, 8, BLOCK), axis=0)

        acc = lax.fori_loop(0, ngrp, body, jnp.zeros((n_acc, 8, BLOCK), F32))
        return jnp.sum(acc, axis=(0, 1)).reshape(1, BLOCK)

    zero = jnp.zeros((1, BLOCK), jnp.int32)
    thr0 = jnp.where(count_ge(zero) >= topk, zero, INT_MIN)

    def bit_step(it, thr):
        cand = thr + (jnp.int32(1) << (30 - it))
        return jnp.where(count_ge(cand) >= topk, cand, thr)

    thr = lax.fori_loop(0, 31, bit_step, thr0)
    thr = jnp.maximum(thr, INT_MIN + 1)

    m_ref[...] = jnp.full(m_ref.shape, NEG, F32)
    l_ref[...] = jnp.zeros_like(l_ref)
    acc_ref[...] = jnp.zeros_like(acc_ref)

    def logits(j0, nb, dst_ref):
        w = nb * BLOCK
        dst_ref[0:w, :] = _dot(c_ref[pl.ds(pl.multiple_of(j0 * BLOCK, BLOCK), w), :], qlat_ref[...])

    def softmax_pv(j0, nb, lg_ref, bias_off=None):
        w = nb * BLOCK
        ckt = ct_ref[:, pl.ds(pl.multiple_of(j0 * BLOCK, BLOCK), w)]
        masked = jnp.where(s_ref[pl.ds(j0, nb)].reshape(w, BLOCK) >= thr, 0.0, NEG)
        for pair in range(DSA_HEADS // 2):
            cols = slice(2 * pair * BLOCK, 2 * (pair + 1) * BLOCK)
            alphas = []
            for h in (2 * pair, 2 * pair + 1):
                x = lg_ref[0:w, hs(h)] + masked
                if bias_off is not None:
                    x = x + bias_ref[h, pl.ds(bias_off, w), :]
                m_prev = m_ref[h:h + 1, :]
                m_new = jnp.maximum(m_prev, jnp.max(x, axis=0, keepdims=True))
                alpha = jnp.exp2(m_prev - m_new)
                p = jnp.exp2(x - m_new)
                l_ref[h:h + 1, :] = alpha * l_ref[h:h + 1, :] + jnp.sum(p, axis=0, keepdims=True)
                m_ref[h:h + 1, :] = m_new
                p_ref[0:w, hs(h)] = p.astype(BF16)
                alphas.append(alpha)
            pv = _dot(ckt, p_ref[0:w, cols])
            for h, alpha in zip((2 * pair, 2 * pair + 1), alphas):
                acc_ref[:, hs(h)] = acc_ref[:, hs(h)] * alpha + pv[:, (h % 2) * BLOCK:(h % 2 + 1) * BLOCK]

    near0 = jnp.maximum(i - 1, 0)
    logits(near0, 2, lgn_ref)

    n_far = jnp.maximum(i - 1, 0)
    n_big = n_far // FAR_BLOCKS
    big = lambda g: g * FAR_BLOCKS

    @pl.when(n_big > 0)
    def _():
        logits(0, FAR_BLOCKS, lg0_ref)

        def pair_of_tiles(g, _):
            logits(big(2 * g + 1), FAR_BLOCKS, lg1_ref)
            softmax_pv(big(2 * g), FAR_BLOCKS, lg0_ref)
            logits(big(jnp.minimum(2 * g + 2, n_big - 1)), FAR_BLOCKS, lg0_ref)
            softmax_pv(big(2 * g + 1), FAR_BLOCKS, lg1_ref)
            return 0

        lax.fori_loop(0, n_big // 2, pair_of_tiles, 0)

        @pl.when(n_big % 2 == 1)
        def _():
            softmax_pv(big(n_big - 1), FAR_BLOCKS, lg0_ref)

    def far_one(j, _):
        logits(j, 1, lg0_ref)
        softmax_pv(j, 1, lg0_ref)
        return 0

    lax.fori_loop(n_big * FAR_BLOCKS, n_far, far_one, 0)
    softmax_pv(near0, 2, lgn_ref, bias_off=pl.multiple_of(jnp.where(i >= 1, 0, BLOCK), BLOCK))

    inv_l = 1.0 / l_ref[...]
    for h in range(DSA_HEADS):
        o_lat = (acc_ref[:, hs(h)] * inv_l[h:h + 1, :]).astype(BF16)
        o = _dot_tn(o_lat, wuv_ref[h])
        o_ref[:, h * DSA_HEAD_DIM:(h + 1) * DSA_HEAD_DIM] = o.astype(o_ref.dtype)


def _dsa(pbf3, misc3, c3, ct3, wuk, wuv, bias, topk):
    b, t, _ = pbf3.shape
    nblk = t // BLOCK
    hq = DSA_HEADS * DSA_HEAD_DIM
    ik_blk = 6 * 1024 // LANE
    small_blk = (IN_W["rq"] + IN_W["rk"] + IN_W["rg"] + IN_W["dc"]) // LANE
    n_tiles = (nblk + SCORE_GROUP - 1) // SCORE_GROUP * SCORE_GROUP + SCORE_GROUP
    const3 = lambda bi, i: (0, 0, 0)
    return pl.pallas_call(
        functools.partial(_dsa_kernel, topk=topk, nblk=nblk),
        grid=(b, nblk),
        in_specs=[
            pl.BlockSpec((None, BLOCK, hq), lambda bi, i: (bi, i, 0)),
            pl.BlockSpec((None, BLOCK, IDX_HEADS * IDX_DIM), lambda bi, i: (bi, i, 1)),
            pl.BlockSpec((None, BLOCK, LANE), lambda bi, i: (bi, i, small_blk)),
            pl.BlockSpec((None, t, LANE), lambda bi, i: (bi, 0, ik_blk)),
            pl.BlockSpec((None, t, DSA_LATENT), lambda bi, i: (bi, 0, 0)),
            pl.BlockSpec((None, DSA_LATENT, t), lambda bi, i: (bi, 0, 0)),
            pl.BlockSpec((DSA_HEADS, DSA_LATENT, DSA_HEAD_DIM), const3),
            pl.BlockSpec((DSA_HEADS, DSA_LATENT, DSA_HEAD_DIM), const3),
            pl.BlockSpec((DSA_HEADS, 3 * BLOCK, BLOCK), const3),
        ],
        out_specs=pl.BlockSpec((None, BLOCK, hq), lambda bi, i: (bi, i, 0)),
        out_shape=jax.ShapeDtypeStruct((b, t, hq), BF16),
        scratch_shapes=[
            pltpu.VMEM((n_tiles, BLOCK, BLOCK), jnp.int32),
            pltpu.VMEM((DSA_LATENT, DSA_HEADS * BLOCK), BF16),
            pltpu.VMEM((FAR_BLOCKS * BLOCK, DSA_HEADS * BLOCK), BF16),
            pltpu.VMEM((DSA_HEADS, BLOCK), F32),
            pltpu.VMEM((DSA_HEADS, BLOCK), F32),
            pltpu.VMEM((DSA_LATENT, DSA_HEADS * BLOCK), F32),
            pltpu.VMEM((FAR_BLOCKS * BLOCK, DSA_HEADS * BLOCK), F32),
            pltpu.VMEM((FAR_BLOCKS * BLOCK, DSA_HEADS * BLOCK), F32),
            pltpu.VMEM((2 * BLOCK, DSA_HEADS * BLOCK), F32),
            pltpu.VMEM((LANE, IDX_HEADS * BLOCK), BF16),
            pltpu.VMEM((IDX_HEADS, BLOCK), F32),
        ],
        compiler_params=_params(("parallel", "arbitrary")),
    )(pbf3, pbf3, misc3, pbf3, c3, ct3, wuk, wuv, bias)


def _dsa_weights(w_uk, w_uv):
    return jnp.transpose(w_uk, (1, 0, 2)).astype(BF16), jnp.transpose(w_uv, (1, 0, 2)).astype(BF16)


def _fox_kernel(q_ref, k_ref, v_ref, cq_ref, ck_ref, o_ref, m_ref, l_ref, acc_ref, lg_ref, *, tq):
    i = pl.program_id(2)
    cq = cq_ref[...] * LOG2E
    m_ref[...] = jnp.full(m_ref.shape, NEG, F32)
    l_ref[...] = jnp.zeros_like(l_ref)
    acc_ref[...] = jnp.zeros_like(acc_ref)

    def qk(j, slot):
        lg_ref[slot] = _dot_nt(k_ref[pl.ds(pl.multiple_of(j * tq, tq), tq), :], q_ref[...])

    def softmax_pv(j, slot, mode):
        off = pl.multiple_of(j * tq, tq)
        ck = ck_ref[pl.ds(off, tq), :] * LOG2E
        x = (lg_ref[slot] + cq) - ck
        if mode == "pad":
            s = lax.broadcasted_iota(jnp.int32, (tq, 1), 0)
            x = jnp.where(s >= PAD, x, NEG)
        elif mode == "diag":
            s = lax.broadcasted_iota(jnp.int32, (tq, tq), 0)
            t = lax.broadcasted_iota(jnp.int32, (tq, tq), 1)
            x = jnp.where((s <= t) & ((s + j * tq >= PAD) | (s == t)), x, NEG)
        m_prev = m_ref[...]
        m_new = jnp.maximum(m_prev, jnp.max(x, axis=0, keepdims=True))
        alpha = jnp.exp2(m_prev - m_new)
        p = jnp.exp2(x - m_new)
        l_ref[...] = alpha * l_ref[...] + jnp.sum(p, axis=0, keepdims=True)
        acc_ref[...] = acc_ref[...] * alpha + _dot_tn(v_ref[pl.ds(off, tq), :], p.astype(BF16))
        m_ref[...] = m_new

    qk(i, 2)

    @pl.when(i > 0)
    def _():
        qk(0, 0)
        qk(jnp.minimum(1, i), 1)
        softmax_pv(0, 0, "pad")
        n_mid = i - 1

        def pair(g, _):
            a = 2 * g + 1
            qk(jnp.minimum(a + 1, i), 0)
            softmax_pv(a, 1, "none")
            qk(jnp.minimum(a + 2, i), 1)
            softmax_pv(a + 1, 0, "none")
            return 0

        lax.fori_loop(0, n_mid // 2, pair, 0)

        @pl.when(n_mid % 2 == 1)
        def _():
            softmax_pv(i - 1, 1, "none")

    softmax_pv(i, 2, "diag")

    o_ref[...] = jnp.transpose(acc_ref[...] * (1.0 / l_ref[...])).astype(o_ref.dtype)


def _fox(pbf3, cq, ck):
    b, t, _ = pbf3.shape
    nblk = t // BLOCK
    tq = BLOCK * max(k for k in (1, 5) if nblk % k == 0)
    d = FOX_HEAD_DIM
    qb, kb, vb = 2 * 1024 // d, 3 * 1024 // d, 4 * 1024 // d
    return pl.pallas_call(
        functools.partial(_fox_kernel, tq=tq),
        grid=(b, FOX_HEADS, t // tq),
        in_specs=[
            pl.BlockSpec((None, tq, d), lambda bi, h, i: (bi, i, qb + h)),
            pl.BlockSpec((None, t, d), lambda bi, h, i: (bi, 0, kb + h)),
            pl.BlockSpec((None, t, d), lambda bi, h, i: (bi, 0, vb + h)),
            pl.BlockSpec((None, None, 1, tq), lambda bi, h, i: (bi, h, 0, i)),
            pl.BlockSpec((None, None, t, 1), lambda bi, h, i: (bi, h, 0, 0)),
        ],
        out_specs=pl.BlockSpec((None, tq, d), lambda bi, h, i: (bi, i, h)),
        out_shape=jax.ShapeDtypeStruct((b, t, FOX_HEADS * d), BF16),
        scratch_shapes=[pltpu.VMEM((1, tq), F32), pltpu.VMEM((1, tq), F32), pltpu.VMEM((d, tq), F32),
                        pltpu.VMEM((3, tq, tq), F32)],
        compiler_params=_params(("parallel", "parallel", "arbitrary")),
    )(pbf3, pbf3, pbf3, cq, ck)


def _ret_kernel(rq_ref, rk_ref, rv_ref, rg_ref, cos_ref, sin_ref, dq_ref, dk_ref, decay_ref, cd_ref, gn_ref,
                o_ref, state_ref):
    i = pl.program_id(1)

    @pl.when(i == 0)
    def _():
        state_ref[...] = jnp.zeros_like(state_ref)

    half = RET_HEADS * RET_QK_DIM // 2
    cos = cos_ref[...]
    sin = sin_ref[...]

    def rope(x):
        x1, x2 = x[:, :half], x[:, half:]
        return jnp.concatenate([x1 * cos - x2 * sin, x1 * sin + x2 * cos], axis=1)

    q = rope(rq_ref[...])
    k = rope(rk_ref[...]) * RET_QK_DIM ** -0.5
    pos = i * BLOCK + lax.broadcasted_iota(jnp.int32, (BLOCK, 1), 0)
    k = jnp.where(pos >= PAD, k, 0.0)
    qb = q.astype(BF16)
    kb = k.astype(BF16)
    q_dec = (q * dq_ref[...]).astype(BF16)
    k_dec = (k * dk_ref[...]).astype(BF16)
    head_of_col = (lax.broadcasted_iota(jnp.int32, (1, 2 * half), 1) % half) // (RET_QK_DIM // 2)
    zero = jnp.zeros((), BF16)
    for h in range(RET_HEADS):
        mine = head_of_col == h
        vh = rv_ref[:, h * RET_V_DIM:(h + 1) * RET_V_DIM]
        s = (_dot_nt(jnp.where(mine, qb, zero), kb) * decay_ref[h]).astype(BF16)
        st = state_ref[h]
        o = _dot(s, vh) + _dot(jnp.where(mine, q_dec, zero), st.astype(BF16))
        state_ref[h] = cd_ref[h] * st + _dot_tn(k_dec, vh)
        mu = jnp.mean(o, axis=1, keepdims=True)
        var = jnp.mean(jnp.square(o - mu), axis=1, keepdims=True)
        sl = slice(h * RET_V_DIM, (h + 1) * RET_V_DIM)
        o = (o - mu) * lax.rsqrt(var + GN_EPS) * gn_ref[:, sl]
        g = rg_ref[:, sl]
        o_ref[:, sl] = (g * jax.nn.sigmoid(g) * o).astype(o_ref.dtype)


def _retention(misc3, pbf3, cos, sin, dq_tab, dk_tab, decay, cd, gn_w):
    b, t, _ = misc3.shape
    qk = RET_HEADS * RET_QK_DIM
    vw = RET_HEADS * RET_V_DIM
    const2 = lambda bi, i: (0, 0)
    const3 = lambda bi, i: (0, 0, 0)
    return pl.pallas_call(
        _ret_kernel,
        grid=(b, t // BLOCK),
        in_specs=[
            pl.BlockSpec((None, BLOCK, qk), lambda bi, i: (bi, i, 0)),
            pl.BlockSpec((None, BLOCK, qk), lambda bi, i: (bi, i, 1)),
            pl.BlockSpec((None, BLOCK, vw), lambda bi, i: (bi, i, 5)),
            pl.BlockSpec((None, BLOCK, vw), lambda bi, i: (bi, i, 1)),
            pl.BlockSpec((BLOCK, qk // 2), lambda bi, i: (i, 0)),
            pl.BlockSpec((BLOCK, qk // 2), lambda bi, i: (i, 0)),
            pl.BlockSpec((BLOCK, qk), const2),
            pl.BlockSpec((BLOCK, qk), const2),
            pl.BlockSpec((RET_HEADS, BLOCK, BLOCK), const3),
            pl.BlockSpec((RET_HEADS, 1, RET_V_DIM), const3),
            pl.BlockSpec((1, vw), const2),
        ],
        out_specs=pl.BlockSpec((None, BLOCK, vw), lambda bi, i: (bi, i, 0)),
        out_shape=jax.ShapeDtypeStruct((b, t, vw), BF16),
        scratch_shapes=[pltpu.VMEM((RET_HEADS, qk, RET_V_DIM), F32)],
        compiler_params=_params(("parallel", "arbitrary")),
    )(misc3, misc3, pbf3, misc3, cos, sin, dq_tab, dk_tab, decay, cd, gn_w.reshape(1, vw))


def _merge_kernel(x_ref, g_ref, b0_ref, b1_ref, b2_ref, wg0_ref, wg1_ref, wg2_ref, wb_ref, o_ref, xn_ref):
    @pl.when(pl.program_id(1) == 0)
    def _():
        xn_ref[...] = _rms(x_ref[...], g_ref[...]).astype(BF16)

    xn = xn_ref[...]
    acc = jax.nn.sigmoid(_dot(xn, wg0_ref[...])) * _dot(b0_ref[...], wb_ref[0])
    acc = acc + jax.nn.sigmoid(_dot(xn, wg1_ref[...])) * _dot(b1_ref[...], wb_ref[1])
    acc = acc + jax.nn.sigmoid(_dot(xn, wg2_ref[...])) * _dot(b2_ref[...], wb_ref[2])
    o_ref[...] = acc.astype(o_ref.dtype)


def _merge(x, g, b0, b1, b2, w_gates, w_branch):
    n, bw = b0.shape
    dm = x.shape[1]
    d = w_branch.shape[2]
    tm = _pick_tile(n, 640)
    tn = _pick_tile(d, 512, LANE)
    nj = d // tn
    bspec = pl.BlockSpec((tm, bw), lambda i, j: (i, 0))
    gspec = lambda r: pl.BlockSpec((dm, tn), lambda i, j: (0, r * nj + j))
    return pl.pallas_call(
        _merge_kernel,
        grid=(n // tm, nj),
        in_specs=[pl.BlockSpec((tm, dm), lambda i, j: (i, 0)), pl.BlockSpec((1, dm), lambda i, j: (0, 0)),
                  bspec, bspec, bspec, gspec(0), gspec(1), gspec(2),
                  pl.BlockSpec((N_BRANCH, bw, tn), lambda i, j: (0, 0, j))],
        out_specs=pl.BlockSpec((tm, tn), lambda i, j: (i, j)),
        out_shape=jax.ShapeDtypeStruct((n, d), BF16),
        scratch_shapes=[pltpu.VMEM((tm, dm), BF16)],
        compiler_params=_params(("parallel", "arbitrary")),
    )(x, g.reshape(1, dm), b0, b1, b2, w_gates, w_gates, w_gates, w_branch)


def _resproj_kernel(a_ref, w_ref, r_ref, o_ref):
    o_ref[...] = r_ref[...] + _dot(a_ref[...], w_ref[...])


def _resproj(a, w, res):
    n, k = a.shape
    d = w.shape[1]
    tm = _pick_tile(n, 640)
    tn = _pick_tile(d, 512, LANE)
    return pl.pallas_call(
        _resproj_kernel,
        grid=(n // tm, d // tn),
        in_specs=[
            pl.BlockSpec((tm, k), lambda i, j: (i, 0)),
            pl.BlockSpec((k, tn), lambda i, j: (0, j)),
            pl.BlockSpec((tm, tn), lambda i, j: (i, j)),
        ],
        out_specs=pl.BlockSpec((tm, tn), lambda i, j: (i, j)),
        out_shape=jax.ShapeDtypeStruct((n, d), F32),
        compiler_params=_params(("parallel", "arbitrary")),
    )(a, w, res)


def _t5_bucket(dist):
    d = jnp.maximum(dist, 1).astype(F32)
    large = T5_MAX_EXACT + (jnp.log(d / T5_MAX_EXACT) / math.log(T5_MAX_DIST / T5_MAX_EXACT)
                            * (T5_BUCKETS - T5_MAX_EXACT)).astype(jnp.int32)
    large = jnp.minimum(large, T5_BUCKETS - 1)
    return jnp.where(dist < T5_MAX_EXACT, dist, large)


def _t5_bias_tiles(t5_table):
    k = jnp.arange(BLOCK, dtype=jnp.int32)[:, None]
    q = jnp.arange(BLOCK, dtype=jnp.int32)[None, :]
    table = t5_table.astype(F32)
    far = table[T5_BUCKETS - 1]
    bdiag = jnp.moveaxis(table[_t5_bucket(jnp.maximum(q - k, 0))] - far, -1, 0)
    bprev = jnp.moveaxis(table[_t5_bucket(BLOCK + q - k)] - far, -1, 0)
    return jnp.concatenate([bprev, bdiag, jnp.zeros_like(bdiag)], axis=1) * LOG2E


def _rope_tables(t):
    half = RET_QK_DIM // 2
    inv = ROPE_BASE ** (-jnp.arange(half, dtype=F32) / half)
    ang = jnp.arange(t, dtype=jnp.int32).astype(F32)[:, None] * inv[None, :]
    return jnp.tile(jnp.cos(ang), (1, RET_HEADS)), jnp.tile(jnp.sin(ang), (1, RET_HEADS))


def _decay_tables():
    log_gamma = jnp.log1p(-(2.0 ** (-5.0 - jnp.arange(RET_HEADS, dtype=F32))))
    i = jnp.arange(BLOCK, dtype=F32)
    gap = i[:, None] - i[None, :]
    decay = jnp.where(gap[None] >= 0, jnp.exp(jnp.maximum(gap, 0.0)[None] * log_gamma[:, None, None]), 0.0)
    head_of_col = (np.arange(RET_HEADS * RET_QK_DIM) % (RET_HEADS * RET_QK_DIM // 2)) // (RET_QK_DIM // 2)
    lg_col = log_gamma[head_of_col]
    dk_tab = jnp.exp((BLOCK - 1 - i)[:, None] * lg_col[None, :])
    dq_tab = jnp.exp((i + 1.0)[:, None] * lg_col[None, :])
    cd = jnp.broadcast_to(jnp.exp(BLOCK * log_gamma)[:, None, None], (RET_HEADS, 1, RET_V_DIM))
    return decay.astype(F32), dq_tab, dk_tab, cd


def _split_w_in(w_in):
    w16 = w_in.astype(BF16)
    cols = lambda name: w16[:, IN_OFF[name]:IN_OFF[name] + IN_W[name]]
    zeros = lambda w: jnp.zeros((w_in.shape[0], w), BF16)
    rope = lambda a: jnp.swapaxes(a.reshape(-1, RET_HEADS, 2, RET_QK_DIM // 2), 1, 2).reshape(a.shape)
    fq = (w_in[:, IN_OFF["fq"]:IN_OFF["fq"] + IN_W["fq"]] * (FOX_HEAD_DIM ** -0.5 * LOG2E)).astype(BF16)
    w_bf = jnp.concatenate([cols("dq"), cols("iq"), fq, cols("fk"), cols("fv"), cols("rv"),
                            cols("ik"), zeros(LANE - IDX_DIM)], axis=1)
    w_misc = jnp.concatenate([rope(cols("rq")), rope(cols("rk")), cols("rg"), cols("dc"),
                              cols("iw"), cols("ff"), zeros(LANE - IDX_HEADS - FOX_HEADS),
                              zeros(LANE)], axis=1)
    assert w_bf.shape[1] == BF_COLS and w_misc.shape[1] == MISC_COLS
    return w_bf, w_misc, cols("gates")


def _mixer(x, b, t, mix_norm, w_in, kv_norm, w_uk, w_uv, f_bias, gn_w, w_branch, w_out, tables, topk):
    n = b * t
    nblk = t // BLOCK
    bias_tiles, rope_tabs, decay_tabs = tables
    w_bf, w_misc, w_gates = _split_w_in(w_in)
    pbf = _proj(x, mix_norm, w_bf, BF16)
    misc = _proj(x, mix_norm, w_misc, F32)
    pbf3 = pbf.reshape(b, t, BF_COLS)
    misc3 = misc.reshape(b, t, MISC_COLS)

    c3 = _latent(misc, kv_norm).reshape(b, t, DSA_LATENT)
    wuk, wuv = _dsa_weights(w_uk, w_uv)
    dsa_o = _dsa(pbf3, misc3, c3, jnp.swapaxes(c3, 1, 2), wuk, wuv, bias_tiles, topk)

    cum = _cum_logf(misc3, f_bias)[:, :, SMALL_FF:SMALL_FF + FOX_HEADS]
    cum_h = jnp.swapaxes(cum, 1, 2)
    fox_o = _fox(pbf3, cum_h[:, :, None, :], cum_h[..., None])

    ret_o = _retention(misc3, pbf3, *rope_tabs, *decay_tabs, gn_w)

    merged = _merge(x, mix_norm, dsa_o.reshape(n, -1), fox_o.reshape(n, -1), ret_o.reshape(n, -1), w_gates,
                    w_branch.astype(BF16))
    return _resproj(merged, w_out.astype(BF16), x)


def kernel(x, meta_tokens, t5_table, ffn1_norm, ffn1_w1, ffn1_w3, ffn1_w2, mix_norm, w_in, dsa_kv_norm,
           dsa_w_uk, dsa_w_uv, fox_f_bias, ret_gn_w, w_branch, w_out, ffn2_norm, ffn2_w1, ffn2_w3, ffn2_w2,
           final_norm):
    b, seq, d = x.shape
    assert d == D_MODEL and seq % BLOCK == 0
    t = seq + BLOCK
    depth = ffn1_norm.shape[0]
    topk = min(INDEX_TOPK, seq // 4)
    h = jnp.concatenate([
        jnp.zeros((b, PAD, d), x.dtype),
        jnp.broadcast_to(meta_tokens.astype(x.dtype)[None], (b, N_META, d)),
        x,
    ], axis=1).reshape(b * t, d)
    cos, sin = _rope_tables(t)
    decay, dq_tab, dk_tab, cd = _decay_tables()
    tables = (_t5_bias_tiles(t5_table), (cos, sin), (dq_tab, dk_tab, decay, cd))
    for l in range(depth):
        h = _ffn(h, ffn1_norm[l], ffn1_w1[l].astype(BF16), ffn1_w3[l].astype(BF16), ffn1_w2[l].astype(BF16))
        h = _mixer(h, b, t, mix_norm[l], w_in[l], dsa_kv_norm[l], dsa_w_uk[l], dsa_w_uv[l], fox_f_bias[l],
                   ret_gn_w[l], w_branch[l], w_out[l], tables, topk)
        h = _ffn(h, ffn2_norm[l], ffn2_w1[l].astype(BF16), ffn2_w3[l].astype(BF16), ffn2_w2[l].astype(BF16),
                 final_g=final_norm if l == depth - 1 else None)
    return h.reshape(b, t, d)[:, BLOCK:]
```

```python
import functools
import math

import numpy as np
import jax
import jax.numpy as jnp
from jax import lax
from jax.experimental import pallas as pl
from jax.experimental.pallas import tpu as pltpu

F32 = jnp.float32
BF16 = jnp.bfloat16

LANE = 128
VMEM_LIMIT = 56 * 1024 * 1024

D_MODEL = 2048
N_META = 16
BLOCK = 128
PAD = BLOCK - N_META
RMS_EPS = 1e-6
GN_EPS = 1e-5
NEG = -1e30

DSA_HEADS = 8
DSA_HEAD_DIM = 128
DSA_LATENT = 256
IDX_HEADS = 16
IDX_DIM = 64
INDEX_TOPK = 256
FOX_HEADS = 8
FOX_HEAD_DIM = 128
RET_HEADS = 8
RET_QK_DIM = 64
RET_V_DIM = 128
ROPE_BASE = 10000.0
BRANCH_WIDTH = 1024
N_BRANCH = 3
T5_BUCKETS = 32
T5_MAX_EXACT = 16
T5_MAX_DIST = 128

IN_WIDTHS = (1024, 256, 1024, 64, 16, 1024, 1024, 1024, 8, 512, 512, 1024, 1024, 6144)
IN_NAMES = ("dq", "dc", "iq", "ik", "iw", "fq", "fk", "fv", "ff", "rq", "rk", "rv", "rg", "gates")
IN_OFF = dict(zip(IN_NAMES, np.cumsum((0,) + IN_WIDTHS[:-1]).tolist()))
IN_W = dict(zip(IN_NAMES, IN_WIDTHS))

INT_MIN = -2 ** 31
LOG2E = math.log2(math.e)
SCORE_GROUP = 4

BF_COLS = 6 * 1024 + LANE
MISC_COLS = 2560
SMALL_IW = 0
SMALL_FF = IDX_HEADS


def _dot(a, b):
    return jnp.dot(a, b, preferred_element_type=F32)


def _dot_nt(a, b):
    return lax.dot_general(a, b, (((1,), (1,)), ((), ())), preferred_element_type=F32)


def _dot_tn(a, b):
    return lax.dot_general(a, b, (((0,), (0,)), ((), ())), preferred_element_type=F32)


def _pick_tile(n, target, mult=8):
    best = None
    for t in range(mult, min(n, target) + 1, mult):
        if n % t == 0:
            best = t
    assert best is not None, (n, target, mult)
    return best


def _params(sem):
    return pltpu.CompilerParams(dimension_semantics=sem, vmem_limit_bytes=VMEM_LIMIT)


def _rms(x, g):
    return x * lax.rsqrt(jnp.mean(x * x, axis=-1, keepdims=True) + RMS_EPS) * g


def _ffn_kernel(*refs, nj, final):
    if final:
        x_ref, g_ref, w1_ref, w3_ref, w2_ref, fg_ref, o_ref, xn_ref, acc_ref = refs
    else:
        x_ref, g_ref, w1_ref, w3_ref, w2_ref, o_ref, xn_ref, acc_ref = refs
    j = pl.program_id(1)

    @pl.when(j == 0)
    def _():
        xn_ref[...] = _rms(x_ref[...], g_ref[...]).astype(BF16)
        acc_ref[...] = jnp.zeros_like(acc_ref)

    xn = xn_ref[...]
    h1 = _dot(xn, w1_ref[...])
    h3 = _dot(xn, w3_ref[...])
    g = (h1 * jax.nn.sigmoid(h1) * h3).astype(BF16)
    acc_ref[...] += _dot(g, w2_ref[...])

    @pl.when(j == nj - 1)
    def _():
        y = x_ref[...] + 0.5 * acc_ref[...]
        if final:
            y = _rms(y, fg_ref[...])
        o_ref[...] = y


def _ffn(x, g, w1, w3, w2, final_g=None):
    n, d = x.shape
    f = w1.shape[1]
    tm = _pick_tile(n, 640)
    tf = _pick_tile(f, 512, LANE)
    nj = f // tf
    final = final_g is not None
    in_specs = [
        pl.BlockSpec((tm, d), lambda i, j: (i, 0)),
        pl.BlockSpec((1, d), lambda i, j: (0, 0)),
        pl.BlockSpec((d, tf), lambda i, j: (0, j)),
        pl.BlockSpec((d, tf), lambda i, j: (0, j)),
        pl.BlockSpec((tf, d), lambda i, j: (j, 0)),
    ]
    args = [x, g.reshape(1, d), w1, w3, w2]
    if final:
        in_specs.append(pl.BlockSpec((1, d), lambda i, j: (0, 0)))
        args.append(final_g.reshape(1, d))
    return pl.pallas_call(
        functools.partial(_ffn_kernel, nj=nj, final=final),
        grid=(n // tm, nj),
        in_specs=in_specs,
        out_specs=pl.BlockSpec((tm, d), lambda i, j: (i, 0)),
        out_shape=jax.ShapeDtypeStruct((n, d), F32),
        scratch_shapes=[pltpu.VMEM((tm, d), BF16), pltpu.VMEM((tm, d), F32)],
        compiler_params=_params(("parallel", "arbitrary")),
    )(*args)


def _proj_kernel(x_ref, g_ref, w_ref, o_ref, xn_ref):
    @pl.when(pl.program_id(1) == 0)
    def _():
        xn_ref[...] = _rms(x_ref[...], g_ref[...]).astype(BF16)

    o_ref[...] = _dot(xn_ref[...], w_ref[...]).astype(o_ref.dtype)


def _proj(x, g, w, out_dtype):
    n, d = x.shape
    c = w.shape[1]
    tm = _pick_tile(n, 640)
    tn = _pick_tile(c, 1024, LANE)
    return pl.pallas_call(
        _proj_kernel,
        grid=(n // tm, c // tn),
        in_specs=[
            pl.BlockSpec((tm, d), lambda i, j: (i, 0)),
            pl.BlockSpec((1, d), lambda i, j: (0, 0)),
            pl.BlockSpec((d, tn), lambda i, j: (0, j)),
        ],
        out_specs=pl.BlockSpec((tm, tn), lambda i, j: (i, j)),
        out_shape=jax.ShapeDtypeStruct((n, c), out_dtype),
        scratch_shapes=[pltpu.VMEM((tm, d), BF16)],
        compiler_params=_params(("parallel", "arbitrary")),
    )(x, g.reshape(1, d), w)


def _latent_kernel(dc_ref, g_ref, c_ref):
    c_ref[...] = _rms(dc_ref[...], g_ref[...]).astype(BF16)


def _latent(misc, kv_norm):
    n = misc.shape[0]
    tm = _pick_tile(n, 1280)
    blk = (IN_W["rq"] + IN_W["rk"] + IN_W["rg"]) // DSA_LATENT
    return pl.pallas_call(
        _latent_kernel,
        grid=(n // tm,),
        in_specs=[
            pl.BlockSpec((tm, DSA_LATENT), lambda i: (i, blk)),
            pl.BlockSpec((1, DSA_LATENT), lambda i: (0, 0)),
        ],
        out_specs=pl.BlockSpec((tm, DSA_LATENT), lambda i: (i, 0)),
        out_shape=jax.ShapeDtypeStruct((n, DSA_LATENT), BF16),
        compiler_params=_params(("parallel",)),
    )(misc, kv_norm.reshape(1, DSA_LATENT))


def _split3(x):
    hi = x.astype(BF16)
    r = x - hi.astype(F32)
    mid = r.astype(BF16)
    lo = (r - mid.astype(F32)).astype(BF16)
    return hi, mid, lo


def _cum_kernel(s_ref, b_ref, o_ref, carry_ref):
    @pl.when(pl.program_id(1) == 0)
    def _():
        carry_ref[...] = jnp.zeros_like(carry_ref)

    z = s_ref[...] + b_ref[...]
    lf = jnp.minimum(z, 0.0) - jnp.log1p(jnp.exp(-jnp.abs(z)))
    r = lax.broadcasted_iota(jnp.int32, (BLOCK, BLOCK), 0)
    c = lax.broadcasted_iota(jnp.int32, (BLOCK, BLOCK), 1)
    tri = jnp.where(c <= r, 1.0, 0.0).astype(BF16)
    hi, mid, lo = _split3(lf)
    cum = (_dot(tri, lo) + _dot(tri, mid)) + _dot(tri, hi) + carry_ref[...]
    o_ref[...] = cum
    carry_ref[...] = cum[BLOCK - 1:BLOCK, :]


def _cum_logf(misc3, f_bias):
    b, t, _ = misc3.shape
    blk = (IN_W["rq"] + IN_W["rk"] + IN_W["rg"] + IN_W["dc"]) // LANE
    bias = jnp.zeros((1, LANE), F32).at[0, SMALL_FF:SMALL_FF + FOX_HEADS].set(f_bias)
    return pl.pallas_call(
        _cum_kernel,
        grid=(b, t // BLOCK),
        in_specs=[
            pl.BlockSpec((None, BLOCK, LANE), lambda bi, i: (bi, i, blk)),
            pl.BlockSpec((1, LANE), lambda bi, i: (0, 0)),
        ],
        out_specs=pl.BlockSpec((None, BLOCK, LANE), lambda bi, i: (bi, i, 0)),
        out_shape=jax.ShapeDtypeStruct((b, t, LANE), F32),
        scratch_shapes=[pltpu.VMEM((1, LANE), F32)],
        compiler_params=_params(("parallel", "arbitrary")),
    )(misc3, bias)


def _bit_transpose32(words):
    a = list(words)
    j, m = 16, 0x0000FFFF
    while j:
        mask = jnp.int32(m if m < 2 ** 31 else m - 2 ** 32)
        for k in range(32):
            if not k & j:
                t = (lax.shift_right_logical(a[k], jnp.int32(j)) ^ a[k + j]) & mask
                a[k + j] = a[k + j] ^ t
                a[k] = a[k] ^ (t << j)
        j >>= 1
        m = (m ^ (m << j)) & 0xFFFFFFFF
    return a


FAR_BLOCKS = 4


def _dsa_kernel(dq_ref, iq_ref, sm_ref, ik_ref, c_ref, ct_ref, wuk_ref, wuv_ref, bias_ref,
                o_ref, s_ref, qlat_ref, p_ref, m_ref, l_ref, acc_ref, lg0_ref, lg1_ref, lgn_ref, iqt_ref, wt_ref,
                planes_ref, alive_ref, *, topk, nblk):
    i = pl.program_id(1)
    key_i = lax.broadcasted_iota(jnp.int32, (BLOCK, BLOCK), 0)
    qry_i = lax.broadcasted_iota(jnp.int32, (BLOCK, BLOCK), 1)
    hs = lambda h: slice(h * BLOCK, (h + 1) * BLOCK)

    iq_t = jnp.transpose(iq_ref[...].astype(F32)).astype(BF16)
    for h in range(IDX_HEADS):
        iqt_ref[0:IDX_DIM, hs(h)] = iq_t[h * IDX_DIM:(h + 1) * IDX_DIM, :]
    iqt_ref[IDX_DIM:LANE, :] = jnp.zeros((LANE - IDX_DIM, IDX_HEADS * BLOCK), BF16)
    wt_ref[...] = jnp.transpose(sm_ref[...])[SMALL_IW:SMALL_IW + IDX_HEADS, :] * (IDX_HEADS ** -0.5 * IDX_DIM ** -0.5)

    for h in range(DSA_HEADS):
        ql = _dot_nt(wuk_ref[h], dq_ref[:, h * DSA_HEAD_DIM:(h + 1) * DSA_HEAD_DIM]) * (DSA_HEAD_DIM ** -0.5 * LOG2E)
        qlat_ref[:, hs(h)] = ql.astype(BF16)

    ngrp = (i + SCORE_GROUP) // SCORE_GROUP
    npair = (ngrp + 1) // 2
    for u in range(1, 2 * SCORE_GROUP):
        s_ref[i + u] = jnp.full((BLOCK, BLOCK), INT_MIN, jnp.int32)

    def score_group(g, _):
        j0 = jnp.minimum(g * SCORE_GROUP, nblk - SCORE_GROUP)
        kt = ik_ref[pl.ds(pl.multiple_of(j0 * BLOCK, BLOCK), SCORE_GROUP * BLOCK), :]
        r = _dot(kt, iqt_ref[...])
        t = i * BLOCK + qry_i
        for u in range(SCORE_GROUP):
            acc = jnp.zeros((BLOCK, BLOCK), F32)
            for h in range(IDX_HEADS):
                acc = acc + jnp.maximum(r[u * BLOCK:(u + 1) * BLOCK, hs(h)], 0.0) * wt_ref[h:h + 1, :]
            s = (j0 + u) * BLOCK + key_i
            ok = (s <= t) & ((s >= PAD) | (s == t))
            bits = lax.bitcast_convert_type(acc, jnp.int32)
            bits = jnp.where(bits == INT_MIN, 0, bits)
            key = bits ^ ((bits >> 31) & 0x7FFFFFFF)
            s_ref[j0 + u] = jnp.where(ok, key, INT_MIN)
        return 0

    lax.fori_loop(0, ngrp, score_group, 0)

    def slice_group(g, _):
        for half in range(2):
            words = []
            for u in (2 * half, 2 * half + 1):
                tile = s_ref[g * SCORE_GROUP + u] ^ INT_MIN
                words += [tile[r * 8:(r + 1) * 8, :] for r in range(BLOCK // 8)]
            planes = _bit_transpose32(words)
            for bit in range(32):
                planes_ref[2 * g + half, bit] = planes[bit]
            planes_ref[2 * g + half, 32] = jnp.full((8, BLOCK), -1, jnp.int32)
        return 0

    lax.fori_loop(0, 2 * npair, slice_group, 0)
    alive_ref[...] = jnp.full(alive_ref.shape, -1, jnp.int32)

    def radix_pass(t, carry):
        thr, need, drop_prev = carry
        bit = 31 - t

        def body(pr, accs):
            accs = list(accs)
            for u in range(4):
                st = 4 * pr + u
                alive = alive_ref[st] & (planes_ref[st, bit + 1] ^ drop_prev)
                alive_ref[st] = alive
                accs[u] = accs[u] + lax.population_count(alive & planes_ref[st, bit])
            return tuple(accs)

        zero = jnp.zeros((8, BLOCK), jnp.int32)
        accs = lax.fori_loop(0, npair, body, (zero,) * 4)
        cnt = ((accs[0] + accs[1]) + (accs[2] + accs[3])).astype(F32)
        cnt = jnp.sum(cnt, axis=0, keepdims=True).astype(jnp.int32)
        take = cnt >= need
        thr = thr | jnp.where(take, jnp.int32(1) << bit, 0)
        need = jnp.where(take, need, need - cnt)
        return thr, need, jnp.where(take, 0, -1)

    zero = jnp.zeros((1, BLOCK), jnp.int32)
    thr, _, _ = lax.fori_loop(0, 32, radix_pass, (zero, zero + topk, zero))
    thr = jnp.maximum(thr ^ INT_MIN, INT_MIN + 1)

    m_ref[...] = jnp.full(m_ref.shape, NEG, F32)
    l_ref[...] = jnp.zeros_like(l_ref)
    acc_ref[...] = jnp.zeros_like(acc_ref)

    def logits(j0, nb, dst_ref):
        w = nb * BLOCK
        dst_ref[0:w, :] = _dot(c_ref[pl.ds(pl.multiple_of(j0 * BLOCK, BLOCK), w), :], qlat_ref[...])

    def softmax_pv(j0, nb, lg_ref, bias_off=None):
        w = nb * BLOCK
        ckt = ct_ref[:, pl.ds(pl.multiple_of(j0 * BLOCK, BLOCK), w)]
        masked = jnp.where(s_ref[pl.ds(j0, nb)].reshape(w, BLOCK) >= thr, 0.0, NEG)
        for pair in range(DSA_HEADS // 2):
            cols = slice(2 * pair * BLOCK, 2 * (pair + 1) * BLOCK)
            alphas = []
            for h in (2 * pair, 2 * pair + 1):
                x = lg_ref[0:w, hs(h)] + masked
                if bias_off is not None:
                    x = x + bias_ref[h, pl.ds(bias_off, w), :]
                m_prev = m_ref[h:h + 1, :]
                m_new = jnp.maximum(m_prev, jnp.max(x, axis=0, keepdims=True))
                alpha = jnp.exp2(m_prev - m_new)
                p = jnp.exp2(x - m_new)
                l_ref[h:h + 1, :] = alpha * l_ref[h:h + 1, :] + jnp.sum(p, axis=0, keepdims=True)
                m_ref[h:h + 1, :] = m_new
                p_ref[0:w, hs(h)] = p.astype(BF16)
                alphas.append(alpha)
            pv = _dot(ckt, p_ref[0:w, cols])
            for h, alpha in zip((2 * pair, 2 * pair + 1), alphas):
                acc_ref[:, hs(h)] = acc_ref[:, hs(h)] * alpha + pv[:, (h % 2) * BLOCK:(h % 2 + 1) * BLOCK]

    near0 = jnp.maximum(i - 1, 0)
    logits(near0, 2, lgn_ref)

    n_far = jnp.maximum(i - 1, 0)
    n_big = n_far // FAR_BLOCKS
    big = lambda g: g * FAR_BLOCKS

    @pl.when(n_big > 0)
    def _():
        logits(0, FAR_BLOCKS, lg0_ref)

        def pair_of_tiles(g, _):
            logits(big(2 * g + 1), FAR_BLOCKS, lg1_ref)
            softmax_pv(big(2 * g), FAR_BLOCKS, lg0_ref)
            logits(big(jnp.minimum(2 * g + 2, n_big - 1)), FAR_BLOCKS, lg0_ref)
            softmax_pv(big(2 * g + 1), FAR_BLOCKS, lg1_ref)
            return 0

        lax.fori_loop(0, n_big // 2, pair_of_tiles, 0)

        @pl.when(n_big % 2 == 1)
        def _():
            softmax_pv(big(n_big - 1), FAR_BLOCKS, lg0_ref)

    def far_one(j, _):
        logits(j, 1, lg0_ref)
        softmax_pv(j, 1, lg0_ref)
        return 0

    lax.fori_loop(n_big * FAR_BLOCKS, n_far, far_one, 0)
    softmax_pv(near0, 2, lgn_ref, bias_off=pl.multiple_of(jnp.where(i >= 1, 0, BLOCK), BLOCK))

    inv_l = 1.0 / l_ref[...]
    for h in range(DSA_HEADS):
        o_lat = (acc_ref[:, hs(h)] * inv_l[h:h + 1, :]).astype(BF16)
        o = _dot_tn(o_lat, wuv_ref[h])
        o_ref[:, h * DSA_HEAD_DIM:(h + 1) * DSA_HEAD_DIM] = o.astype(o_ref.dtype)


def _dsa(pbf3, misc3, c3, ct3, wuk, wuv, bias, topk):
    b, t, _ = pbf3.shape
    nblk = t // BLOCK
    hq = DSA_HEADS * DSA_HEAD_DIM
    ik_blk = 6 * 1024 // LANE
    small_blk = (IN_W["rq"] + IN_W["rk"] + IN_W["rg"] + IN_W["dc"]) // LANE
    n_sets = 4 * (((nblk + SCORE_GROUP - 1) // SCORE_GROUP + 1) // 2)
    n_tiles = nblk + 2 * SCORE_GROUP - 1
    const3 = lambda bi, i: (0, 0, 0)
    return pl.pallas_call(
        functools.partial(_dsa_kernel, topk=topk, nblk=nblk),
        grid=(b, nblk),
        in_specs=[
            pl.BlockSpec((None, BLOCK, hq), lambda bi, i: (bi, i, 0)),
            pl.BlockSpec((None, BLOCK, IDX_HEADS * IDX_DIM), lambda bi, i: (bi, i, 1)),
            pl.BlockSpec((None, BLOCK, LANE), lambda bi, i: (bi, i, small_blk)),
            pl.BlockSpec((None, t, LANE), lambda bi, i: (bi, 0, ik_blk)),
            pl.BlockSpec((None, t, DSA_LATENT), lambda bi, i: (bi, 0, 0)),
            pl.BlockSpec((None, DSA_LATENT, t), lambda bi, i: (bi, 0, 0)),
            pl.BlockSpec((DSA_HEADS, DSA_LATENT, DSA_HEAD_DIM), const3),
            pl.BlockSpec((DSA_HEADS, DSA_LATENT, DSA_HEAD_DIM), const3),
            pl.BlockSpec((DSA_HEADS, 3 * BLOCK, BLOCK), const3),
        ],
        out_specs=pl.BlockSpec((None, BLOCK, hq), lambda bi, i: (bi, i, 0)),
        out_shape=jax.ShapeDtypeStruct((b, t, hq), BF16),
        scratch_shapes=[
            pltpu.VMEM((n_tiles, BLOCK, BLOCK), jnp.int32),
            pltpu.VMEM((DSA_LATENT, DSA_HEADS * BLOCK), BF16),
            pltpu.VMEM((FAR_BLOCKS * BLOCK, DSA_HEADS * BLOCK), BF16),
            pltpu.VMEM((DSA_HEADS, BLOCK), F32),
            pltpu.VMEM((DSA_HEADS, BLOCK), F32),
            pltpu.VMEM((DSA_LATENT, DSA_HEADS * BLOCK), F32),
            pltpu.VMEM((FAR_BLOCKS * BLOCK, DSA_HEADS * BLOCK), F32),
            pltpu.VMEM((FAR_BLOCKS * BLOCK, DSA_HEADS * BLOCK), F32),
            pltpu.VMEM((2 * BLOCK, DSA_HEADS * BLOCK), F32),
            pltpu.VMEM((LANE, IDX_HEADS * BLOCK), BF16),
            pltpu.VMEM((IDX_HEADS, BLOCK), F32),
            pltpu.VMEM((n_sets, 33, 8, BLOCK), jnp.int32),
            pltpu.VMEM((n_sets, 8, BLOCK), jnp.int32),
        ],
        compiler_params=_params(("parallel", "arbitrary")),
    )(pbf3, pbf3, misc3, pbf3, c3, ct3, wuk, wuv, bias)


def _dsa_weights(w_uk, w_uv):
    return jnp.transpose(w_uk, (1, 0, 2)).astype(BF16), jnp.transpose(w_uv, (1, 0, 2)).astype(BF16)


def _fox_kernel(q_ref, k_ref, v_ref, cq_ref, ck_ref, o_ref, m_ref, l_ref, acc_ref, lg_ref, *, tq):
    i = pl.program_id(2)
    cq = cq_ref[...] * LOG2E
    m_ref[...] = jnp.full(m_ref.shape, NEG, F32)
    l_ref[...] = jnp.zeros_like(l_ref)
    acc_ref[...] = jnp.zeros_like(acc_ref)

    def qk(j, slot):
        lg_ref[slot] = _dot_nt(k_ref[pl.ds(pl.multiple_of(j * tq, tq), tq), :], q_ref[...])

    def softmax_pv(j, slot, mode):
        off = pl.multiple_of(j * tq, tq)
        ck = ck_ref[pl.ds(off, tq), :] * LOG2E
        x = (lg_ref[slot] + cq) - ck
        if mode == "pad":
            s = lax.broadcasted_iota(jnp.int32, (tq, 1), 0)
            x = jnp.where(s >= PAD, x, NEG)
        elif mode == "diag":
            s = lax.broadcasted_iota(jnp.int32, (tq, tq), 0)
            t = lax.broadcasted_iota(jnp.int32, (tq, tq), 1)
            x = jnp.where((s <= t) & ((s + j * tq >= PAD) | (s == t)), x, NEG)
        m_prev = m_ref[...]
        m_new = jnp.maximum(m_prev, jnp.max(x, axis=0, keepdims=True))
        alpha = jnp.exp2(m_prev - m_new)
        p = jnp.exp2(x - m_new)
        l_ref[...] = alpha * l_ref[...] + jnp.sum(p, axis=0, keepdims=True)
        acc_ref[...] = acc_ref[...] * alpha + _dot_tn(v_ref[pl.ds(off, tq), :], p.astype(BF16))
        m_ref[...] = m_new

    qk(i, 2)

    @pl.when(i > 0)
    def _():
        qk(0, 0)
        qk(jnp.minimum(1, i), 1)
        softmax_pv(0, 0, "pad")
        n_mid = i - 1

        def pair(g, _):
            a = 2 * g + 1
            qk(jnp.minimum(a + 1, i), 0)
            softmax_pv(a, 1, "none")
            qk(jnp.minimum(a + 2, i), 1)
            softmax_pv(a + 1, 0, "none")
            return 0

        lax.fori_loop(0, n_mid // 2, pair, 0)

        @pl.when(n_mid % 2 == 1)
        def _():
            softmax_pv(i - 1, 1, "none")

    softmax_pv(i, 2, "diag")

    o_ref[...] = jnp.transpose(acc_ref[...] * (1.0 / l_ref[...])).astype(o_ref.dtype)


def _fox(pbf3, cq, ck):
    b, t, _ = pbf3.shape
    nblk = t // BLOCK
    tq = BLOCK * max(k for k in (1, 5) if nblk % k == 0)
    d = FOX_HEAD_DIM
    qb, kb, vb = 2 * 1024 // d, 3 * 1024 // d, 4 * 1024 // d
    return pl.pallas_call(
        functools.partial(_fox_kernel, tq=tq),
        grid=(b, FOX_HEADS, t // tq),
        in_specs=[
            pl.BlockSpec((None, tq, d), lambda bi, h, i: (bi, i, qb + h)),
            pl.BlockSpec((None, t, d), lambda bi, h, i: (bi, 0, kb + h)),
            pl.BlockSpec((None, t, d), lambda bi, h, i: (bi, 0, vb + h)),
            pl.BlockSpec((None, None, 1, tq), lambda bi, h, i: (bi, h, 0, i)),
            pl.BlockSpec((None, None, t, 1), lambda bi, h, i: (bi, h, 0, 0)),
        ],
        out_specs=pl.BlockSpec((None, tq, d), lambda bi, h, i: (bi, i, h)),
        out_shape=jax.ShapeDtypeStruct((b, t, FOX_HEADS * d), BF16),
        scratch_shapes=[pltpu.VMEM((1, tq), F32), pltpu.VMEM((1, tq), F32), pltpu.VMEM((d, tq), F32),
                        pltpu.VMEM((3, tq, tq), F32)],
        compiler_params=_params(("parallel", "parallel", "arbitrary")),
    )(pbf3, pbf3, pbf3, cq, ck)


def _ret_kernel(rq_ref, rk_ref, rv_ref, rg_ref, cos_ref, sin_ref, dq_ref, dk_ref, decay_ref, cd_ref, gn_ref,
                o_ref, state_ref):
    i = pl.program_id(1)

    @pl.when(i == 0)
    def _():
        state_ref[...] = jnp.zeros_like(state_ref)

    half = RET_HEADS * RET_QK_DIM // 2
    cos = cos_ref[...]
    sin = sin_ref[...]

    def rope(x):
        x1, x2 = x[:, :half], x[:, half:]
        return jnp.concatenate([x1 * cos - x2 * sin, x1 * sin + x2 * cos], axis=1)

    q = rope(rq_ref[...])
    k = rope(rk_ref[...]) * RET_QK_DIM ** -0.5
    pos = i * BLOCK + lax.broadcasted_iota(jnp.int32, (BLOCK, 1), 0)
    k = jnp.where(pos >= PAD, k, 0.0)
    qb = q.astype(BF16)
    kb = k.astype(BF16)
    q_dec = (q * dq_ref[...]).astype(BF16)
    k_dec = (k * dk_ref[...]).astype(BF16)
    head_of_col = (lax.broadcasted_iota(jnp.int32, (1, 2 * half), 1) % half) // (RET_QK_DIM // 2)
    zero = jnp.zeros((), BF16)
    for h in range(RET_HEADS):
        mine = head_of_col == h
        vh = rv_ref[:, h * RET_V_DIM:(h + 1) * RET_V_DIM]
        s = (_dot_nt(jnp.where(mine, qb, zero), kb) * decay_ref[h]).astype(BF16)
        st = state_ref[h]
        o = _dot(s, vh) + _dot(jnp.where(mine, q_dec, zero), st.astype(BF16))
        state_ref[h] = cd_ref[h] * st + _dot_tn(k_dec, vh)
        mu = jnp.mean(o, axis=1, keepdims=True)
        var = jnp.mean(jnp.square(o - mu), axis=1, keepdims=True)
        sl = slice(h * RET_V_DIM, (h + 1) * RET_V_DIM)
        o = (o - mu) * lax.rsqrt(var + GN_EPS) * gn_ref[:, sl]
        g = rg_ref[:, sl]
        o_ref[:, sl] = (g * jax.nn.sigmoid(g) * o).astype(o_ref.dtype)


def _retention(misc3, pbf3, cos, sin, dq_tab, dk_tab, decay, cd, gn_w):
    b, t, _ = misc3.shape
    qk = RET_HEADS * RET_QK_DIM
    vw = RET_HEADS * RET_V_DIM
    const2 = lambda bi, i: (0, 0)
    const3 = lambda bi, i: (0, 0, 0)
    return pl.pallas_call(
        _ret_kernel,
        grid=(b, t // BLOCK),
        in_specs=[
            pl.BlockSpec((None, BLOCK, qk), lambda bi, i: (bi, i, 0)),
            pl.BlockSpec((None, BLOCK, qk), lambda bi, i: (bi, i, 1)),
            pl.BlockSpec((None, BLOCK, vw), lambda bi, i: (bi, i, 5)),
            pl.BlockSpec((None, BLOCK, vw), lambda bi, i: (bi, i, 1)),
            pl.BlockSpec((BLOCK, qk // 2), lambda bi, i: (i, 0)),
            pl.BlockSpec((BLOCK, qk // 2), lambda bi, i: (i, 0)),
            pl.BlockSpec((BLOCK, qk), const2),
            pl.BlockSpec((BLOCK, qk), const2),
            pl.BlockSpec((RET_HEADS, BLOCK, BLOCK), const3),
            pl.BlockSpec((RET_HEADS, 1, RET_V_DIM), const3),
            pl.BlockSpec((1, vw), const2),
        ],
        out_specs=pl.BlockSpec((None, BLOCK, vw), lambda bi, i: (bi, i, 0)),
        out_shape=jax.ShapeDtypeStruct((b, t, vw), BF16),
        scratch_shapes=[pltpu.VMEM((RET_HEADS, qk, RET_V_DIM), F32)],
        compiler_params=_params(("parallel", "arbitrary")),
    )(misc3, misc3, pbf3, misc3, cos, sin, dq_tab, dk_tab, decay, cd, gn_w.reshape(1, vw))


def _merge_kernel(x_ref, g_ref, b0_ref, b1_ref, b2_ref, wg0_ref, wg1_ref, wg2_ref, wb_ref, o_ref, xn_ref):
    @pl.when(pl.program_id(1) == 0)
    def _():
        xn_ref[...] = _rms(x_ref[...], g_ref[...]).astype(BF16)

    xn = xn_ref[...]
    acc = jax.nn.sigmoid(_dot(xn, wg0_ref[...])) * _dot(b0_ref[...], wb_ref[0])
    acc = acc + jax.nn.sigmoid(_dot(xn, wg1_ref[...])) * _dot(b1_ref[...], wb_ref[1])
    acc = acc + jax.nn.sigmoid(_dot(xn, wg2_ref[...])) * _dot(b2_ref[...], wb_ref[2])
    o_ref[...] = acc.astype(o_ref.dtype)


def _merge(x, g, b0, b1, b2, w_gates, w_branch):
    n, bw = b0.shape
    dm = x.shape[1]
    d = w_branch.shape[2]
    tm = _pick_tile(n, 640)
    tn = _pick_tile(d, 512, LANE)
    nj = d // tn
    bspec = pl.BlockSpec((tm, bw), lambda i, j: (i, 0))
    gspec = lambda r: pl.BlockSpec((dm, tn), lambda i, j: (0, r * nj + j))
    return pl.pallas_call(
        _merge_kernel,
        grid=(n // tm, nj),
        in_specs=[pl.BlockSpec((tm, dm), lambda i, j: (i, 0)), pl.BlockSpec((1, dm), lambda i, j: (0, 0)),
                  bspec, bspec, bspec, gspec(0), gspec(1), gspec(2),
                  pl.BlockSpec((N_BRANCH, bw, tn), lambda i, j: (0, 0, j))],
        out_specs=pl.BlockSpec((tm, tn), lambda i, j: (i, j)),
        out_shape=jax.ShapeDtypeStruct((n, d), BF16),
        scratch_shapes=[pltpu.VMEM((tm, dm), BF16)],
        compiler_params=_params(("parallel", "arbitrary")),
    )(x, g.reshape(1, dm), b0, b1, b2, w_gates, w_gates, w_gates, w_branch)


def _resproj_kernel(a_ref, w_ref, r_ref, o_ref):
    o_ref[...] = r_ref[...] + _dot(a_ref[...], w_ref[...])


def _resproj(a, w, res):
    n, k = a.shape
    d = w.shape[1]
    tm = _pick_tile(n, 640)
    tn = _pick_tile(d, 512, LANE)
    return pl.pallas_call(
        _resproj_kernel,
        grid=(n // tm, d // tn),
        in_specs=[
            pl.BlockSpec((tm, k), lambda i, j: (i, 0)),
            pl.BlockSpec((k, tn), lambda i, j: (0, j)),
            pl.BlockSpec((tm, tn), lambda i, j: (i, j)),
        ],
        out_specs=pl.BlockSpec((tm, tn), lambda i, j: (i, j)),
        out_shape=jax.ShapeDtypeStruct((n, d), F32),
        compiler_params=_params(("parallel", "arbitrary")),
    )(a, w, res)


def _t5_bucket(dist):
    d = jnp.maximum(dist, 1).astype(F32)
    large = T5_MAX_EXACT + (jnp.log(d / T5_MAX_EXACT) / math.log(T5_MAX_DIST / T5_MAX_EXACT)
                            * (T5_BUCKETS - T5_MAX_EXACT)).astype(jnp.int32)
    large = jnp.minimum(large, T5_BUCKETS - 1)
    return jnp.where(dist < T5_MAX_EXACT, dist, large)


def _t5_bias_tiles(t5_table):
    k = jnp.arange(BLOCK, dtype=jnp.int32)[:, None]
    q = jnp.arange(BLOCK, dtype=jnp.int32)[None, :]
    table = t5_table.astype(F32)
    far = table[T5_BUCKETS - 1]
    bdiag = jnp.moveaxis(table[_t5_bucket(jnp.maximum(q - k, 0))] - far, -1, 0)
    bprev = jnp.moveaxis(table[_t5_bucket(BLOCK + q - k)] - far, -1, 0)
    return jnp.concatenate([bprev, bdiag, jnp.zeros_like(bdiag)], axis=1) * LOG2E


def _rope_tables(t):
    half = RET_QK_DIM // 2
    inv = ROPE_BASE ** (-jnp.arange(half, dtype=F32) / half)
    ang = jnp.arange(t, dtype=jnp.int32).astype(F32)[:, None] * inv[None, :]
    return jnp.tile(jnp.cos(ang), (1, RET_HEADS)), jnp.tile(jnp.sin(ang), (1, RET_HEADS))


def _decay_tables():
    log_gamma = jnp.log1p(-(2.0 ** (-5.0 - jnp.arange(RET_HEADS, dtype=F32))))
    i = jnp.arange(BLOCK, dtype=F32)
    gap = i[:, None] - i[None, :]
    decay = jnp.where(gap[None] >= 0, jnp.exp(jnp.maximum(gap, 0.0)[None] * log_gamma[:, None, None]), 0.0)
    head_of_col = (np.arange(RET_HEADS * RET_QK_DIM) % (RET_HEADS * RET_QK_DIM // 2)) // (RET_QK_DIM // 2)
    lg_col = log_gamma[head_of_col]
    dk_tab = jnp.exp((BLOCK - 1 - i)[:, None] * lg_col[None, :])
    dq_tab = jnp.exp((i + 1.0)[:, None] * lg_col[None, :])
    cd = jnp.broadcast_to(jnp.exp(BLOCK * log_gamma)[:, None, None], (RET_HEADS, 1, RET_V_DIM))
    return decay.astype(F32), dq_tab, dk_tab, cd


def _cast_kernel(w_ref, o_ref):
    o_ref[...] = w_ref[...].astype(o_ref.dtype)


def _layer_bf16(w, l):
    _, r, c = w.shape
    tr = _pick_tile(r, max(16, (6 * 2 ** 20) // (4 * c)), 16)
    return pl.pallas_call(
        _cast_kernel,
        grid=(r // tr,),
        in_specs=[pl.BlockSpec((None, tr, c), lambda i: (l, i, 0))],
        out_specs=pl.BlockSpec((tr, c), lambda i: (i, 0)),
        out_shape=jax.ShapeDtypeStruct((r, c), BF16),
        compiler_params=_params(("parallel",)),
    )(w)


def _split_w_in(w_in, w16):
    cols = lambda name: w16[:, IN_OFF[name]:IN_OFF[name] + IN_W[name]]
    zeros = lambda w: jnp.zeros((w_in.shape[0], w), BF16)
    rope = lambda a: jnp.swapaxes(a.reshape(-1, RET_HEADS, 2, RET_QK_DIM // 2), 1, 2).reshape(a.shape)
    fq = (w_in[:, IN_OFF["fq"]:IN_OFF["fq"] + IN_W["fq"]] * (FOX_HEAD_DIM ** -0.5 * LOG2E)).astype(BF16)
    w_bf = jnp.concatenate([cols("dq"), cols("iq"), fq, cols("fk"), cols("fv"), cols("rv"),
                            cols("ik"), zeros(LANE - IDX_DIM)], axis=1)
    w_misc = jnp.concatenate([rope(cols("rq")), rope(cols("rk")), cols("rg"), cols("dc"),
                              cols("iw"), cols("ff"), zeros(LANE - IDX_HEADS - FOX_HEADS),
                              zeros(LANE)], axis=1)
    assert w_bf.shape[1] == BF_COLS and w_misc.shape[1] == MISC_COLS
    return w_bf, w_misc, cols("gates")


def _mixer(x, b, t, mix_norm, w_in, w_in16, kv_norm, w_uk, w_uv, f_bias, gn_w, w_branch16, w_out16, tables, topk):
    n = b * t
    bias_tiles, rope_tabs, decay_tabs = tables
    w_bf, w_misc, w_gates = _split_w_in(w_in, w_in16)
    pbf = _proj(x, mix_norm, w_bf, BF16)
    misc = _proj(x, mix_norm, w_misc, F32)
    pbf3 = pbf.reshape(b, t, BF_COLS)
    misc3 = misc.reshape(b, t, MISC_COLS)

    c3 = _latent(misc, kv_norm).reshape(b, t, DSA_LATENT)
    wuk, wuv = _dsa_weights(w_uk, w_uv)
    dsa_o = _dsa(pbf3, misc3, c3, jnp.swapaxes(c3, 1, 2), wuk, wuv, bias_tiles, topk)

    cum = _cum_logf(misc3, f_bias)[:, :, SMALL_FF:SMALL_FF + FOX_HEADS]
    cum_h = jnp.swapaxes(cum, 1, 2)
    fox_o = _fox(pbf3, cum_h[:, :, None, :], cum_h[..., None])

    ret_o = _retention(misc3, pbf3, *rope_tabs, *decay_tabs, gn_w)

    merged = _merge(x, mix_norm, dsa_o.reshape(n, -1), fox_o.reshape(n, -1), ret_o.reshape(n, -1), w_gates,
                    w_branch16)
    return _resproj(merged, w_out16, x)


def kernel(x, meta_tokens, t5_table, ffn1_norm, ffn1_w1, ffn1_w3, ffn1_w2, mix_norm, w_in, dsa_kv_norm,
           dsa_w_uk, dsa_w_uv, fox_f_bias, ret_gn_w, w_branch, w_out, ffn2_norm, ffn2_w1, ffn2_w3, ffn2_w2,
           final_norm):
    b, seq, d = x.shape
    assert d == D_MODEL and seq % BLOCK == 0
    t = seq + BLOCK
    depth = ffn1_norm.shape[0]
    topk = min(INDEX_TOPK, seq // 4)
    h = jnp.concatenate([
        jnp.zeros((b, PAD, d), x.dtype),
        jnp.broadcast_to(meta_tokens.astype(x.dtype)[None], (b, N_META, d)),
        x,
    ], axis=1).reshape(b * t, d)
    cos, sin = _rope_tables(t)
    decay, dq_tab, dk_tab, cd = _decay_tables()
    tables = (_t5_bias_tiles(t5_table), (cos, sin), (dq_tab, dk_tab, decay, cd))
    n_br, bw = w_branch.shape[1:3]
    w_branch_rows = w_branch.reshape(depth, n_br * bw, d)
    for l in range(depth):
        h = _ffn(h, ffn1_norm[l], _layer_bf16(ffn1_w1, l), _layer_bf16(ffn1_w3, l), _layer_bf16(ffn1_w2, l))
        h = _mixer(h, b, t, mix_norm[l], w_in[l], _layer_bf16(w_in, l), dsa_kv_norm[l], dsa_w_uk[l], dsa_w_uv[l],
                   fox_f_bias[l], ret_gn_w[l], _layer_bf16(w_branch_rows, l).reshape(n_br, bw, d),
                   _layer_bf16(w_out, l), tables, topk)
        h = _ffn(h, ffn2_norm[l], _layer_bf16(ffn2_w1, l), _layer_bf16(ffn2_w3, l), _layer_bf16(ffn2_w2, l),
                 final_g=final_norm if l == depth - 1 else None)
    return h.reshape(b, t, d)[:, BLOCK:]
```

```python
import functools
import math

import numpy as np
import jax
import jax.numpy as jnp
from jax import lax
from jax.experimental import pallas as pl
from jax.experimental.pallas import tpu as pltpu

F32 = jnp.float32
BF16 = jnp.bfloat16

LANE = 128
VMEM_LIMIT = 56 * 1024 * 1024

D_MODEL = 2048
N_META = 16
BLOCK = 128
PAD = BLOCK - N_META
RMS_EPS = 1e-6
GN_EPS = 1e-5
NEG = -1e30

DSA_HEADS = 8
DSA_HEAD_DIM = 128
DSA_LATENT = 256
IDX_HEADS = 16
IDX_DIM = 64
INDEX_TOPK = 256
FOX_HEADS = 8
FOX_HEAD_DIM = 128
RET_HEADS = 8
RET_QK_DIM = 64
RET_V_DIM = 128
ROPE_BASE = 10000.0
BRANCH_WIDTH = 1024
N_BRANCH = 3
T5_BUCKETS = 32
T5_MAX_EXACT = 16
T5_MAX_DIST = 128

IN_WIDTHS = (1024, 256, 1024, 64, 16, 1024, 1024, 1024, 8, 512, 512, 1024, 1024, 6144)
IN_NAMES = ("dq", "dc", "iq", "ik", "iw", "fq", "fk", "fv", "ff", "rq", "rk", "rv", "rg", "gates")
IN_OFF = dict(zip(IN_NAMES, np.cumsum((0,) + IN_WIDTHS[:-1]).tolist()))
IN_W = dict(zip(IN_NAMES, IN_WIDTHS))

INT_MIN = -2 ** 31
LOG2E = math.log2(math.e)
SCORE_GROUP = 4

BF_COLS = 6 * 1024 + LANE
MISC_COLS = 2560
SMALL_IW = 0
SMALL_FF = IDX_HEADS


def _dot(a, b):
    return jnp.dot(a, b, preferred_element_type=F32)


def _dot_nt(a, b):
    return lax.dot_general(a, b, (((1,), (1,)), ((), ())), preferred_element_type=F32)


def _dot_tn(a, b):
    return lax.dot_general(a, b, (((0,), (0,)), ((), ())), preferred_element_type=F32)


def _pick_tile(n, target, mult=8):
    best = None
    for t in range(mult, min(n, target) + 1, mult):
        if n % t == 0:
            best = t
    assert best is not None, (n, target, mult)
    return best


def _params(sem):
    return pltpu.CompilerParams(dimension_semantics=sem, vmem_limit_bytes=VMEM_LIMIT)


def _rms(x, g):
    return x * lax.rsqrt(jnp.mean(x * x, axis=-1, keepdims=True) + RMS_EPS) * g


def _ffn_kernel(*refs, nj, final):
    if final:
        x_ref, g_ref, w1_ref, w3_ref, w2_ref, fg_ref, o_ref, xn_ref, acc_ref = refs
    else:
        x_ref, g_ref, w1_ref, w3_ref, w2_ref, o_ref, xn_ref, acc_ref = refs
    j = pl.program_id(1)

    @pl.when(j == 0)
    def _():
        xn_ref[...] = _rms(x_ref[...], g_ref[...]).astype(BF16)
        acc_ref[...] = jnp.zeros_like(acc_ref)

    xn = xn_ref[...]
    h1 = _dot(xn, w1_ref[...])
    h3 = _dot(xn, w3_ref[...])
    g = (h1 * jax.nn.sigmoid(h1) * h3).astype(BF16)
    acc_ref[...] += _dot(g, w2_ref[...])

    @pl.when(j == nj - 1)
    def _():
        y = x_ref[...] + 0.5 * acc_ref[...]
        if final:
            y = _rms(y, fg_ref[...])
        o_ref[...] = y


def _ffn(x, g, w1, w3, w2, final_g=None):
    n, d = x.shape
    f = w1.shape[1]
    tm = _pick_tile(n, 640)
    tf = _pick_tile(f, 512, LANE)
    nj = f // tf
    final = final_g is not None
    in_specs = [
        pl.BlockSpec((tm, d), lambda i, j: (i, 0)),
        pl.BlockSpec((1, d), lambda i, j: (0, 0)),
        pl.BlockSpec((d, tf), lambda i, j: (0, j)),
        pl.BlockSpec((d, tf), lambda i, j: (0, j)),
        pl.BlockSpec((tf, d), lambda i, j: (j, 0)),
    ]
    args = [x, g.reshape(1, d), w1, w3, w2]
    if final:
        in_specs.append(pl.BlockSpec((1, d), lambda i, j: (0, 0)))
        args.append(final_g.reshape(1, d))
    return pl.pallas_call(
        functools.partial(_ffn_kernel, nj=nj, final=final),
        grid=(n // tm, nj),
        in_specs=in_specs,
        out_specs=pl.BlockSpec((tm, d), lambda i, j: (i, 0)),
        out_shape=jax.ShapeDtypeStruct((n, d), F32),
        scratch_shapes=[pltpu.VMEM((tm, d), BF16), pltpu.VMEM((tm, d), F32)],
        compiler_params=_params(("parallel", "arbitrary")),
    )(*args)


def _proj_kernel(x_ref, g_ref, w_ref, o_ref, xn_ref):
    @pl.when(pl.program_id(1) == 0)
    def _():
        xn_ref[...] = _rms(x_ref[...], g_ref[...]).astype(BF16)

    o_ref[...] = _dot(xn_ref[...], w_ref[...]).astype(o_ref.dtype)


def _proj(x, g, w, out_dtype):
    n, d = x.shape
    c = w.shape[1]
    tm = _pick_tile(n, 640)
    tn = _pick_tile(c, 1024, LANE)
    return pl.pallas_call(
        _proj_kernel,
        grid=(n // tm, c // tn),
        in_specs=[
            pl.BlockSpec((tm, d), lambda i, j: (i, 0)),
            pl.BlockSpec((1, d), lambda i, j: (0, 0)),
            pl.BlockSpec((d, tn), lambda i, j: (0, j)),
        ],
        out_specs=pl.BlockSpec((tm, tn), lambda i, j: (i, j)),
        out_shape=jax.ShapeDtypeStruct((n, c), out_dtype),
        scratch_shapes=[pltpu.VMEM((tm, d), BF16)],
        compiler_params=_params(("parallel", "arbitrary")),
    )(x, g.reshape(1, d), w)


def _latent_kernel(dc_ref, g_ref, c_ref):
    c_ref[...] = _rms(dc_ref[...], g_ref[...]).astype(BF16)


def _latent(misc, kv_norm):
    n = misc.shape[0]
    tm = _pick_tile(n, 1280)
    blk = (IN_W["rq"] + IN_W["rk"] + IN_W["rg"]) // DSA_LATENT
    return pl.pallas_call(
        _latent_kernel,
        grid=(n // tm,),
        in_specs=[
            pl.BlockSpec((tm, DSA_LATENT), lambda i: (i, blk)),
            pl.BlockSpec((1, DSA_LATENT), lambda i: (0, 0)),
        ],
        out_specs=pl.BlockSpec((tm, DSA_LATENT), lambda i: (i, 0)),
        out_shape=jax.ShapeDtypeStruct((n, DSA_LATENT), BF16),
        compiler_params=_params(("parallel",)),
    )(misc, kv_norm.reshape(1, DSA_LATENT))


def _split3(x):
    hi = x.astype(BF16)
    r = x - hi.astype(F32)
    mid = r.astype(BF16)
    lo = (r - mid.astype(F32)).astype(BF16)
    return hi, mid, lo


def _cum_kernel(s_ref, b_ref, o_ref, carry_ref):
    @pl.when(pl.program_id(1) == 0)
    def _():
        carry_ref[...] = jnp.zeros_like(carry_ref)

    z = s_ref[...] + b_ref[...]
    lf = jnp.minimum(z, 0.0) - jnp.log1p(jnp.exp(-jnp.abs(z)))
    r = lax.broadcasted_iota(jnp.int32, (BLOCK, BLOCK), 0)
    c = lax.broadcasted_iota(jnp.int32, (BLOCK, BLOCK), 1)
    tri = jnp.where(c <= r, 1.0, 0.0).astype(BF16)
    hi, mid, lo = _split3(lf)
    cum = (_dot(tri, lo) + _dot(tri, mid)) + _dot(tri, hi) + carry_ref[...]
    o_ref[...] = cum
    carry_ref[...] = cum[BLOCK - 1:BLOCK, :]


def _cum_logf(misc3, f_bias):
    b, t, _ = misc3.shape
    blk = (IN_W["rq"] + IN_W["rk"] + IN_W["rg"] + IN_W["dc"]) // LANE
    bias = jnp.zeros((1, LANE), F32).at[0, SMALL_FF:SMALL_FF + FOX_HEADS].set(f_bias)
    return pl.pallas_call(
        _cum_kernel,
        grid=(b, t // BLOCK),
        in_specs=[
            pl.BlockSpec((None, BLOCK, LANE), lambda bi, i: (bi, i, blk)),
            pl.BlockSpec((1, LANE), lambda bi, i: (0, 0)),
        ],
        out_specs=pl.BlockSpec((None, BLOCK, LANE), lambda bi, i: (bi, i, 0)),
        out_shape=jax.ShapeDtypeStruct((b, t, LANE), F32),
        scratch_shapes=[pltpu.VMEM((1, LANE), F32)],
        compiler_params=_params(("parallel", "arbitrary")),
    )(misc3, bias)


def _bit_transpose32(words):
    a = list(words)
    j, m = 16, 0x0000FFFF
    while j:
        mask = jnp.int32(m if m < 2 ** 31 else m - 2 ** 32)
        for k in range(32):
            if not k & j:
                t = (lax.shift_right_logical(a[k], jnp.int32(j)) ^ a[k + j]) & mask
                a[k + j] = a[k + j] ^ t
                a[k] = a[k] ^ (t << j)
        j >>= 1
        m = (m ^ (m << j)) & 0xFFFFFFFF
    return a


FAR_BLOCKS = 4


def _dsa_kernel(dq_ref, iq_ref, sm_ref, ik_ref, c_ref, ct_ref, wuk_ref, wuv_ref, bias_ref,
                o_ref, s_ref, qlat_ref, p_ref, m_ref, acc_ref, lg0_ref, lg1_ref, lgn_ref, iqt_ref, wt_ref,
                planes_ref, alive_ref, *, topk, nblk):
    i = pl.program_id(1)
    key_i = lax.broadcasted_iota(jnp.int32, (BLOCK, BLOCK), 0)
    qry_i = lax.broadcasted_iota(jnp.int32, (BLOCK, BLOCK), 1)
    hs = lambda h: slice(h * BLOCK, (h + 1) * BLOCK)

    iq_t = jnp.transpose(iq_ref[...].astype(F32)).astype(BF16)
    for h in range(IDX_HEADS):
        iqt_ref[0:IDX_DIM, hs(h)] = iq_t[h * IDX_DIM:(h + 1) * IDX_DIM, :]
    iqt_ref[IDX_DIM:LANE, :] = jnp.zeros((LANE - IDX_DIM, IDX_HEADS * BLOCK), BF16)
    wt_ref[...] = jnp.transpose(sm_ref[...])[SMALL_IW:SMALL_IW + IDX_HEADS, :] * (IDX_HEADS ** -0.5 * IDX_DIM ** -0.5)

    for h in range(DSA_HEADS):
        ql = _dot_nt(wuk_ref[h], dq_ref[:, h * DSA_HEAD_DIM:(h + 1) * DSA_HEAD_DIM]) * (DSA_HEAD_DIM ** -0.5 * LOG2E)
        qlat_ref[:, hs(h)] = ql.astype(BF16)

    ngrp = (i + SCORE_GROUP) // SCORE_GROUP
    npair = (ngrp + 1) // 2
    for u in range(1, 2 * SCORE_GROUP):
        s_ref[i + u] = jnp.full((BLOCK, BLOCK), INT_MIN, jnp.int32)

    def score_group(g, _):
        j0 = jnp.minimum(g * SCORE_GROUP, nblk - SCORE_GROUP)
        kt = ik_ref[pl.ds(pl.multiple_of(j0 * BLOCK, BLOCK), SCORE_GROUP * BLOCK), :]
        r = _dot(kt, iqt_ref[...])
        t = i * BLOCK + qry_i
        for u in range(SCORE_GROUP):
            acc = jnp.zeros((BLOCK, BLOCK), F32)
            for h in range(IDX_HEADS):
                acc = acc + jnp.maximum(r[u * BLOCK:(u + 1) * BLOCK, hs(h)], 0.0) * wt_ref[h:h + 1, :]
            s = (j0 + u) * BLOCK + key_i
            ok = (s <= t) & ((s >= PAD) | (s == t))
            bits = lax.bitcast_convert_type(acc, jnp.int32)
            bits = jnp.where(bits == INT_MIN, 0, bits)
            key = bits ^ ((bits >> 31) & 0x7FFFFFFF)
            s_ref[j0 + u] = jnp.where(ok, key, INT_MIN)
        return 0

    lax.fori_loop(0, ngrp, score_group, 0)

    def slice_group(g, _):
        for half in range(2):
            words = []
            for u in (2 * half, 2 * half + 1):
                tile = s_ref[g * SCORE_GROUP + u] ^ INT_MIN
                words += [tile[r * 8:(r + 1) * 8, :] for r in range(BLOCK // 8)]
            planes = _bit_transpose32(words)
            for bit in range(32):
                planes_ref[2 * g + half, bit] = planes[bit]
            planes_ref[2 * g + half, 32] = jnp.full((8, BLOCK), -1, jnp.int32)
        return 0

    lax.fori_loop(0, 2 * npair, slice_group, 0)
    alive_ref[...] = jnp.full(alive_ref.shape, -1, jnp.int32)

    def radix_pass(t, carry):
        thr, need, drop_prev = carry
        bit = 31 - t

        def body(pr, accs):
            accs = list(accs)
            for u in range(4):
                st = 4 * pr + u
                alive = alive_ref[st] & (planes_ref[st, bit + 1] ^ drop_prev)
                alive_ref[st] = alive
                accs[u] = accs[u] + lax.population_count(alive & planes_ref[st, bit])
            return tuple(accs)

        zero = jnp.zeros((8, BLOCK), jnp.int32)
        accs = lax.fori_loop(0, npair, body, (zero,) * 4)
        cnt = ((accs[0] + accs[1]) + (accs[2] + accs[3])).astype(F32)
        cnt = jnp.sum(cnt, axis=0, keepdims=True).astype(jnp.int32)
        take = cnt >= need
        thr = thr | jnp.where(take, jnp.int32(1) << bit, 0)
        need = jnp.where(take, need, need - cnt)
        return thr, need, jnp.where(take, 0, -1)

    zero = jnp.zeros((1, BLOCK), jnp.int32)
    thr, _, _ = lax.fori_loop(0, 32, radix_pass, (zero, zero + topk, zero))
    thr = jnp.maximum(thr ^ INT_MIN, INT_MIN + 1)

    m_ref[...] = jnp.full(m_ref.shape, NEG, F32)
    acc_ref[...] = jnp.zeros_like(acc_ref)

    def logits(j0, nb, dst_ref):
        w = nb * BLOCK
        dst_ref[0:w, :] = _dot(c_ref[pl.ds(pl.multiple_of(j0 * BLOCK, BLOCK), w), :], qlat_ref[...])

    def softmax_pv(j0, nb, lg_ref, bias_off=None):
        w = nb * BLOCK
        ckt = ct_ref[:, pl.ds(pl.multiple_of(j0 * BLOCK, BLOCK), w)]
        masked = jnp.where(s_ref[pl.ds(j0, nb)].reshape(w, BLOCK) >= thr, 0.0, NEG)
        for pair in range(DSA_HEADS // 2):
            cols = slice(2 * pair * BLOCK, 2 * (pair + 1) * BLOCK)
            alphas = []
            for h in (2 * pair, 2 * pair + 1):
                x = lg_ref[0:w, hs(h)] + masked
                if bias_off is not None:
                    x = x + bias_ref[h, pl.ds(bias_off, w), :]
                m_prev = m_ref[h:h + 1, :]
                m_new = jnp.maximum(m_prev, jnp.max(x, axis=0, keepdims=True))
                alpha = jnp.exp2(m_prev - m_new)
                p = jnp.exp2(x - m_new)
                m_ref[h:h + 1, :] = m_new
                p_ref[0:w, hs(h)] = p.astype(BF16)
                alphas.append(alpha)
            pv = _dot(ckt, p_ref[0:w, cols])
            for h, alpha in zip((2 * pair, 2 * pair + 1), alphas):
                acc_ref[:, hs(h)] = acc_ref[:, hs(h)] * alpha + pv[:, (h % 2) * BLOCK:(h % 2 + 1) * BLOCK]

    near0 = jnp.maximum(i - 1, 0)
    logits(near0, 2, lgn_ref)

    n_far = jnp.maximum(i - 1, 0)
    n_big = n_far // FAR_BLOCKS
    big = lambda g: g * FAR_BLOCKS

    @pl.when(n_big > 0)
    def _():
        logits(0, FAR_BLOCKS, lg0_ref)

        def pair_of_tiles(g, _):
            logits(big(2 * g + 1), FAR_BLOCKS, lg1_ref)
            softmax_pv(big(2 * g), FAR_BLOCKS, lg0_ref)
            logits(big(jnp.minimum(2 * g + 2, n_big - 1)), FAR_BLOCKS, lg0_ref)
            softmax_pv(big(2 * g + 1), FAR_BLOCKS, lg1_ref)
            return 0

        lax.fori_loop(0, n_big // 2, pair_of_tiles, 0)

        @pl.when(n_big % 2 == 1)
        def _():
            softmax_pv(big(n_big - 1), FAR_BLOCKS, lg0_ref)

    def far_one(j, _):
        logits(j, 1, lg0_ref)
        softmax_pv(j, 1, lg0_ref)
        return 0

    lax.fori_loop(n_big * FAR_BLOCKS, n_far, far_one, 0)
    softmax_pv(near0, 2, lgn_ref, bias_off=pl.multiple_of(jnp.where(i >= 1, 0, BLOCK), BLOCK))

    for h in range(DSA_HEADS):
        inv_l = 1.0 / acc_ref[DSA_LATENT:DSA_LATENT + 1, hs(h)]
        o_lat = (acc_ref[0:DSA_LATENT, hs(h)] * inv_l).astype(BF16)
        o = _dot_tn(o_lat, wuv_ref[h])
        o_ref[:, h * DSA_HEAD_DIM:(h + 1) * DSA_HEAD_DIM] = o.astype(o_ref.dtype)


def _dsa(pbf3, misc3, c3, ct3, wuk, wuv, bias, topk):
    b, t, _ = pbf3.shape
    nblk = t // BLOCK
    hq = DSA_HEADS * DSA_HEAD_DIM
    ik_blk = 6 * 1024 // LANE
    small_blk = (IN_W["rq"] + IN_W["rk"] + IN_W["rg"] + IN_W["dc"]) // LANE
    n_sets = 4 * (((nblk + SCORE_GROUP - 1) // SCORE_GROUP + 1) // 2)
    n_tiles = nblk + 2 * SCORE_GROUP - 1
    const3 = lambda bi, i: (0, 0, 0)
    return pl.pallas_call(
        functools.partial(_dsa_kernel, topk=topk, nblk=nblk),
        grid=(b, nblk),
        in_specs=[
            pl.BlockSpec((None, BLOCK, hq), lambda bi, i: (bi, i, 0)),
            pl.BlockSpec((None, BLOCK, IDX_HEADS * IDX_DIM), lambda bi, i: (bi, i, 1)),
            pl.BlockSpec((None, BLOCK, LANE), lambda bi, i: (bi, i, small_blk)),
            pl.BlockSpec((None, t, LANE), lambda bi, i: (bi, 0, ik_blk)),
            pl.BlockSpec((None, t, DSA_LATENT), lambda bi, i: (bi, 0, 0)),
            pl.BlockSpec((None, DSA_LATENT + 8, t), lambda bi, i: (bi, 0, 0)),
            pl.BlockSpec((DSA_HEADS, DSA_LATENT, DSA_HEAD_DIM), const3),
            pl.BlockSpec((DSA_HEADS, DSA_LATENT, DSA_HEAD_DIM), const3),
            pl.BlockSpec((DSA_HEADS, 3 * BLOCK, BLOCK), const3),
        ],
        out_specs=pl.BlockSpec((None, BLOCK, hq), lambda bi, i: (bi, i, 0)),
        out_shape=jax.ShapeDtypeStruct((b, t, hq), BF16),
        scratch_shapes=[
            pltpu.VMEM((n_tiles, BLOCK, BLOCK), jnp.int32),
            pltpu.VMEM((DSA_LATENT, DSA_HEADS * BLOCK), BF16),
            pltpu.VMEM((FAR_BLOCKS * BLOCK, DSA_HEADS * BLOCK), BF16),
            pltpu.VMEM((DSA_HEADS, BLOCK), F32),
            pltpu.VMEM((DSA_LATENT + 8, DSA_HEADS * BLOCK), F32),
            pltpu.VMEM((FAR_BLOCKS * BLOCK, DSA_HEADS * BLOCK), F32),
            pltpu.VMEM((FAR_BLOCKS * BLOCK, DSA_HEADS * BLOCK), F32),
            pltpu.VMEM((2 * BLOCK, DSA_HEADS * BLOCK), F32),
            pltpu.VMEM((LANE, IDX_HEADS * BLOCK), BF16),
            pltpu.VMEM((IDX_HEADS, BLOCK), F32),
            pltpu.VMEM((n_sets, 33, 8, BLOCK), jnp.int32),
            pltpu.VMEM((n_sets, 8, BLOCK), jnp.int32),
        ],
        compiler_params=_params(("parallel", "arbitrary")),
    )(pbf3, pbf3, misc3, pbf3, c3, ct3, wuk, wuv, bias)


def _dsa_weights(w_uk, w_uv):
    return jnp.transpose(w_uk, (1, 0, 2)).astype(BF16), jnp.transpose(w_uv, (1, 0, 2)).astype(BF16)


def _fox_kernel(q_ref, k_ref, v_ref, cq_ref, ck_ref, o_ref, m_ref, l_ref, acc_ref, lg_ref, *, tq):
    i = pl.program_id(2)
    cq = cq_ref[...] * LOG2E
    m_ref[...] = jnp.full(m_ref.shape, NEG, F32)
    l_ref[...] = jnp.zeros_like(l_ref)
    acc_ref[...] = jnp.zeros_like(acc_ref)

    def qk(j, slot):
        lg_ref[slot] = _dot_nt(k_ref[pl.ds(pl.multiple_of(j * tq, tq), tq), :], q_ref[...])

    def softmax_pv(j, slot, mode):
        off = pl.multiple_of(j * tq, tq)
        ck = ck_ref[pl.ds(off, tq), :] * LOG2E
        x = (lg_ref[slot] + cq) - ck
        if mode == "pad":
            s = lax.broadcasted_iota(jnp.int32, (tq, 1), 0)
            x = jnp.where(s >= PAD, x, NEG)
        elif mode == "diag":
            s = lax.broadcasted_iota(jnp.int32, (tq, tq), 0)
            t = lax.broadcasted_iota(jnp.int32, (tq, tq), 1)
            x = jnp.where((s <= t) & ((s + j * tq >= PAD) | (s == t)), x, NEG)
        m_prev = m_ref[...]
        m_new = jnp.maximum(m_prev, jnp.max(x, axis=0, keepdims=True))
        alpha = jnp.exp2(m_prev - m_new)
        p = jnp.exp2(x - m_new)
        l_ref[...] = alpha * l_ref[...] + jnp.sum(p, axis=0, keepdims=True)
        acc_ref[...] = acc_ref[...] * alpha + _dot_tn(v_ref[pl.ds(off, tq), :], p.astype(BF16))
        m_ref[...] = m_new

    qk(i, 2)

    @pl.when(i > 0)
    def _():
        qk(0, 0)
        qk(jnp.minimum(1, i), 1)
        softmax_pv(0, 0, "pad")
        n_mid = i - 1

        def pair(g, _):
            a = 2 * g + 1
            qk(jnp.minimum(a + 1, i), 0)
            softmax_pv(a, 1, "none")
            qk(jnp.minimum(a + 2, i), 1)
            softmax_pv(a + 1, 0, "none")
            return 0

        lax.fori_loop(0, n_mid // 2, pair, 0)

        @pl.when(n_mid % 2 == 1)
        def _():
            softmax_pv(i - 1, 1, "none")

    softmax_pv(i, 2, "diag")

    o_ref[...] = jnp.transpose(acc_ref[...] * (1.0 / l_ref[...])).astype(o_ref.dtype)


def _fox(pbf3, cq, ck):
    b, t, _ = pbf3.shape
    nblk = t // BLOCK
    tq = BLOCK * max(k for k in (1, 5) if nblk % k == 0)
    d = FOX_HEAD_DIM
    qb, kb, vb = 2 * 1024 // d, 3 * 1024 // d, 4 * 1024 // d
    return pl.pallas_call(
        functools.partial(_fox_kernel, tq=tq),
        grid=(b, FOX_HEADS, t // tq),
        in_specs=[
            pl.BlockSpec((None, tq, d), lambda bi, h, i: (bi, i, qb + h)),
            pl.BlockSpec((None, t, d), lambda bi, h, i: (bi, 0, kb + h)),
            pl.BlockSpec((None, t, d), lambda bi, h, i: (bi, 0, vb + h)),
            pl.BlockSpec((None, None, 1, tq), lambda bi, h, i: (bi, h, 0, i)),
            pl.BlockSpec((None, None, t, 1), lambda bi, h, i: (bi, h, 0, 0)),
        ],
        out_specs=pl.BlockSpec((None, tq, d), lambda bi, h, i: (bi, i, h)),
        out_shape=jax.ShapeDtypeStruct((b, t, FOX_HEADS * d), BF16),
        scratch_shapes=[pltpu.VMEM((1, tq), F32), pltpu.VMEM((1, tq), F32), pltpu.VMEM((d, tq), F32),
                        pltpu.VMEM((3, tq, tq), F32)],
        compiler_params=_params(("parallel", "parallel", "arbitrary")),
    )(pbf3, pbf3, pbf3, cq, ck)


def _ret_kernel(rq_ref, rk_ref, rv_ref, rg_ref, cos_ref, sin_ref, dq_ref, dk_ref, decay_ref, cd_ref, gn_ref,
                o_ref, state_ref):
    i = pl.program_id(1)

    @pl.when(i == 0)
    def _():
        state_ref[...] = jnp.zeros_like(state_ref)

    half = RET_HEADS * RET_QK_DIM // 2
    cos = cos_ref[...]
    sin = sin_ref[...]

    def rope(x):
        x1, x2 = x[:, :half], x[:, half:]
        return jnp.concatenate([x1 * cos - x2 * sin, x1 * sin + x2 * cos], axis=1)

    q = rope(rq_ref[...])
    k = rope(rk_ref[...]) * RET_QK_DIM ** -0.5
    pos = i * BLOCK + lax.broadcasted_iota(jnp.int32, (BLOCK, 1), 0)
    k = jnp.where(pos >= PAD, k, 0.0)
    qb = q.astype(BF16)
    kb = k.astype(BF16)
    q_dec = (q * dq_ref[...]).astype(BF16)
    k_dec = (k * dk_ref[...]).astype(BF16)
    head_of_col = (lax.broadcasted_iota(jnp.int32, (1, 2 * half), 1) % half) // (RET_QK_DIM // 2)
    zero = jnp.zeros((), BF16)
    for h in range(RET_HEADS):
        mine = head_of_col == h
        vh = rv_ref[:, h * RET_V_DIM:(h + 1) * RET_V_DIM]
        s = (_dot_nt(jnp.where(mine, qb, zero), kb) * decay_ref[h]).astype(BF16)
        st = state_ref[h]
        o = _dot(s, vh) + _dot(jnp.where(mine, q_dec, zero), st.astype(BF16))
        state_ref[h] = cd_ref[h] * st + _dot_tn(k_dec, vh)
        mu = jnp.mean(o, axis=1, keepdims=True)
        var = jnp.mean(jnp.square(o - mu), axis=1, keepdims=True)
        sl = slice(h * RET_V_DIM, (h + 1) * RET_V_DIM)
        o = (o - mu) * lax.rsqrt(var + GN_EPS) * gn_ref[:, sl]
        g = rg_ref[:, sl]
        o_ref[:, sl] = (g * jax.nn.sigmoid(g) * o).astype(o_ref.dtype)


def _retention(misc3, pbf3, cos, sin, dq_tab, dk_tab, decay, cd, gn_w):
    b, t, _ = misc3.shape
    qk = RET_HEADS * RET_QK_DIM
    vw = RET_HEADS * RET_V_DIM
    const2 = lambda bi, i: (0, 0)
    const3 = lambda bi, i: (0, 0, 0)
    return pl.pallas_call(
        _ret_kernel,
        grid=(b, t // BLOCK),
        in_specs=[
            pl.BlockSpec((None, BLOCK, qk), lambda bi, i: (bi, i, 0)),
            pl.BlockSpec((None, BLOCK, qk), lambda bi, i: (bi, i, 1)),
            pl.BlockSpec((None, BLOCK, vw), lambda bi, i: (bi, i, 5)),
            pl.BlockSpec((None, BLOCK, vw), lambda bi, i: (bi, i, 1)),
            pl.BlockSpec((BLOCK, qk // 2), lambda bi, i: (i, 0)),
            pl.BlockSpec((BLOCK, qk // 2), lambda bi, i: (i, 0)),
            pl.BlockSpec((BLOCK, qk), const2),
            pl.BlockSpec((BLOCK, qk), const2),
            pl.BlockSpec((RET_HEADS, BLOCK, BLOCK), const3),
            pl.BlockSpec((RET_HEADS, 1, RET_V_DIM), const3),
            pl.BlockSpec((1, vw), const2),
        ],
        out_specs=pl.BlockSpec((None, BLOCK, vw), lambda bi, i: (bi, i, 0)),
        out_shape=jax.ShapeDtypeStruct((b, t, vw), BF16),
        scratch_shapes=[pltpu.VMEM((RET_HEADS, qk, RET_V_DIM), F32)],
        compiler_params=_params(("parallel", "arbitrary")),
    )(misc3, misc3, pbf3, misc3, cos, sin, dq_tab, dk_tab, decay, cd, gn_w.reshape(1, vw))


def _merge_kernel(x_ref, g_ref, b0_ref, b1_ref, b2_ref, wg0_ref, wg1_ref, wg2_ref, wb_ref, o_ref, xn_ref):
    @pl.when(pl.program_id(1) == 0)
    def _():
        xn_ref[...] = _rms(x_ref[...], g_ref[...]).astype(BF16)

    xn = xn_ref[...]
    acc = jax.nn.sigmoid(_dot(xn, wg0_ref[...])) * _dot(b0_ref[...], wb_ref[0])
    acc = acc + jax.nn.sigmoid(_dot(xn, wg1_ref[...])) * _dot(b1_ref[...], wb_ref[1])
    acc = acc + jax.nn.sigmoid(_dot(xn, wg2_ref[...])) * _dot(b2_ref[...], wb_ref[2])
    o_ref[...] = acc.astype(o_ref.dtype)


def _merge(x, g, b0, b1, b2, w_gates, w_branch):
    n, bw = b0.shape
    dm = x.shape[1]
    d = w_branch.shape[2]
    tm = _pick_tile(n, 640)
    tn = _pick_tile(d, 512, LANE)
    nj = d // tn
    bspec = pl.BlockSpec((tm, bw), lambda i, j: (i, 0))
    gspec = lambda r: pl.BlockSpec((dm, tn), lambda i, j: (0, r * nj + j))
    return pl.pallas_call(
        _merge_kernel,
        grid=(n // tm, nj),
        in_specs=[pl.BlockSpec((tm, dm), lambda i, j: (i, 0)), pl.BlockSpec((1, dm), lambda i, j: (0, 0)),
                  bspec, bspec, bspec, gspec(0), gspec(1), gspec(2),
                  pl.BlockSpec((N_BRANCH, bw, tn), lambda i, j: (0, 0, j))],
        out_specs=pl.BlockSpec((tm, tn), lambda i, j: (i, j)),
        out_shape=jax.ShapeDtypeStruct((n, d), BF16),
        scratch_shapes=[pltpu.VMEM((tm, dm), BF16)],
        compiler_params=_params(("parallel", "arbitrary")),
    )(x, g.reshape(1, dm), b0, b1, b2, w_gates, w_gates, w_gates, w_branch)


def _resproj_kernel(a_ref, w_ref, r_ref, o_ref):
    o_ref[...] = r_ref[...] + _dot(a_ref[...], w_ref[...])


def _resproj(a, w, res):
    n, k = a.shape
    d = w.shape[1]
    tm = _pick_tile(n, 640)
    tn = _pick_tile(d, 512, LANE)
    return pl.pallas_call(
        _resproj_kernel,
        grid=(n // tm, d // tn),
        in_specs=[
            pl.BlockSpec((tm, k), lambda i, j: (i, 0)),
            pl.BlockSpec((k, tn), lambda i, j: (0, j)),
            pl.BlockSpec((tm, tn), lambda i, j: (i, j)),
        ],
        out_specs=pl.BlockSpec((tm, tn), lambda i, j: (i, j)),
        out_shape=jax.ShapeDtypeStruct((n, d), F32),
        compiler_params=_params(("parallel", "arbitrary")),
    )(a, w, res)


def _t5_bucket(dist):
    d = jnp.maximum(dist, 1).astype(F32)
    large = T5_MAX_EXACT + (jnp.log(d / T5_MAX_EXACT) / math.log(T5_MAX_DIST / T5_MAX_EXACT)
                            * (T5_BUCKETS - T5_MAX_EXACT)).astype(jnp.int32)
    large = jnp.minimum(large, T5_BUCKETS - 1)
    return jnp.where(dist < T5_MAX_EXACT, dist, large)


def _t5_bias_tiles(t5_table):
    k = jnp.arange(BLOCK, dtype=jnp.int32)[:, None]
    q = jnp.arange(BLOCK, dtype=jnp.int32)[None, :]
    table = t5_table.astype(F32)
    rel = table - table[T5_BUCKETS - 1]

    def lookup(dist):
        onehot = (_t5_bucket(dist)[..., None] == jnp.arange(T5_BUCKETS, dtype=jnp.int32)).astype(F32)
        return jnp.einsum("kqb,bh->hkq", onehot, rel, precision=lax.Precision.HIGHEST)

    bdiag = lookup(jnp.maximum(q - k, 0))
    bprev = lookup(BLOCK + q - k)
    return jnp.concatenate([bprev, bdiag, jnp.zeros_like(bdiag)], axis=1) * LOG2E


def _rope_tables(t):
    half = RET_QK_DIM // 2
    inv = ROPE_BASE ** (-jnp.arange(half, dtype=F32) / half)
    ang = jnp.arange(t, dtype=jnp.int32).astype(F32)[:, None] * inv[None, :]
    return jnp.tile(jnp.cos(ang), (1, RET_HEADS)), jnp.tile(jnp.sin(ang), (1, RET_HEADS))


def _decay_tables():
    log_gamma = jnp.log1p(-(2.0 ** (-5.0 - jnp.arange(RET_HEADS, dtype=F32))))
    i = jnp.arange(BLOCK, dtype=F32)
    gap = i[:, None] - i[None, :]
    decay = jnp.where(gap[None] >= 0, jnp.exp(jnp.maximum(gap, 0.0)[None] * log_gamma[:, None, None]), 0.0)
    head_of_col = (np.arange(RET_HEADS * RET_QK_DIM) % (RET_HEADS * RET_QK_DIM // 2)) // (RET_QK_DIM // 2)
    lg_col = log_gamma[head_of_col]
    dk_tab = jnp.exp((BLOCK - 1 - i)[:, None] * lg_col[None, :])
    dq_tab = jnp.exp((i + 1.0)[:, None] * lg_col[None, :])
    cd = jnp.broadcast_to(jnp.exp(BLOCK * log_gamma)[:, None, None], (RET_HEADS, 1, RET_V_DIM))
    return decay.astype(F32), dq_tab, dk_tab, cd


def _split_w_in(fq32, w16):
    cols = lambda name: w16[:, IN_OFF[name]:IN_OFF[name] + IN_W[name]]
    zeros = lambda w: jnp.zeros((w16.shape[0], w), BF16)
    rope = lambda a: jnp.swapaxes(a.reshape(-1, RET_HEADS, 2, RET_QK_DIM // 2), 1, 2).reshape(a.shape)
    fq = (fq32 * (FOX_HEAD_DIM ** -0.5 * LOG2E)).astype(BF16)
    w_bf = jnp.concatenate([cols("dq"), cols("iq"), fq, cols("fk"), cols("fv"), cols("rv"),
                            cols("ik"), zeros(LANE - IDX_DIM)], axis=1)
    w_misc = jnp.concatenate([rope(cols("rq")), rope(cols("rk")), cols("rg"), cols("dc"),
                              cols("iw"), cols("ff"), zeros(LANE - IDX_HEADS - FOX_HEADS),
                              zeros(LANE)], axis=1)
    assert w_bf.shape[1] == BF_COLS and w_misc.shape[1] == MISC_COLS
    return w_bf, w_misc, cols("gates")


def _mixer(x, b, t, mix_norm, fq32, w_in16, kv_norm, w_uk, w_uv, f_bias, gn_w, w_branch16, w_out16, tables, topk):
    n = b * t
    bias_tiles, rope_tabs, decay_tabs = tables
    w_bf, w_misc, w_gates = _split_w_in(fq32, w_in16)
    pbf = _proj(x, mix_norm, w_bf, BF16)
    misc = _proj(x, mix_norm, w_misc, F32)
    pbf3 = pbf.reshape(b, t, BF_COLS)
    misc3 = misc.reshape(b, t, MISC_COLS)

    c3 = _latent(misc, kv_norm).reshape(b, t, DSA_LATENT)
    wuk, wuv = _dsa_weights(w_uk, w_uv)
    ones_rows = jnp.zeros((b, 8, t), BF16).at[:, 0].set(1)
    ct3 = jnp.concatenate([jnp.swapaxes(c3, 1, 2), ones_rows], axis=1)
    dsa_o = _dsa(pbf3, misc3, c3, ct3, wuk, wuv, bias_tiles, topk)

    cum = _cum_logf(misc3, f_bias)[:, :, SMALL_FF:SMALL_FF + FOX_HEADS]
    cum_h = jnp.swapaxes(cum, 1, 2)
    fox_o = _fox(pbf3, cum_h[:, :, None, :], cum_h[..., None])

    ret_o = _retention(misc3, pbf3, *rope_tabs, *decay_tabs, gn_w)

    merged = _merge(x, mix_norm, dsa_o.reshape(n, -1), fox_o.reshape(n, -1), ret_o.reshape(n, -1), w_gates,
                    w_branch16)
    return _resproj(merged, w_out16, x)


def kernel(x, meta_tokens, t5_table, ffn1_norm, ffn1_w1, ffn1_w3, ffn1_w2, mix_norm, w_in, dsa_kv_norm,
           dsa_w_uk, dsa_w_uv, fox_f_bias, ret_gn_w, w_branch, w_out, ffn2_norm, ffn2_w1, ffn2_w3, ffn2_w2,
           final_norm):
    b, seq, d = x.shape
    assert d == D_MODEL and seq % BLOCK == 0
    t = seq + BLOCK
    depth = ffn1_norm.shape[0]
    topk = min(INDEX_TOPK, seq // 4)
    h = jnp.concatenate([
        jnp.zeros((b, PAD, d), x.dtype),
        jnp.broadcast_to(meta_tokens.astype(x.dtype)[None], (b, N_META, d)),
        x,
    ], axis=1).reshape(b * t, d)
    cos, sin = _rope_tables(t)
    decay, dq_tab, dk_tab, cd = _decay_tables()
    tables = (_t5_bias_tiles(t5_table), (cos, sin), (dq_tab, dk_tab, decay, cd))
    bf = lambda w, l: w[l].astype(BF16)
    for l in range(depth):
        h = _ffn(h, ffn1_norm[l], bf(ffn1_w1, l), bf(ffn1_w3, l), bf(ffn1_w2, l))
        fq32 = w_in[l, :, IN_OFF["fq"]:IN_OFF["fq"] + IN_W["fq"]]
        h = _mixer(h, b, t, mix_norm[l], fq32, bf(w_in, l), dsa_kv_norm[l], dsa_w_uk[l], dsa_w_uv[l],
                   fox_f_bias[l], ret_gn_w[l], bf(w_branch, l), bf(w_out, l), tables, topk)
        h = _ffn(h, ffn2_norm[l], bf(ffn2_w1, l), bf(ffn2_w3, l), bf(ffn2_w2, l),
                 final_g=final_norm if l == depth - 1 else None)
    return h.reshape(b, t, d)[:, BLOCK:]
```

```python
import functools
import math

import numpy as np
import jax
import jax.numpy as jnp
from jax import lax
from jax.experimental import pallas as pl
from jax.experimental.pallas import tpu as pltpu

F32 = jnp.float32
BF16 = jnp.bfloat16

LANE = 128
VMEM_LIMIT = 56 * 1024 * 1024

D_MODEL = 2048
N_META = 16
BLOCK = 128
PAD = BLOCK - N_META
RMS_EPS = 1e-6
GN_EPS = 1e-5
NEG = -1e30

DSA_HEADS = 8
DSA_HEAD_DIM = 128
DSA_LATENT = 256
IDX_HEADS = 16
IDX_DIM = 64
INDEX_TOPK = 256
FOX_HEADS = 8
FOX_HEAD_DIM = 128
RET_HEADS = 8
RET_QK_DIM = 64
RET_V_DIM = 128
ROPE_BASE = 10000.0
BRANCH_WIDTH = 1024
N_BRANCH = 3
T5_BUCKETS = 32
T5_MAX_EXACT = 16
T5_MAX_DIST = 128

IN_WIDTHS = (1024, 256, 1024, 64, 16, 1024, 1024, 1024, 8, 512, 512, 1024, 1024, 6144)
IN_NAMES = ("dq", "dc", "iq", "ik", "iw", "fq", "fk", "fv", "ff", "rq", "rk", "rv", "rg", "gates")
IN_OFF = dict(zip(IN_NAMES, np.cumsum((0,) + IN_WIDTHS[:-1]).tolist()))
IN_W = dict(zip(IN_NAMES, IN_WIDTHS))

INT_MIN = -2 ** 31
LOG2E = math.log2(math.e)
SCORE_GROUP = 4

BF_COLS = 6 * 1024 + LANE
MISC_COLS = 2560
SMALL_IW = 0
SMALL_FF = IDX_HEADS


def _dot(a, b):
    return jnp.dot(a, b, preferred_element_type=F32)


def _dot_nt(a, b):
    return lax.dot_general(a, b, (((1,), (1,)), ((), ())), preferred_element_type=F32)


def _dot_tn(a, b):
    return lax.dot_general(a, b, (((0,), (0,)), ((), ())), preferred_element_type=F32)


def _pick_tile(n, target, mult=8):
    best = None
    for t in range(mult, min(n, target) + 1, mult):
        if n % t == 0:
            best = t
    assert best is not None, (n, target, mult)
    return best


def _params(sem):
    return pltpu.CompilerParams(dimension_semantics=sem, vmem_limit_bytes=VMEM_LIMIT)


def _rms(x, g):
    return x * lax.rsqrt(jnp.mean(x * x, axis=-1, keepdims=True) + RMS_EPS) * g


def _ffn_kernel(*refs, nj, final):
    if final:
        x_ref, g_ref, w1_ref, w3_ref, w2_ref, fg_ref, o_ref, xn_ref, acc_ref = refs
    else:
        x_ref, g_ref, w1_ref, w3_ref, w2_ref, o_ref, xn_ref, acc_ref = refs
    j = pl.program_id(1)

    @pl.when(j == 0)
    def _():
        xn_ref[...] = _rms(x_ref[...], g_ref[...]).astype(BF16)
        acc_ref[...] = jnp.zeros_like(acc_ref)

    xn = xn_ref[...]
    h1 = _dot(xn, w1_ref[...])
    h3 = _dot(xn, w3_ref[...])
    g = (h1 * jax.nn.sigmoid(h1) * h3).astype(BF16)
    acc_ref[...] += _dot(g, w2_ref[...])

    @pl.when(j == nj - 1)
    def _():
        y = x_ref[...] + 0.5 * acc_ref[...]
        if final:
            y = _rms(y, fg_ref[...])
        o_ref[...] = y


def _ffn(x, g, w1, w3, w2, final_g=None):
    n, d = x.shape
    f = w1.shape[1]
    tm = _pick_tile(n, 640)
    tf = _pick_tile(f, 512, LANE)
    nj = f // tf
    final = final_g is not None
    in_specs = [
        pl.BlockSpec((tm, d), lambda i, j: (i, 0)),
        pl.BlockSpec((1, d), lambda i, j: (0, 0)),
        pl.BlockSpec((d, tf), lambda i, j: (0, j)),
        pl.BlockSpec((d, tf), lambda i, j: (0, j)),
        pl.BlockSpec((tf, d), lambda i, j: (j, 0)),
    ]
    args = [x, g.reshape(1, d), w1, w3, w2]
    if final:
        in_specs.append(pl.BlockSpec((1, d), lambda i, j: (0, 0)))
        args.append(final_g.reshape(1, d))
    return pl.pallas_call(
        functools.partial(_ffn_kernel, nj=nj, final=final),
        grid=(n // tm, nj),
        in_specs=in_specs,
        out_specs=pl.BlockSpec((tm, d), lambda i, j: (i, 0)),
        out_shape=jax.ShapeDtypeStruct((n, d), F32),
        scratch_shapes=[pltpu.VMEM((tm, d), BF16), pltpu.VMEM((tm, d), F32)],
        compiler_params=_params(("parallel", "arbitrary")),
    )(*args)


def _proj_kernel(x_ref, g_ref, wa_ref, sa_ref, wb_ref, kvn_ref, oa_ref, ob_ref, oc_ref, xn_ref, *, na, dc_tile):
    j = pl.program_id(1)

    @pl.when(j == 0)
    def _():
        xn_ref[...] = _rms(x_ref[...], g_ref[...]).astype(BF16)

    @pl.when(j < na)
    def _():
        oa_ref[...] = (_dot(xn_ref[...], wa_ref[...]) * sa_ref[...]).astype(oa_ref.dtype)

    @pl.when(j >= na)
    def _():
        r = _dot(xn_ref[...], wb_ref[...])
        ob_ref[...] = r

        @pl.when(j == na + dc_tile)
        def _():
            oc_ref[...] = _rms(r[:, 0:DSA_LATENT], kvn_ref[...]).astype(oc_ref.dtype)


def _proj(x, g, w_bf, bf_scale, w_misc, kv_norm):
    n, d = x.shape
    ca, cb = w_bf.shape[1], w_misc.shape[1]
    tm = _pick_tile(n, 640)
    ta = _pick_tile(ca, 1024, LANE)
    tb = _pick_tile(cb, 512, LANE)
    na, nb = ca // ta, cb // tb
    dc_col = IN_W["rq"] + IN_W["rk"] + IN_W["rg"]
    assert dc_col % tb == 0 and tb >= DSA_LATENT
    a_idx = lambda j: jnp.minimum(j, na - 1)
    b_idx = lambda j: jnp.maximum(j - na, 0)
    return pl.pallas_call(
        functools.partial(_proj_kernel, na=na, dc_tile=dc_col // tb),
        grid=(n // tm, na + nb),
        in_specs=[
            pl.BlockSpec((tm, d), lambda i, j: (i, 0)),
            pl.BlockSpec((1, d), lambda i, j: (0, 0)),
            pl.BlockSpec((d, ta), lambda i, j: (0, a_idx(j))),
            pl.BlockSpec((1, ta), lambda i, j: (0, a_idx(j))),
            pl.BlockSpec((d, tb), lambda i, j: (0, b_idx(j))),
            pl.BlockSpec((1, DSA_LATENT), lambda i, j: (0, 0)),
        ],
        out_specs=[
            pl.BlockSpec((tm, ta), lambda i, j: (i, a_idx(j))),
            pl.BlockSpec((tm, tb), lambda i, j: (i, b_idx(j))),
            pl.BlockSpec((tm, DSA_LATENT), lambda i, j: (i, 0)),
        ],
        out_shape=[jax.ShapeDtypeStruct((n, ca), BF16), jax.ShapeDtypeStruct((n, cb), F32),
                   jax.ShapeDtypeStruct((n, DSA_LATENT), BF16)],
        scratch_shapes=[pltpu.VMEM((tm, d), BF16)],
        compiler_params=_params(("parallel", "arbitrary")),
    )(x, g.reshape(1, d), w_bf, bf_scale.reshape(1, ca), w_misc, kv_norm.reshape(1, DSA_LATENT))


def _split3(x):
    hi = x.astype(BF16)
    r = x - hi.astype(F32)
    mid = r.astype(BF16)
    lo = (r - mid.astype(F32)).astype(BF16)
    return hi, mid, lo


def _cum_kernel(s_ref, b_ref, o_ref, carry_ref):
    @pl.when(pl.program_id(1) == 0)
    def _():
        carry_ref[...] = jnp.zeros_like(carry_ref)

    z = s_ref[...] + b_ref[...]
    lf = jnp.minimum(z, 0.0) - jnp.log1p(jnp.exp(-jnp.abs(z)))
    r = lax.broadcasted_iota(jnp.int32, (BLOCK, BLOCK), 0)
    c = lax.broadcasted_iota(jnp.int32, (BLOCK, BLOCK), 1)
    tri = jnp.where(c <= r, 1.0, 0.0).astype(BF16)
    hi, mid, lo = _split3(lf)
    cum = (_dot(tri, lo) + _dot(tri, mid)) + _dot(tri, hi) + carry_ref[...]
    o_ref[...] = cum
    carry_ref[...] = cum[BLOCK - 1:BLOCK, :]


def _cum_logf(misc3, f_bias):
    b, t, _ = misc3.shape
    blk = (IN_W["rq"] + IN_W["rk"] + IN_W["rg"] + IN_W["dc"]) // LANE
    bias = jnp.zeros((1, LANE), F32).at[0, SMALL_FF:SMALL_FF + FOX_HEADS].set(f_bias)
    return pl.pallas_call(
        _cum_kernel,
        grid=(b, t // BLOCK),
        in_specs=[
            pl.BlockSpec((None, BLOCK, LANE), lambda bi, i: (bi, i, blk)),
            pl.BlockSpec((1, LANE), lambda bi, i: (0, 0)),
        ],
        out_specs=pl.BlockSpec((None, BLOCK, LANE), lambda bi, i: (bi, i, 0)),
        out_shape=jax.ShapeDtypeStruct((b, t, LANE), F32),
        scratch_shapes=[pltpu.VMEM((1, LANE), F32)],
        compiler_params=_params(("parallel", "arbitrary")),
    )(misc3, bias)


def _bit_transpose32(words):
    a = list(words)
    j, m = 16, 0x0000FFFF
    while j:
        mask = jnp.int32(m if m < 2 ** 31 else m - 2 ** 32)
        for k in range(32):
            if not k & j:
                t = (lax.shift_right_logical(a[k], jnp.int32(j)) ^ a[k + j]) & mask
                a[k + j] = a[k + j] ^ t
                a[k] = a[k] ^ (t << j)
        j >>= 1
        m = (m ^ (m << j)) & 0xFFFFFFFF
    return a


FAR_BLOCKS = 4


def _dsa_kernel(dq_ref, iq_ref, sm_ref, ik_ref, c_ref, ct_ref, wuk_ref, wuv_ref, bias_ref,
                o_ref, s_ref, qlat_ref, p_ref, m_ref, acc_ref, lg0_ref, lg1_ref, lgn_ref, iqt_ref, wt_ref,
                planes_ref, alive_ref, p1_ref, *, topk, nblk):
    i = pl.program_id(1)
    key_i = lax.broadcasted_iota(jnp.int32, (BLOCK, BLOCK), 0)
    qry_i = lax.broadcasted_iota(jnp.int32, (BLOCK, BLOCK), 1)
    hs = lambda h: slice(h * BLOCK, (h + 1) * BLOCK)

    iq_t = jnp.transpose(iq_ref[...].astype(F32)).astype(BF16)
    for h in range(IDX_HEADS):
        iqt_ref[0:IDX_DIM, hs(h)] = iq_t[h * IDX_DIM:(h + 1) * IDX_DIM, :]
    iqt_ref[IDX_DIM:LANE, :] = jnp.zeros((LANE - IDX_DIM, IDX_HEADS * BLOCK), BF16)
    wt_ref[...] = jnp.transpose(sm_ref[...])[SMALL_IW:SMALL_IW + IDX_HEADS, :] * (IDX_HEADS ** -0.5 * IDX_DIM ** -0.5)

    for h in range(DSA_HEADS):
        ql = _dot_nt(wuk_ref[h], dq_ref[:, h * DSA_HEAD_DIM:(h + 1) * DSA_HEAD_DIM]) * (DSA_HEAD_DIM ** -0.5 * LOG2E)
        qlat_ref[:, hs(h)] = ql.astype(BF16)

    ngrp = (i + SCORE_GROUP) // SCORE_GROUP
    npair = (ngrp + 1) // 2
    for u in range(1, 2 * SCORE_GROUP):
        s_ref[i + u] = jnp.full((BLOCK, BLOCK), INT_MIN, jnp.int32)

    def score_group(g, _):
        j0 = jnp.minimum(g * SCORE_GROUP, nblk - SCORE_GROUP)
        kt = ik_ref[pl.ds(pl.multiple_of(j0 * BLOCK, BLOCK), SCORE_GROUP * BLOCK), :]
        r = _dot(kt, iqt_ref[...])
        t = i * BLOCK + qry_i
        for u in range(SCORE_GROUP):
            acc = jnp.zeros((BLOCK, BLOCK), F32)
            for h in range(IDX_HEADS):
                acc = acc + jnp.maximum(r[u * BLOCK:(u + 1) * BLOCK, hs(h)], 0.0) * wt_ref[h:h + 1, :]
            s = (j0 + u) * BLOCK + key_i
            ok = (s <= t) & ((s >= PAD) | (s == t))
            bits = lax.bitcast_convert_type(acc, jnp.int32)
            bits = jnp.where(bits == INT_MIN, 0, bits)
            key = bits ^ ((bits >> 31) & 0x7FFFFFFF)
            s_ref[j0 + u] = jnp.where(ok, key, INT_MIN)
        return 0

    lax.fori_loop(0, ngrp, score_group, 0)

    def slice_group(g, _):
        for half in range(2):
            words = []
            for u in (2 * half, 2 * half + 1):
                tile = s_ref[g * SCORE_GROUP + u] ^ INT_MIN
                words += [tile[r * 8:(r + 1) * 8, :] for r in range(BLOCK // 8)]
            planes = _bit_transpose32(words)
            for bit in range(32):
                planes_ref[2 * g + half, bit] = planes[bit]
            planes_ref[2 * g + half, 32] = jnp.full((8, BLOCK), -1, jnp.int32)
        return 0

    lax.fori_loop(0, 2 * npair, slice_group, 0)
    alive_ref[...] = jnp.full(alive_ref.shape, -1, jnp.int32)

    def radix_pass(t, carry):
        thr, need, drop_prev = carry
        bit = 31 - t

        def body(pr, accs):
            accs = list(accs)
            for u in range(4):
                st = 4 * pr + u
                alive = alive_ref[st] & (planes_ref[st, bit + 1] ^ drop_prev)
                alive_ref[st] = alive
                accs[u] = accs[u] + lax.population_count(alive & planes_ref[st, bit])
            return tuple(accs)

        zero = jnp.zeros((8, BLOCK), jnp.int32)
        accs = lax.fori_loop(0, npair, body, (zero,) * 4)
        cnt = ((accs[0] + accs[1]) + (accs[2] + accs[3])).astype(F32)
        cnt = jnp.sum(cnt, axis=0, keepdims=True).astype(jnp.int32)
        take = cnt >= need
        thr = thr | jnp.where(take, jnp.int32(1) << bit, 0)
        need = jnp.where(take, need, need - cnt)
        return thr, need, jnp.where(take, 0, -1)

    zero = jnp.zeros((1, BLOCK), jnp.int32)
    thr, _, _ = lax.fori_loop(0, 32, radix_pass, (zero, zero + topk, zero))
    thr = jnp.maximum(thr ^ INT_MIN, INT_MIN + 1)

    m_ref[...] = jnp.full(m_ref.shape, NEG, F32)
    acc_ref[...] = jnp.zeros_like(acc_ref)

    def logits(j0, nb, dst_ref):
        w = nb * BLOCK
        dst_ref[0:w, :] = _dot(c_ref[pl.ds(pl.multiple_of(j0 * BLOCK, BLOCK), w), :], qlat_ref[...])

    def softmax_pv(j0, nb, lg_ref, pb_ref, bias_off=None):
        w = nb * BLOCK
        ckt = ct_ref[:, pl.ds(pl.multiple_of(j0 * BLOCK, BLOCK), w)]
        masked = jnp.where(s_ref[pl.ds(j0, nb)].reshape(w, BLOCK) >= thr, 0.0, NEG)
        alphas = []
        for h in range(DSA_HEADS):
            x = lg_ref[0:w, hs(h)] + masked
            if bias_off is not None:
                x = x + bias_ref[h, pl.ds(bias_off, w), :]
            m_prev = m_ref[h:h + 1, :]
            m_new = jnp.maximum(m_prev, jnp.max(x, axis=0, keepdims=True))
            alphas.append(jnp.exp2(m_prev - m_new))
            m_ref[h:h + 1, :] = m_new
            pb_ref[0:w, hs(h)] = jnp.exp2(x - m_new).astype(BF16)
        pv = _dot(ckt, pb_ref[0:w, :])
        for h in range(DSA_HEADS):
            acc_ref[:, hs(h)] = acc_ref[:, hs(h)] * alphas[h] + pv[:, hs(h)]

    near0 = jnp.maximum(i - 1, 0)
    logits(near0, 2, lgn_ref)

    n_far = jnp.maximum(i - 1, 0)
    n_big = n_far // FAR_BLOCKS
    big = lambda g: g * FAR_BLOCKS

    @pl.when(n_big > 0)
    def _():
        logits(0, FAR_BLOCKS, lg0_ref)

        def pair_of_tiles(g, _):
            logits(big(2 * g + 1), FAR_BLOCKS, lg1_ref)
            softmax_pv(big(2 * g), FAR_BLOCKS, lg0_ref, p_ref)
            logits(big(jnp.minimum(2 * g + 2, n_big - 1)), FAR_BLOCKS, lg0_ref)
            softmax_pv(big(2 * g + 1), FAR_BLOCKS, lg1_ref, p1_ref)
            return 0

        lax.fori_loop(0, n_big // 2, pair_of_tiles, 0)

        @pl.when(n_big % 2 == 1)
        def _():
            softmax_pv(big(n_big - 1), FAR_BLOCKS, lg0_ref, p_ref)

    def far_one(j, _):
        logits(j, 1, lg0_ref)
        softmax_pv(j, 1, lg0_ref, p_ref)
        return 0

    lax.fori_loop(n_big * FAR_BLOCKS, n_far, far_one, 0)
    softmax_pv(near0, 2, lgn_ref, p1_ref, bias_off=pl.multiple_of(jnp.where(i >= 1, 0, BLOCK), BLOCK))

    for h in range(DSA_HEADS):
        inv_l = 1.0 / acc_ref[DSA_LATENT:DSA_LATENT + 1, hs(h)]
        o_lat = (acc_ref[0:DSA_LATENT, hs(h)] * inv_l).astype(BF16)
        o = _dot_tn(o_lat, wuv_ref[h])
        o_ref[:, h * DSA_HEAD_DIM:(h + 1) * DSA_HEAD_DIM] = o.astype(o_ref.dtype)


def _dsa(pbf3, misc3, c3, ct3, wuk, wuv, bias, topk):
    b, t, _ = pbf3.shape
    nblk = t // BLOCK
    hq = DSA_HEADS * DSA_HEAD_DIM
    ik_blk = 6 * 1024 // LANE
    small_blk = (IN_W["rq"] + IN_W["rk"] + IN_W["rg"] + IN_W["dc"]) // LANE
    n_sets = 4 * (((nblk + SCORE_GROUP - 1) // SCORE_GROUP + 1) // 2)
    n_tiles = nblk + 2 * SCORE_GROUP - 1
    const3 = lambda bi, i: (0, 0, 0)
    return pl.pallas_call(
        functools.partial(_dsa_kernel, topk=topk, nblk=nblk),
        grid=(b, nblk),
        in_specs=[
            pl.BlockSpec((None, BLOCK, hq), lambda bi, i: (bi, i, 0)),
            pl.BlockSpec((None, BLOCK, IDX_HEADS * IDX_DIM), lambda bi, i: (bi, i, 1)),
            pl.BlockSpec((None, BLOCK, LANE), lambda bi, i: (bi, i, small_blk)),
            pl.BlockSpec((None, t, LANE), lambda bi, i: (bi, 0, ik_blk)),
            pl.BlockSpec((None, t, DSA_LATENT), lambda bi, i: (bi, 0, 0)),
            pl.BlockSpec((None, DSA_LATENT + 8, t), lambda bi, i: (bi, 0, 0)),
            pl.BlockSpec((DSA_HEADS, DSA_LATENT, DSA_HEAD_DIM), const3),
            pl.BlockSpec((DSA_HEADS, DSA_LATENT, DSA_HEAD_DIM), const3),
            pl.BlockSpec((DSA_HEADS, 3 * BLOCK, BLOCK), const3),
        ],
        out_specs=pl.BlockSpec((None, BLOCK, hq), lambda bi, i: (bi, i, 0)),
        out_shape=jax.ShapeDtypeStruct((b, t, hq), BF16),
        scratch_shapes=[
            pltpu.VMEM((n_tiles, BLOCK, BLOCK), jnp.int32),
            pltpu.VMEM((DSA_LATENT, DSA_HEADS * BLOCK), BF16),
            pltpu.VMEM((FAR_BLOCKS * BLOCK, DSA_HEADS * BLOCK), BF16),
            pltpu.VMEM((DSA_HEADS, BLOCK), F32),
            pltpu.VMEM((DSA_LATENT + 8, DSA_HEADS * BLOCK), F32),
            pltpu.VMEM((FAR_BLOCKS * BLOCK, DSA_HEADS * BLOCK), F32),
            pltpu.VMEM((FAR_BLOCKS * BLOCK, DSA_HEADS * BLOCK), F32),
            pltpu.VMEM((2 * BLOCK, DSA_HEADS * BLOCK), F32),
            pltpu.VMEM((LANE, IDX_HEADS * BLOCK), BF16),
            pltpu.VMEM((IDX_HEADS, BLOCK), F32),
            pltpu.VMEM((n_sets, 33, 8, BLOCK), jnp.int32),
            pltpu.VMEM((n_sets, 8, BLOCK), jnp.int32),
            pltpu.VMEM((FAR_BLOCKS * BLOCK, DSA_HEADS * BLOCK), BF16),
        ],
        compiler_params=_params(("parallel", "arbitrary")),
    )(pbf3, pbf3, misc3, pbf3, c3, ct3, wuk, wuv, bias)


def _dsa_weights(w_uk, w_uv):
    return jnp.transpose(w_uk, (1, 0, 2)).astype(BF16), jnp.transpose(w_uv, (1, 0, 2)).astype(BF16)


def _fox_kernel(q_ref, k_ref, v_ref, cq_ref, ck_ref, o_ref, m_ref, l_ref, acc_ref, lg_ref, *, tq):
    i = pl.program_id(2)
    cq = cq_ref[...] * LOG2E
    m_ref[...] = jnp.full(m_ref.shape, NEG, F32)
    l_ref[...] = jnp.zeros_like(l_ref)
    acc_ref[...] = jnp.zeros_like(acc_ref)

    def qk(j, slot):
        lg_ref[slot] = _dot_nt(k_ref[pl.ds(pl.multiple_of(j * tq, tq), tq), :], q_ref[...])

    def softmax_pv(j, slot, mode):
        off = pl.multiple_of(j * tq, tq)
        ck = ck_ref[pl.ds(off, tq), :] * LOG2E
        x = (lg_ref[slot] + cq) - ck
        if mode == "pad":
            s = lax.broadcasted_iota(jnp.int32, (tq, 1), 0)
            x = jnp.where(s >= PAD, x, NEG)
        elif mode == "diag":
            s = lax.broadcasted_iota(jnp.int32, (tq, tq), 0)
            t = lax.broadcasted_iota(jnp.int32, (tq, tq), 1)
            x = jnp.where((s <= t) & ((s + j * tq >= PAD) | (s == t)), x, NEG)
        m_prev = m_ref[...]
        m_new = jnp.maximum(m_prev, jnp.max(x, axis=0, keepdims=True))
        alpha = jnp.exp2(m_prev - m_new)
        p = jnp.exp2(x - m_new)
        l_ref[...] = alpha * l_ref[...] + jnp.sum(p, axis=0, keepdims=True)
        acc_ref[...] = acc_ref[...] * alpha + _dot_tn(v_ref[pl.ds(off, tq), :], p.astype(BF16))
        m_ref[...] = m_new

    qk(i, 2)

    @pl.when(i > 0)
    def _():
        qk(0, 0)
        qk(jnp.minimum(1, i), 1)
        softmax_pv(0, 0, "pad")
        n_mid = i - 1

        def pair(g, _):
            a = 2 * g + 1
            qk(jnp.minimum(a + 1, i), 0)
            softmax_pv(a, 1, "none")
            qk(jnp.minimum(a + 2, i), 1)
            softmax_pv(a + 1, 0, "none")
            return 0

        lax.fori_loop(0, n_mid // 2, pair, 0)

        @pl.when(n_mid % 2 == 1)
        def _():
            softmax_pv(i - 1, 1, "none")

    softmax_pv(i, 2, "diag")

    o_ref[...] = jnp.transpose(acc_ref[...] * (1.0 / l_ref[...])).astype(o_ref.dtype)


def _fox(pbf3, cq, ck):
    b, t, _ = pbf3.shape
    nblk = t // BLOCK
    tq = BLOCK * max(k for k in (1, 5) if nblk % k == 0)
    d = FOX_HEAD_DIM
    qb, kb, vb = 2 * 1024 // d, 3 * 1024 // d, 4 * 1024 // d
    return pl.pallas_call(
        functools.partial(_fox_kernel, tq=tq),
        grid=(b, FOX_HEADS, t // tq),
        in_specs=[
            pl.BlockSpec((None, tq, d), lambda bi, h, i: (bi, i, qb + h)),
            pl.BlockSpec((None, t, d), lambda bi, h, i: (bi, 0, kb + h)),
            pl.BlockSpec((None, t, d), lambda bi, h, i: (bi, 0, vb + h)),
            pl.BlockSpec((None, None, 1, tq), lambda bi, h, i: (bi, h, 0, i)),
            pl.BlockSpec((None, None, t, 1), lambda bi, h, i: (bi, h, 0, 0)),
        ],
        out_specs=pl.BlockSpec((None, tq, d), lambda bi, h, i: (bi, i, h)),
        out_shape=jax.ShapeDtypeStruct((b, t, FOX_HEADS * d), BF16),
        scratch_shapes=[pltpu.VMEM((1, tq), F32), pltpu.VMEM((1, tq), F32), pltpu.VMEM((d, tq), F32),
                        pltpu.VMEM((3, tq, tq), F32)],
        compiler_params=_params(("parallel", "parallel", "arbitrary")),
    )(pbf3, pbf3, pbf3, cq, ck)


def _ret_kernel(rq_ref, rk_ref, rv_ref, rg_ref, cos_ref, sin_ref, dq_ref, dk_ref, decay_ref, cd_ref, gn_ref,
                o_ref, state_ref):
    i = pl.program_id(1)

    @pl.when(i == 0)
    def _():
        state_ref[...] = jnp.zeros_like(state_ref)

    half = RET_HEADS * RET_QK_DIM // 2
    cos = cos_ref[...]
    sin = sin_ref[...]

    def rope(x):
        x1, x2 = x[:, :half], x[:, half:]
        return jnp.concatenate([x1 * cos - x2 * sin, x1 * sin + x2 * cos], axis=1)

    q = rope(rq_ref[...])
    k = rope(rk_ref[...]) * RET_QK_DIM ** -0.5
    pos = i * BLOCK + lax.broadcasted_iota(jnp.int32, (BLOCK, 1), 0)
    k = jnp.where(pos >= PAD, k, 0.0)
    qb = q.astype(BF16)
    kb = k.astype(BF16)
    q_dec = (q * dq_ref[...]).astype(BF16)
    k_dec = (k * dk_ref[...]).astype(BF16)
    head_of_col = (lax.broadcasted_iota(jnp.int32, (1, 2 * half), 1) % half) // (RET_QK_DIM // 2)
    zero = jnp.zeros((), BF16)
    for h in range(RET_HEADS):
        mine = head_of_col == h
        vh = rv_ref[:, h * RET_V_DIM:(h + 1) * RET_V_DIM]
        s = (_dot_nt(jnp.where(mine, qb, zero), kb) * decay_ref[h]).astype(BF16)
        st = state_ref[h]
        o = _dot(s, vh) + _dot(jnp.where(mine, q_dec, zero), st.astype(BF16))
        state_ref[h] = cd_ref[h] * st + _dot_tn(k_dec, vh)
        mu = jnp.mean(o, axis=1, keepdims=True)
        var = jnp.mean(jnp.square(o - mu), axis=1, keepdims=True)
        sl = slice(h * RET_V_DIM, (h + 1) * RET_V_DIM)
        o = (o - mu) * lax.rsqrt(var + GN_EPS) * gn_ref[:, sl]
        g = rg_ref[:, sl]
        o_ref[:, sl] = (g * jax.nn.sigmoid(g) * o).astype(o_ref.dtype)


def _retention(misc3, pbf3, cos, sin, dq_tab, dk_tab, decay, cd, gn_w):
    b, t, _ = misc3.shape
    qk = RET_HEADS * RET_QK_DIM
    vw = RET_HEADS * RET_V_DIM
    const2 = lambda bi, i: (0, 0)
    const3 = lambda bi, i: (0, 0, 0)
    return pl.pallas_call(
        _ret_kernel,
        grid=(b, t // BLOCK),
        in_specs=[
            pl.BlockSpec((None, BLOCK, qk), lambda bi, i: (bi, i, 0)),
            pl.BlockSpec((None, BLOCK, qk), lambda bi, i: (bi, i, 1)),
            pl.BlockSpec((None, BLOCK, vw), lambda bi, i: (bi, i, 5)),
            pl.BlockSpec((None, BLOCK, vw), lambda bi, i: (bi, i, 1)),
            pl.BlockSpec((BLOCK, qk // 2), lambda bi, i: (i, 0)),
            pl.BlockSpec((BLOCK, qk // 2), lambda bi, i: (i, 0)),
            pl.BlockSpec((BLOCK, qk), const2),
            pl.BlockSpec((BLOCK, qk), const2),
            pl.BlockSpec((RET_HEADS, BLOCK, BLOCK), const3),
            pl.BlockSpec((RET_HEADS, 1, RET_V_DIM), const3),
            pl.BlockSpec((1, vw), const2),
        ],
        out_specs=pl.BlockSpec((None, BLOCK, vw), lambda bi, i: (bi, i, 0)),
        out_shape=jax.ShapeDtypeStruct((b, t, vw), BF16),
        scratch_shapes=[pltpu.VMEM((RET_HEADS, qk, RET_V_DIM), F32)],
        compiler_params=_params(("parallel", "arbitrary")),
    )(misc3, misc3, pbf3, misc3, cos, sin, dq_tab, dk_tab, decay, cd, gn_w.reshape(1, vw))


def _merge_kernel(x_ref, g_ref, b0_ref, b1_ref, b2_ref, wg0_ref, wg1_ref, wg2_ref, wb_ref, o_ref, xn_ref):
    @pl.when(pl.program_id(1) == 0)
    def _():
        xn_ref[...] = _rms(x_ref[...], g_ref[...]).astype(BF16)

    xn = xn_ref[...]
    acc = jax.nn.sigmoid(_dot(xn, wg0_ref[...])) * _dot(b0_ref[...], wb_ref[0])
    acc = acc + jax.nn.sigmoid(_dot(xn, wg1_ref[...])) * _dot(b1_ref[...], wb_ref[1])
    acc = acc + jax.nn.sigmoid(_dot(xn, wg2_ref[...])) * _dot(b2_ref[...], wb_ref[2])
    o_ref[...] = acc.astype(o_ref.dtype)


def _merge(x, g, b0, b1, b2, w_gates, w_branch):
    n, bw = b0.shape
    dm = x.shape[1]
    d = w_branch.shape[2]
    tm = _pick_tile(n, 640)
    tn = _pick_tile(d, 512, LANE)
    nj = d // tn
    bspec = pl.BlockSpec((tm, bw), lambda i, j: (i, 0))
    gspec = lambda r: pl.BlockSpec((dm, tn), lambda i, j: (0, r * nj + j))
    return pl.pallas_call(
        _merge_kernel,
        grid=(n // tm, nj),
        in_specs=[pl.BlockSpec((tm, dm), lambda i, j: (i, 0)), pl.BlockSpec((1, dm), lambda i, j: (0, 0)),
                  bspec, bspec, bspec, gspec(0), gspec(1), gspec(2),
                  pl.BlockSpec((N_BRANCH, bw, tn), lambda i, j: (0, 0, j))],
        out_specs=pl.BlockSpec((tm, tn), lambda i, j: (i, j)),
        out_shape=jax.ShapeDtypeStruct((n, d), BF16),
        scratch_shapes=[pltpu.VMEM((tm, dm), BF16)],
        compiler_params=_params(("parallel", "arbitrary")),
    )(x, g.reshape(1, dm), b0, b1, b2, w_gates, w_gates, w_gates, w_branch)


def _resproj_kernel(a_ref, w_ref, r_ref, o_ref):
    o_ref[...] = r_ref[...] + _dot(a_ref[...], w_ref[...])


def _resproj(a, w, res):
    n, k = a.shape
    d = w.shape[1]
    tm = _pick_tile(n, 640)
    tn = _pick_tile(d, 512, LANE)
    return pl.pallas_call(
        _resproj_kernel,
        grid=(n // tm, d // tn),
        in_specs=[
            pl.BlockSpec((tm, k), lambda i, j: (i, 0)),
            pl.BlockSpec((k, tn), lambda i, j: (0, j)),
            pl.BlockSpec((tm, tn), lambda i, j: (i, j)),
        ],
        out_specs=pl.BlockSpec((tm, tn), lambda i, j: (i, j)),
        out_shape=jax.ShapeDtypeStruct((n, d), F32),
        compiler_params=_params(("parallel", "arbitrary")),
    )(a, w, res)


def _t5_bucket(dist):
    d = jnp.maximum(dist, 1).astype(F32)
    large = T5_MAX_EXACT + (jnp.log(d / T5_MAX_EXACT) / math.log(T5_MAX_DIST / T5_MAX_EXACT)
                            * (T5_BUCKETS - T5_MAX_EXACT)).astype(jnp.int32)
    large = jnp.minimum(large, T5_BUCKETS - 1)
    return jnp.where(dist < T5_MAX_EXACT, dist, large)


def _t5_bias_tiles(t5_table):
    k = jnp.arange(BLOCK, dtype=jnp.int32)[:, None]
    q = jnp.arange(BLOCK, dtype=jnp.int32)[None, :]
    table = t5_table.astype(F32)
    rel = table - table[T5_BUCKETS - 1]

    def lookup(dist):
        onehot = (_t5_bucket(dist)[..., None] == jnp.arange(T5_BUCKETS, dtype=jnp.int32)).astype(F32)
        return jnp.einsum("kqb,bh->hkq", onehot, rel, precision=lax.Precision.HIGHEST)

    bdiag = lookup(jnp.maximum(q - k, 0))
    bprev = lookup(BLOCK + q - k)
    return jnp.concatenate([bprev, bdiag, jnp.zeros_like(bdiag)], axis=1) * LOG2E


def _rope_tables(t):
    half = RET_QK_DIM // 2
    inv = ROPE_BASE ** (-jnp.arange(half, dtype=F32) / half)
    ang = jnp.arange(t, dtype=jnp.int32).astype(F32)[:, None] * inv[None, :]
    return jnp.tile(jnp.cos(ang), (1, RET_HEADS)), jnp.tile(jnp.sin(ang), (1, RET_HEADS))


def _decay_tables():
    log_gamma = jnp.log1p(-(2.0 ** (-5.0 - jnp.arange(RET_HEADS, dtype=F32))))
    i = jnp.arange(BLOCK, dtype=F32)
    gap = i[:, None] - i[None, :]
    decay = jnp.where(gap[None] >= 0, jnp.exp(jnp.maximum(gap, 0.0)[None] * log_gamma[:, None, None]), 0.0)
    head_of_col = (np.arange(RET_HEADS * RET_QK_DIM) % (RET_HEADS * RET_QK_DIM // 2)) // (RET_QK_DIM // 2)
    lg_col = log_gamma[head_of_col]
    dk_tab = jnp.exp((BLOCK - 1 - i)[:, None] * lg_col[None, :])
    dq_tab = jnp.exp((i + 1.0)[:, None] * lg_col[None, :])
    cd = jnp.broadcast_to(jnp.exp(BLOCK * log_gamma)[:, None, None], (RET_HEADS, 1, RET_V_DIM))
    return decay.astype(F32), dq_tab, dk_tab, cd


def _split_w_in(w16):
    cols = lambda name: w16[:, IN_OFF[name]:IN_OFF[name] + IN_W[name]]
    zeros = lambda w: jnp.zeros((w16.shape[0], w), BF16)
    rope = lambda a: jnp.swapaxes(a.reshape(-1, RET_HEADS, 2, RET_QK_DIM // 2), 1, 2).reshape(a.shape)
    w_bf = jnp.concatenate([cols("dq"), cols("iq"), cols("fq"), cols("fk"), cols("fv"), cols("rv"),
                            cols("ik"), zeros(LANE - IDX_DIM)], axis=1)
    w_misc = jnp.concatenate([rope(cols("rq")), rope(cols("rk")), cols("rg"), cols("dc"),
                              cols("iw"), cols("ff"), zeros(LANE - IDX_HEADS - FOX_HEADS),
                              zeros(LANE)], axis=1)
    assert w_bf.shape[1] == BF_COLS and w_misc.shape[1] == MISC_COLS
    return w_bf, w_misc, cols("gates")


def _bf_col_scale():
    scale = np.ones((BF_COLS,), np.float32)
    scale[2 * 1024:3 * 1024] = FOX_HEAD_DIM ** -0.5 * LOG2E
    return jnp.asarray(scale)


def _mixer(x, b, t, mix_norm, w_in16, kv_norm, w_uk, w_uv, f_bias, gn_w, w_branch16, w_out16, tables, topk):
    n = b * t
    bias_tiles, rope_tabs, decay_tabs = tables
    w_bf, w_misc, w_gates = _split_w_in(w_in16)
    pbf, misc, c = _proj(x, mix_norm, w_bf, _bf_col_scale(), w_misc, kv_norm)
    pbf3 = pbf.reshape(b, t, BF_COLS)
    misc3 = misc.reshape(b, t, MISC_COLS)

    c3 = c.reshape(b, t, DSA_LATENT)
    wuk, wuv = _dsa_weights(w_uk, w_uv)
    ones_rows = jnp.zeros((b, 8, t), BF16).at[:, 0].set(1)
    ct3 = jnp.concatenate([jnp.swapaxes(c3, 1, 2), ones_rows], axis=1)
    dsa_o = _dsa(pbf3, misc3, c3, ct3, wuk, wuv, bias_tiles, topk)

    cum = _cum_logf(misc3, f_bias)[:, :, SMALL_FF:SMALL_FF + FOX_HEADS]
    cum_h = jnp.swapaxes(cum, 1, 2)
    fox_o = _fox(pbf3, cum_h[:, :, None, :], cum_h[..., None])

    ret_o = _retention(misc3, pbf3, *rope_tabs, *decay_tabs, gn_w)

    merged = _merge(x, mix_norm, dsa_o.reshape(n, -1), fox_o.reshape(n, -1), ret_o.reshape(n, -1), w_gates,
                    w_branch16)
    return _resproj(merged, w_out16, x)


def kernel(x, meta_tokens, t5_table, ffn1_norm, ffn1_w1, ffn1_w3, ffn1_w2, mix_norm, w_in, dsa_kv_norm,
           dsa_w_uk, dsa_w_uv, fox_f_bias, ret_gn_w, w_branch, w_out, ffn2_norm, ffn2_w1, ffn2_w3, ffn2_w2,
           final_norm):
    b, seq, d = x.shape
    assert d == D_MODEL and seq % BLOCK == 0
    t = seq + BLOCK
    depth = ffn1_norm.shape[0]
    topk = min(INDEX_TOPK, seq // 4)
    h = jnp.concatenate([
        jnp.zeros((b, PAD, d), x.dtype),
        jnp.broadcast_to(meta_tokens.astype(x.dtype)[None], (b, N_META, d)),
        x,
    ], axis=1).reshape(b * t, d)
    cos, sin = _rope_tables(t)
    decay, dq_tab, dk_tab, cd = _decay_tables()
    tables = (_t5_bias_tiles(t5_table), (cos, sin), (dq_tab, dk_tab, decay, cd))
    bf = lambda w, l: w[l].astype(BF16)
    for l in range(depth):
        h = _ffn(h, ffn1_norm[l], bf(ffn1_w1, l), bf(ffn1_w3, l), bf(ffn1_w2, l))
        h = _mixer(h, b, t, mix_norm[l], bf(w_in, l), dsa_kv_norm[l], dsa_w_uk[l], dsa_w_uv[l],
                   fox_f_bias[l], ret_gn_w[l], bf(w_branch, l), bf(w_out, l), tables, topk)
        h = _ffn(h, ffn2_norm[l], bf(ffn2_w1, l), bf(ffn2_w3, l), bf(ffn2_w2, l),
                 final_g=final_norm if l == depth - 1 else None)
    return h.reshape(b, t, d)[:, BLOCK:]
```

```python
import functools
import math

import numpy as np
import jax
import jax.numpy as jnp
from jax import lax
from jax.experimental import pallas as pl
from jax.experimental.pallas import tpu as pltpu

F32 = jnp.float32
BF16 = jnp.bfloat16

LANE = 128
VMEM_LIMIT = 56 * 1024 * 1024

D_MODEL = 2048
N_META = 16
BLOCK = 128
PAD = BLOCK - N_META
RMS_EPS = 1e-6
GN_EPS = 1e-5
NEG = -1e30

DSA_HEADS = 8
DSA_HEAD_DIM = 128
DSA_LATENT = 256
IDX_HEADS = 16
IDX_DIM = 64
INDEX_TOPK = 256
FOX_HEADS = 8
FOX_HEAD_DIM = 128
RET_HEADS = 8
RET_QK_DIM = 64
RET_V_DIM = 128
ROPE_BASE = 10000.0
BRANCH_WIDTH = 1024
N_BRANCH = 3
T5_BUCKETS = 32
T5_MAX_EXACT = 16
T5_MAX_DIST = 128

IN_WIDTHS = (1024, 256, 1024, 64, 16, 1024, 1024, 1024, 8, 512, 512, 1024, 1024, 6144)
IN_NAMES = ("dq", "dc", "iq", "ik", "iw", "fq", "fk", "fv", "ff", "rq", "rk", "rv", "rg", "gates")
IN_OFF = dict(zip(IN_NAMES, np.cumsum((0,) + IN_WIDTHS[:-1]).tolist()))
IN_W = dict(zip(IN_NAMES, IN_WIDTHS))

INT_MIN = -2 ** 31
LOG2E = math.log2(math.e)
SCORE_GROUP = 4

BF_COLS = 6 * 1024 + LANE
MISC_COLS = 2560
SMALL_IW = 0
SMALL_FF = IDX_HEADS


def _dot(a, b):
    return jnp.dot(a, b, preferred_element_type=F32)


def _dot_nt(a, b):
    return lax.dot_general(a, b, (((1,), (1,)), ((), ())), preferred_element_type=F32)


def _dot_tn(a, b):
    return lax.dot_general(a, b, (((0,), (0,)), ((), ())), preferred_element_type=F32)


def _pick_tile(n, target, mult=8):
    best = None
    for t in range(mult, min(n, target) + 1, mult):
        if n % t == 0:
            best = t
    assert best is not None, (n, target, mult)
    return best


def _params(sem):
    return pltpu.CompilerParams(dimension_semantics=sem, vmem_limit_bytes=VMEM_LIMIT)


def _rms(x, g):
    return x * lax.rsqrt(jnp.mean(x * x, axis=-1, keepdims=True) + RMS_EPS) * g


def _ffn_kernel(*refs, nj, final):
    if final:
        x_ref, g_ref, w1_ref, w3_ref, w2_ref, fg_ref, o_ref, xn_ref, acc_ref = refs
    else:
        x_ref, g_ref, w1_ref, w3_ref, w2_ref, o_ref, xn_ref, acc_ref = refs
    j = pl.program_id(1)

    @pl.when(j == 0)
    def _():
        xn_ref[...] = _rms(x_ref[...], g_ref[...]).astype(BF16)
        acc_ref[...] = jnp.zeros_like(acc_ref)

    xn = xn_ref[...]
    h1 = _dot(xn, w1_ref[...])
    h3 = _dot(xn, w3_ref[...])
    g = (h1 * jax.nn.sigmoid(h1) * h3).astype(BF16)
    acc_ref[...] += _dot(g, w2_ref[...])

    @pl.when(j == nj - 1)
    def _():
        y = x_ref[...] + 0.5 * acc_ref[...]
        if final:
            y = _rms(y, fg_ref[...])
        o_ref[...] = y


def _ffn(x, g, w1, w3, w2, final_g=None):
    n, d = x.shape
    f = w1.shape[1]
    tm = _pick_tile(n, 640)
    tf = _pick_tile(f, 512, LANE)
    nj = f // tf
    final = final_g is not None
    in_specs = [
        pl.BlockSpec((tm, d), lambda i, j: (i, 0)),
        pl.BlockSpec((1, d), lambda i, j: (0, 0)),
        pl.BlockSpec((d, tf), lambda i, j: (0, j)),
        pl.BlockSpec((d, tf), lambda i, j: (0, j)),
        pl.BlockSpec((tf, d), lambda i, j: (j, 0)),
    ]
    args = [x, g.reshape(1, d), w1, w3, w2]
    if final:
        in_specs.append(pl.BlockSpec((1, d), lambda i, j: (0, 0)))
        args.append(final_g.reshape(1, d))
    return pl.pallas_call(
        functools.partial(_ffn_kernel, nj=nj, final=final),
        grid=(n // tm, nj),
        in_specs=in_specs,
        out_specs=pl.BlockSpec((tm, d), lambda i, j: (i, 0)),
        out_shape=jax.ShapeDtypeStruct((n, d), F32),
        scratch_shapes=[pltpu.VMEM((tm, d), BF16), pltpu.VMEM((tm, d), F32)],
        compiler_params=_params(("parallel", "arbitrary")),
    )(*args)


def _proj_kernel(x_ref, g_ref, wa_ref, sa_ref, wb_ref, kvn_ref, oa_ref, ob_ref, oc_ref, xn_ref, *, na, dc_tile):
    j = pl.program_id(1)

    @pl.when(j == 0)
    def _():
        xn_ref[...] = _rms(x_ref[...], g_ref[...]).astype(BF16)

    @pl.when(j < na)
    def _():
        oa_ref[...] = (_dot(xn_ref[...], wa_ref[...]) * sa_ref[...]).astype(oa_ref.dtype)

    @pl.when(j >= na)
    def _():
        r = _dot(xn_ref[...], wb_ref[...])
        ob_ref[...] = r

        @pl.when(j == na + dc_tile)
        def _():
            oc_ref[...] = _rms(r[:, 0:DSA_LATENT], kvn_ref[...]).astype(oc_ref.dtype)


def _proj(x, g, w_bf, bf_scale, w_misc, kv_norm):
    n, d = x.shape
    ca, cb = w_bf.shape[1], w_misc.shape[1]
    tm = _pick_tile(n, 640)
    ta = _pick_tile(ca, 1024, LANE)
    tb = _pick_tile(cb, 512, LANE)
    na, nb = ca // ta, cb // tb
    dc_col = IN_W["rq"] + IN_W["rk"] + IN_W["rg"]
    assert dc_col % tb == 0 and tb >= DSA_LATENT
    a_idx = lambda j: jnp.minimum(j, na - 1)
    b_idx = lambda j: jnp.maximum(j - na, 0)
    return pl.pallas_call(
        functools.partial(_proj_kernel, na=na, dc_tile=dc_col // tb),
        grid=(n // tm, na + nb),
        in_specs=[
            pl.BlockSpec((tm, d), lambda i, j: (i, 0)),
            pl.BlockSpec((1, d), lambda i, j: (0, 0)),
            pl.BlockSpec((d, ta), lambda i, j: (0, a_idx(j))),
            pl.BlockSpec((1, ta), lambda i, j: (0, a_idx(j))),
            pl.BlockSpec((d, tb), lambda i, j: (0, b_idx(j))),
            pl.BlockSpec((1, DSA_LATENT), lambda i, j: (0, 0)),
        ],
        out_specs=[
            pl.BlockSpec((tm, ta), lambda i, j: (i, a_idx(j))),
            pl.BlockSpec((tm, tb), lambda i, j: (i, b_idx(j))),
            pl.BlockSpec((tm, DSA_LATENT), lambda i, j: (i, 0)),
        ],
        out_shape=[jax.ShapeDtypeStruct((n, ca), BF16), jax.ShapeDtypeStruct((n, cb), F32),
                   jax.ShapeDtypeStruct((n, DSA_LATENT), BF16)],
        scratch_shapes=[pltpu.VMEM((tm, d), BF16)],
        compiler_params=_params(("parallel", "arbitrary")),
    )(x, g.reshape(1, d), w_bf, bf_scale.reshape(1, ca), w_misc, kv_norm.reshape(1, DSA_LATENT))


def _split3(x):
    hi = x.astype(BF16)
    r = x - hi.astype(F32)
    mid = r.astype(BF16)
    lo = (r - mid.astype(F32)).astype(BF16)
    return hi, mid, lo


def _cum_kernel(s_ref, b_ref, hi_ref, mid_ref, lo_ref, carry_ref):
    @pl.when(pl.program_id(1) == 0)
    def _():
        carry_ref[...] = jnp.zeros_like(carry_ref)

    z = s_ref[...] + b_ref[...]
    lf = jnp.minimum(z, 0.0) - jnp.log1p(jnp.exp(-jnp.abs(z)))
    r = lax.broadcasted_iota(jnp.int32, (BLOCK, BLOCK), 0)
    c = lax.broadcasted_iota(jnp.int32, (BLOCK, BLOCK), 1)
    tri = jnp.where(c <= r, 1.0, 0.0).astype(BF16)
    hi, mid, lo = _split3(lf)
    cum = (_dot(tri, lo) + _dot(tri, mid)) + _dot(tri, hi) + carry_ref[...]
    carry_ref[...] = cum[BLOCK - 1:BLOCK, :]
    hi_ref[...], mid_ref[...], lo_ref[...] = _split3(cum * LOG2E)


def _cum_logf(misc3, f_bias):
    b, t, _ = misc3.shape
    blk = (IN_W["rq"] + IN_W["rk"] + IN_W["rg"] + IN_W["dc"]) // LANE
    bias = jnp.zeros((1, LANE), F32).at[0, SMALL_FF:SMALL_FF + FOX_HEADS].set(f_bias)
    out = pl.BlockSpec((None, BLOCK, LANE), lambda bi, i: (bi, i, 0))
    return pl.pallas_call(
        _cum_kernel,
        grid=(b, t // BLOCK),
        in_specs=[
            pl.BlockSpec((None, BLOCK, LANE), lambda bi, i: (bi, i, blk)),
            pl.BlockSpec((1, LANE), lambda bi, i: (0, 0)),
        ],
        out_specs=[out, out, out],
        out_shape=[jax.ShapeDtypeStruct((b, t, LANE), BF16)] * 3,
        scratch_shapes=[pltpu.VMEM((1, LANE), F32)],
        compiler_params=_params(("parallel", "arbitrary")),
    )(misc3, bias)


def _bit_transpose32(words):
    a = list(words)
    j, m = 16, 0x0000FFFF
    while j:
        mask = jnp.int32(m if m < 2 ** 31 else m - 2 ** 32)
        for k in range(32):
            if not k & j:
                t = (lax.shift_right_logical(a[k], jnp.int32(j)) ^ a[k + j]) & mask
                a[k + j] = a[k + j] ^ t
                a[k] = a[k] ^ (t << j)
        j >>= 1
        m = (m ^ (m << j)) & 0xFFFFFFFF
    return a


FAR_BLOCKS = 4


def _dsa_kernel(dq_ref, iq_ref, sm_ref, ik_ref, c_ref, ct_ref, wuk_ref, wuv_ref, bias_ref,
                o_ref, s_ref, qlat_ref, p_ref, m_ref, acc_ref, lg0_ref, lg1_ref, lgn_ref, iqt_ref, wt_ref,
                planes_ref, alive_ref, p1_ref, *, topk, nblk):
    i = pl.program_id(1)
    key_i = lax.broadcasted_iota(jnp.int32, (BLOCK, BLOCK), 0)
    qry_i = lax.broadcasted_iota(jnp.int32, (BLOCK, BLOCK), 1)
    hs = lambda h: slice(h * BLOCK, (h + 1) * BLOCK)

    iq_t = jnp.transpose(iq_ref[...].astype(F32)).astype(BF16)
    for h in range(IDX_HEADS):
        iqt_ref[0:IDX_DIM, hs(h)] = iq_t[h * IDX_DIM:(h + 1) * IDX_DIM, :]
    iqt_ref[IDX_DIM:LANE, :] = jnp.zeros((LANE - IDX_DIM, IDX_HEADS * BLOCK), BF16)
    wt_ref[...] = jnp.transpose(sm_ref[...])[SMALL_IW:SMALL_IW + IDX_HEADS, :] * (IDX_HEADS ** -0.5 * IDX_DIM ** -0.5)

    for h in range(DSA_HEADS):
        ql = _dot_nt(wuk_ref[h], dq_ref[:, h * DSA_HEAD_DIM:(h + 1) * DSA_HEAD_DIM]) * (DSA_HEAD_DIM ** -0.5 * LOG2E)
        qlat_ref[:, hs(h)] = ql.astype(BF16)

    ngrp = (i + SCORE_GROUP) // SCORE_GROUP
    npair = (ngrp + 1) // 2
    for u in range(1, 2 * SCORE_GROUP):
        s_ref[i + u] = jnp.full((BLOCK, BLOCK), INT_MIN, jnp.int32)

    def score_group(g, _):
        j0 = jnp.minimum(g * SCORE_GROUP, nblk - SCORE_GROUP)
        kt = ik_ref[pl.ds(pl.multiple_of(j0 * BLOCK, BLOCK), SCORE_GROUP * BLOCK), :]
        r = _dot(kt, iqt_ref[...])
        t = i * BLOCK + qry_i
        for u in range(SCORE_GROUP):
            acc = jnp.zeros((BLOCK, BLOCK), F32)
            for h in range(IDX_HEADS):
                acc = acc + jnp.maximum(r[u * BLOCK:(u + 1) * BLOCK, hs(h)], 0.0) * wt_ref[h:h + 1, :]
            s = (j0 + u) * BLOCK + key_i
            ok = (s <= t) & ((s >= PAD) | (s == t))
            bits = lax.bitcast_convert_type(acc, jnp.int32)
            bits = jnp.where(bits == INT_MIN, 0, bits)
            key = bits ^ ((bits >> 31) & 0x7FFFFFFF)
            s_ref[j0 + u] = jnp.where(ok, key, INT_MIN)
        return 0

    lax.fori_loop(0, ngrp, score_group, 0)

    def slice_group(g, _):
        for half in range(2):
            words = []
            for u in (2 * half, 2 * half + 1):
                tile = s_ref[g * SCORE_GROUP + u] ^ INT_MIN
                words += [tile[r * 8:(r + 1) * 8, :] for r in range(BLOCK // 8)]
            planes = _bit_transpose32(words)
            for bit in range(32):
                planes_ref[2 * g + half, bit] = planes[bit]
            planes_ref[2 * g + half, 32] = jnp.full((8, BLOCK), -1, jnp.int32)
        return 0

    lax.fori_loop(0, 2 * npair, slice_group, 0)
    alive_ref[...] = jnp.full(alive_ref.shape, -1, jnp.int32)

    def radix_pass(t, carry):
        thr, need, drop_prev = carry
        bit = 31 - t

        def body(pr, accs):
            accs = list(accs)
            for u in range(4):
                st = 4 * pr + u
                alive = alive_ref[st] & (planes_ref[st, bit + 1] ^ drop_prev)
                alive_ref[st] = alive
                accs[u] = accs[u] + lax.population_count(alive & planes_ref[st, bit])
            return tuple(accs)

        zero = jnp.zeros((8, BLOCK), jnp.int32)
        accs = lax.fori_loop(0, npair, body, (zero,) * 4)
        cnt = ((accs[0] + accs[1]) + (accs[2] + accs[3])).astype(F32)
        cnt = jnp.sum(cnt, axis=0, keepdims=True).astype(jnp.int32)
        take = cnt >= need
        thr = thr | jnp.where(take, jnp.int32(1) << bit, 0)
        need = jnp.where(take, need, need - cnt)
        return thr, need, jnp.where(take, 0, -1)

    zero = jnp.zeros((1, BLOCK), jnp.int32)
    thr, _, _ = lax.fori_loop(0, 32, radix_pass, (zero, zero + topk, zero))
    thr = jnp.maximum(thr ^ INT_MIN, INT_MIN + 1)

    m_ref[...] = jnp.full(m_ref.shape, NEG, F32)
    acc_ref[...] = jnp.zeros_like(acc_ref)

    def logits(j0, nb, dst_ref):
        w = nb * BLOCK
        dst_ref[0:w, :] = _dot(c_ref[pl.ds(pl.multiple_of(j0 * BLOCK, BLOCK), w), :], qlat_ref[...])

    def softmax_pv(j0, nb, lg_ref, pb_ref, bias_off=None):
        w = nb * BLOCK
        ckt = ct_ref[:, pl.ds(pl.multiple_of(j0 * BLOCK, BLOCK), w)]
        masked = jnp.where(s_ref[pl.ds(j0, nb)].reshape(w, BLOCK) >= thr, 0.0, NEG)
        alphas = []
        for h in range(DSA_HEADS):
            x = lg_ref[0:w, hs(h)] + masked
            if bias_off is not None:
                x = x + bias_ref[h, pl.ds(bias_off, w), :]
            m_prev = m_ref[h:h + 1, :]
            m_new = jnp.maximum(m_prev, jnp.max(x, axis=0, keepdims=True))
            alphas.append(jnp.exp2(m_prev - m_new))
            m_ref[h:h + 1, :] = m_new
            pb_ref[0:w, hs(h)] = jnp.exp2(x - m_new).astype(BF16)
        pv = _dot(ckt, pb_ref[0:w, :])
        for h in range(DSA_HEADS):
            acc_ref[:, hs(h)] = acc_ref[:, hs(h)] * alphas[h] + pv[:, hs(h)]

    near0 = jnp.maximum(i - 1, 0)
    logits(near0, 2, lgn_ref)

    n_far = jnp.maximum(i - 1, 0)
    n_big = n_far // FAR_BLOCKS
    big = lambda g: g * FAR_BLOCKS

    @pl.when(n_big > 0)
    def _():
        logits(0, FAR_BLOCKS, lg0_ref)

        def pair_of_tiles(g, _):
            logits(big(2 * g + 1), FAR_BLOCKS, lg1_ref)
            softmax_pv(big(2 * g), FAR_BLOCKS, lg0_ref, p_ref)
            logits(big(jnp.minimum(2 * g + 2, n_big - 1)), FAR_BLOCKS, lg0_ref)
            softmax_pv(big(2 * g + 1), FAR_BLOCKS, lg1_ref, p1_ref)
            return 0

        lax.fori_loop(0, n_big // 2, pair_of_tiles, 0)

        @pl.when(n_big % 2 == 1)
        def _():
            softmax_pv(big(n_big - 1), FAR_BLOCKS, lg0_ref, p_ref)

    def far_one(j, _):
        logits(j, 1, lg0_ref)
        softmax_pv(j, 1, lg0_ref, p_ref)
        return 0

    lax.fori_loop(n_big * FAR_BLOCKS, n_far, far_one, 0)
    softmax_pv(near0, 2, lgn_ref, p1_ref, bias_off=pl.multiple_of(jnp.where(i >= 1, 0, BLOCK), BLOCK))

    for h in range(DSA_HEADS):
        inv_l = 1.0 / acc_ref[DSA_LATENT:DSA_LATENT + 1, hs(h)]
        o_lat = (acc_ref[0:DSA_LATENT, hs(h)] * inv_l).astype(BF16)
        o = _dot_tn(o_lat, wuv_ref[h])
        o_ref[:, h * DSA_HEAD_DIM:(h + 1) * DSA_HEAD_DIM] = o.astype(o_ref.dtype)


def _dsa(pbf3, misc3, c3, ct3, wuk, wuv, bias, topk):
    b, t, _ = pbf3.shape
    nblk = t // BLOCK
    hq = DSA_HEADS * DSA_HEAD_DIM
    ik_blk = 6 * 1024 // LANE
    small_blk = (IN_W["rq"] + IN_W["rk"] + IN_W["rg"] + IN_W["dc"]) // LANE
    n_sets = 4 * (((nblk + SCORE_GROUP - 1) // SCORE_GROUP + 1) // 2)
    n_tiles = nblk + 2 * SCORE_GROUP - 1
    const3 = lambda bi, i: (0, 0, 0)
    return pl.pallas_call(
        functools.partial(_dsa_kernel, topk=topk, nblk=nblk),
        grid=(b, nblk),
        in_specs=[
            pl.BlockSpec((None, BLOCK, hq), lambda bi, i: (bi, i, 0)),
            pl.BlockSpec((None, BLOCK, IDX_HEADS * IDX_DIM), lambda bi, i: (bi, i, 1)),
            pl.BlockSpec((None, BLOCK, LANE), lambda bi, i: (bi, i, small_blk)),
            pl.BlockSpec((None, t, LANE), lambda bi, i: (bi, 0, ik_blk)),
            pl.BlockSpec((None, t, DSA_LATENT), lambda bi, i: (bi, 0, 0)),
            pl.BlockSpec((None, DSA_LATENT + 8, t), lambda bi, i: (bi, 0, 0)),
            pl.BlockSpec((DSA_HEADS, DSA_LATENT, DSA_HEAD_DIM), const3),
            pl.BlockSpec((DSA_HEADS, DSA_LATENT, DSA_HEAD_DIM), const3),
            pl.BlockSpec((DSA_HEADS, 3 * BLOCK, BLOCK), const3),
        ],
        out_specs=pl.BlockSpec((None, BLOCK, hq), lambda bi, i: (bi, i, 0)),
        out_shape=jax.ShapeDtypeStruct((b, t, hq), BF16),
        scratch_shapes=[
            pltpu.VMEM((n_tiles, BLOCK, BLOCK), jnp.int32),
            pltpu.VMEM((DSA_LATENT, DSA_HEADS * BLOCK), BF16),
            pltpu.VMEM((FAR_BLOCKS * BLOCK, DSA_HEADS * BLOCK), BF16),
            pltpu.VMEM((DSA_HEADS, BLOCK), F32),
            pltpu.VMEM((DSA_LATENT + 8, DSA_HEADS * BLOCK), F32),
            pltpu.VMEM((FAR_BLOCKS * BLOCK, DSA_HEADS * BLOCK), F32),
            pltpu.VMEM((FAR_BLOCKS * BLOCK, DSA_HEADS * BLOCK), F32),
            pltpu.VMEM((2 * BLOCK, DSA_HEADS * BLOCK), F32),
            pltpu.VMEM((LANE, IDX_HEADS * BLOCK), BF16),
            pltpu.VMEM((IDX_HEADS, BLOCK), F32),
            pltpu.VMEM((n_sets, 33, 8, BLOCK), jnp.int32),
            pltpu.VMEM((n_sets, 8, BLOCK), jnp.int32),
            pltpu.VMEM((FAR_BLOCKS * BLOCK, DSA_HEADS * BLOCK), BF16),
        ],
        compiler_params=_params(("parallel", "arbitrary")),
    )(pbf3, pbf3, misc3, pbf3, c3, ct3, wuk, wuv, bias)


def _dsa_weights(w_uk, w_uv):
    return jnp.transpose(w_uk, (1, 0, 2)).astype(BF16), jnp.transpose(w_uv, (1, 0, 2)).astype(BF16)


FOX_ROWS = 64


def _fox_kernel(q_ref, qa_ref, k_ref, ka_ref, v_ref, o_ref, m_ref, l_ref, acc_ref, lg_ref, p_ref, *, tq):
    i = pl.program_id(2)
    qq = jnp.concatenate([q_ref[...], qa_ref[...]], axis=1)
    m_ref[...] = jnp.full(m_ref.shape, NEG, F32)
    l_ref[...] = jnp.zeros_like(l_ref)
    acc_ref[...] = jnp.zeros_like(acc_ref)
    rows = min(FOX_ROWS, tq)
    chunks = [slice(c, c + rows) for c in range(0, tq, rows)]

    def qk(j, slot):
        off = pl.multiple_of(j * tq, tq)
        kk = jnp.concatenate([k_ref[pl.ds(off, tq), :], ka_ref[pl.ds(off, tq), :]], axis=1)
        lg_ref[slot] = _dot_nt(kk, qq)

    def softmax_pv(j, slot, mode):
        def logits(rs):
            x = lg_ref[slot, rs, :]
            if mode == "none":
                return x
            s = rs.start + lax.broadcasted_iota(jnp.int32, (rows, 1), 0)
            if mode == "pad":
                return jnp.where(s >= PAD, x, NEG)
            t = lax.broadcasted_iota(jnp.int32, (1, tq), 1)
            return jnp.where((s <= t) & ((s + j * tq >= PAD) | (s == t)), x, NEG)

        fold = lambda a: a.reshape(rows // 8, 8, tq)
        mx = jnp.full((8, tq), NEG, F32)
        for rs in chunks:
            mx = jnp.maximum(mx, jnp.max(fold(logits(rs)), axis=0))
        m_prev = m_ref[...]
        m_new = jnp.maximum(m_prev, jnp.max(mx, axis=0, keepdims=True))
        alpha = jnp.exp2(m_prev - m_new)
        tot = jnp.zeros((8, tq), F32)
        for rs in chunks:
            p = jnp.exp2(logits(rs) - m_new)
            tot = tot + jnp.sum(fold(p), axis=0)
            p_ref[rs, :] = p.astype(BF16)
        l_ref[...] = alpha * l_ref[...] + jnp.sum(tot, axis=0, keepdims=True)
        off = pl.multiple_of(j * tq, tq)
        acc_ref[...] = acc_ref[...] * alpha + _dot_tn(v_ref[pl.ds(off, tq), :], p_ref[...])
        m_ref[...] = m_new

    qk(i, 2)

    @pl.when(i > 0)
    def _():
        qk(0, 0)
        qk(jnp.minimum(1, i), 1)
        softmax_pv(0, 0, "pad")
        n_mid = i - 1

        def pair(g, _):
            a = 2 * g + 1
            qk(jnp.minimum(a + 1, i), 0)
            softmax_pv(a, 1, "none")
            qk(jnp.minimum(a + 2, i), 1)
            softmax_pv(a + 1, 0, "none")
            return 0

        lax.fori_loop(0, n_mid // 2, pair, 0)

        @pl.when(n_mid % 2 == 1)
        def _():
            softmax_pv(i - 1, 1, "none")

    softmax_pv(i, 2, "diag")

    o_ref[...] = jnp.transpose(acc_ref[...] * (1.0 / l_ref[...])).astype(o_ref.dtype)


def _fox(pbf3, aug_q, aug_k):
    b, t, _ = pbf3.shape
    nblk = t // BLOCK
    tq = BLOCK * max(k for k in (1, 5) if nblk % k == 0)
    d = FOX_HEAD_DIM
    qb, kb, vb = 2 * 1024 // d, 3 * 1024 // d, 4 * 1024 // d
    return pl.pallas_call(
        functools.partial(_fox_kernel, tq=tq),
        grid=(b, FOX_HEADS, t // tq),
        in_specs=[
            pl.BlockSpec((None, tq, d), lambda bi, h, i: (bi, i, qb + h)),
            pl.BlockSpec((None, tq, d), lambda bi, h, i: (bi, i, h)),
            pl.BlockSpec((None, t, d), lambda bi, h, i: (bi, 0, kb + h)),
            pl.BlockSpec((None, t, d), lambda bi, h, i: (bi, 0, h)),
            pl.BlockSpec((None, t, d), lambda bi, h, i: (bi, 0, vb + h)),
        ],
        out_specs=pl.BlockSpec((None, tq, d), lambda bi, h, i: (bi, i, h)),
        out_shape=jax.ShapeDtypeStruct((b, t, FOX_HEADS * d), BF16),
        scratch_shapes=[pltpu.VMEM((1, tq), F32), pltpu.VMEM((1, tq), F32), pltpu.VMEM((d, tq), F32),
                        pltpu.VMEM((3, tq, tq), F32), pltpu.VMEM((tq, tq), BF16)],
        compiler_params=_params(("parallel", "parallel", "arbitrary")),
    )(pbf3, aug_q, pbf3, aug_k, pbf3)


def _fox_bias_columns(hi, mid, lo):
    b, t, _ = hi.shape
    pieces = jnp.stack([a[:, :, SMALL_FF:SMALL_FF + FOX_HEADS] for a in (hi, mid, lo)], axis=-1)
    ones = jnp.ones_like(pieces)
    zeros = jnp.zeros((b, t, FOX_HEADS, FOX_HEAD_DIM - 6), BF16)
    aug_q = jnp.concatenate([pieces, ones, zeros], axis=-1).reshape(b, t, FOX_HEADS * FOX_HEAD_DIM)
    aug_k = jnp.concatenate([ones, -pieces, zeros], axis=-1).reshape(b, t, FOX_HEADS * FOX_HEAD_DIM)
    return aug_q, aug_k


def _ret_kernel(rq_ref, rk_ref, rv_ref, rg_ref, cos_ref, sin_ref, dq_ref, dk_ref, decay_ref, cd_ref, gn_ref,
                o_ref, state_ref):
    i = pl.program_id(1)

    @pl.when(i == 0)
    def _():
        state_ref[...] = jnp.zeros_like(state_ref)

    half = RET_HEADS * RET_QK_DIM // 2
    cos = cos_ref[...]
    sin = sin_ref[...]

    def rope(x):
        x1, x2 = x[:, :half], x[:, half:]
        return jnp.concatenate([x1 * cos - x2 * sin, x1 * sin + x2 * cos], axis=1)

    q = rope(rq_ref[...])
    k = rope(rk_ref[...]) * RET_QK_DIM ** -0.5
    pos = i * BLOCK + lax.broadcasted_iota(jnp.int32, (BLOCK, 1), 0)
    k = jnp.where(pos >= PAD, k, 0.0)
    qb = q.astype(BF16)
    kb = k.astype(BF16)
    q_dec = (q * dq_ref[...]).astype(BF16)
    k_dec = (k * dk_ref[...]).astype(BF16)
    head_of_col = (lax.broadcasted_iota(jnp.int32, (1, 2 * half), 1) % half) // (RET_QK_DIM // 2)
    zero = jnp.zeros((), BF16)
    for h in range(RET_HEADS):
        mine = head_of_col == h
        vh = rv_ref[:, h * RET_V_DIM:(h + 1) * RET_V_DIM]
        s = (_dot_nt(jnp.where(mine, qb, zero), kb) * decay_ref[h]).astype(BF16)
        st = state_ref[h]
        o = _dot(s, vh) + _dot(jnp.where(mine, q_dec, zero), st.astype(BF16))
        state_ref[h] = cd_ref[h] * st + _dot_tn(k_dec, vh)
        mu = jnp.mean(o, axis=1, keepdims=True)
        var = jnp.mean(jnp.square(o - mu), axis=1, keepdims=True)
        sl = slice(h * RET_V_DIM, (h + 1) * RET_V_DIM)
        o = (o - mu) * lax.rsqrt(var + GN_EPS) * gn_ref[:, sl]
        g = rg_ref[:, sl]
        o_ref[:, sl] = (g * jax.nn.sigmoid(g) * o).astype(o_ref.dtype)


def _retention(misc3, pbf3, cos, sin, dq_tab, dk_tab, decay, cd, gn_w):
    b, t, _ = misc3.shape
    qk = RET_HEADS * RET_QK_DIM
    vw = RET_HEADS * RET_V_DIM
    const2 = lambda bi, i: (0, 0)
    const3 = lambda bi, i: (0, 0, 0)
    return pl.pallas_call(
        _ret_kernel,
        grid=(b, t // BLOCK),
        in_specs=[
            pl.BlockSpec((None, BLOCK, qk), lambda bi, i: (bi, i, 0)),
            pl.BlockSpec((None, BLOCK, qk), lambda bi, i: (bi, i, 1)),
            pl.BlockSpec((None, BLOCK, vw), lambda bi, i: (bi, i, 5)),
            pl.BlockSpec((None, BLOCK, vw), lambda bi, i: (bi, i, 1)),
            pl.BlockSpec((BLOCK, qk // 2), lambda bi, i: (i, 0)),
            pl.BlockSpec((BLOCK, qk // 2), lambda bi, i: (i, 0)),
            pl.BlockSpec((BLOCK, qk), const2),
            pl.BlockSpec((BLOCK, qk), const2),
            pl.BlockSpec((RET_HEADS, BLOCK, BLOCK), const3),
            pl.BlockSpec((RET_HEADS, 1, RET_V_DIM), const3),
            pl.BlockSpec((1, vw), const2),
        ],
        out_specs=pl.BlockSpec((None, BLOCK, vw), lambda bi, i: (bi, i, 0)),
        out_shape=jax.ShapeDtypeStruct((b, t, vw), BF16),
        scratch_shapes=[pltpu.VMEM((RET_HEADS, qk, RET_V_DIM), F32)],
        compiler_params=_params(("parallel", "arbitrary")),
    )(misc3, misc3, pbf3, misc3, cos, sin, dq_tab, dk_tab, decay, cd, gn_w.reshape(1, vw))


def _merge_kernel(x_ref, g_ref, b0_ref, b1_ref, b2_ref, wg0_ref, wg1_ref, wg2_ref, wb_ref, o_ref, xn_ref):
    @pl.when(pl.program_id(1) == 0)
    def _():
        xn_ref[...] = _rms(x_ref[...], g_ref[...]).astype(BF16)

    xn = xn_ref[...]
    acc = jax.nn.sigmoid(_dot(xn, wg0_ref[...])) * _dot(b0_ref[...], wb_ref[0])
    acc = acc + jax.nn.sigmoid(_dot(xn, wg1_ref[...])) * _dot(b1_ref[...], wb_ref[1])
    acc = acc + jax.nn.sigmoid(_dot(xn, wg2_ref[...])) * _dot(b2_ref[...], wb_ref[2])
    o_ref[...] = acc.astype(o_ref.dtype)


def _merge(x, g, b0, b1, b2, w_gates, w_branch):
    n, bw = b0.shape
    dm = x.shape[1]
    d = w_branch.shape[2]
    tm = _pick_tile(n, 640)
    tn = _pick_tile(d, 512, LANE)
    nj = d // tn
    bspec = pl.BlockSpec((tm, bw), lambda i, j: (i, 0))
    gspec = lambda r: pl.BlockSpec((dm, tn), lambda i, j: (0, r * nj + j))
    return pl.pallas_call(
        _merge_kernel,
        grid=(n // tm, nj),
        in_specs=[pl.BlockSpec((tm, dm), lambda i, j: (i, 0)), pl.BlockSpec((1, dm), lambda i, j: (0, 0)),
                  bspec, bspec, bspec, gspec(0), gspec(1), gspec(2),
                  pl.BlockSpec((N_BRANCH, bw, tn), lambda i, j: (0, 0, j))],
        out_specs=pl.BlockSpec((tm, tn), lambda i, j: (i, j)),
        out_shape=jax.ShapeDtypeStruct((n, d), BF16),
        scratch_shapes=[pltpu.VMEM((tm, dm), BF16)],
        compiler_params=_params(("parallel", "arbitrary")),
    )(x, g.reshape(1, dm), b0, b1, b2, w_gates, w_gates, w_gates, w_branch)


def _resproj_kernel(a_ref, w_ref, r_ref, o_ref):
    o_ref[...] = r_ref[...] + _dot(a_ref[...], w_ref[...])


def _resproj(a, w, res):
    n, k = a.shape
    d = w.shape[1]
    tm = _pick_tile(n, 640)
    tn = _pick_tile(d, 512, LANE)
    return pl.pallas_call(
        _resproj_kernel,
        grid=(n // tm, d // tn),
        in_specs=[
            pl.BlockSpec((tm, k), lambda i, j: (i, 0)),
            pl.BlockSpec((k, tn), lambda i, j: (0, j)),
            pl.BlockSpec((tm, tn), lambda i, j: (i, j)),
        ],
        out_specs=pl.BlockSpec((tm, tn), lambda i, j: (i, j)),
        out_shape=jax.ShapeDtypeStruct((n, d), F32),
        compiler_params=_params(("parallel", "arbitrary")),
    )(a, w, res)


def _t5_bucket(dist):
    d = jnp.maximum(dist, 1).astype(F32)
    large = T5_MAX_EXACT + (jnp.log(d / T5_MAX_EXACT) / math.log(T5_MAX_DIST / T5_MAX_EXACT)
                            * (T5_BUCKETS - T5_MAX_EXACT)).astype(jnp.int32)
    large = jnp.minimum(large, T5_BUCKETS - 1)
    return jnp.where(dist < T5_MAX_EXACT, dist, large)


def _t5_bias_tiles(t5_table):
    k = jnp.arange(BLOCK, dtype=jnp.int32)[:, None]
    q = jnp.arange(BLOCK, dtype=jnp.int32)[None, :]
    table = t5_table.astype(F32)
    rel = table - table[T5_BUCKETS - 1]

    def lookup(dist):
        onehot = (_t5_bucket(dist)[..., None] == jnp.arange(T5_BUCKETS, dtype=jnp.int32)).astype(F32)
        return jnp.einsum("kqb,bh->hkq", onehot, rel, precision=lax.Precision.HIGHEST)

    bdiag = lookup(jnp.maximum(q - k, 0))
    bprev = lookup(BLOCK + q - k)
    return jnp.concatenate([bprev, bdiag, jnp.zeros_like(bdiag)], axis=1) * LOG2E


def _rope_tables(t):
    half = RET_QK_DIM // 2
    inv = ROPE_BASE ** (-jnp.arange(half, dtype=F32) / half)
    ang = jnp.arange(t, dtype=jnp.int32).astype(F32)[:, None] * inv[None, :]
    return jnp.tile(jnp.cos(ang), (1, RET_HEADS)), jnp.tile(jnp.sin(ang), (1, RET_HEADS))


def _decay_tables():
    log_gamma = jnp.log1p(-(2.0 ** (-5.0 - jnp.arange(RET_HEADS, dtype=F32))))
    i = jnp.arange(BLOCK, dtype=F32)
    gap = i[:, None] - i[None, :]
    decay = jnp.where(gap[None] >= 0, jnp.exp(jnp.maximum(gap, 0.0)[None] * log_gamma[:, None, None]), 0.0)
    head_of_col = (np.arange(RET_HEADS * RET_QK_DIM) % (RET_HEADS * RET_QK_DIM // 2)) // (RET_QK_DIM // 2)
    lg_col = log_gamma[head_of_col]
    dk_tab = jnp.exp((BLOCK - 1 - i)[:, None] * lg_col[None, :])
    dq_tab = jnp.exp((i + 1.0)[:, None] * lg_col[None, :])
    cd = jnp.broadcast_to(jnp.exp(BLOCK * log_gamma)[:, None, None], (RET_HEADS, 1, RET_V_DIM))
    return decay.astype(F32), dq_tab, dk_tab, cd


def _split_w_in(w16):
    cols = lambda name: w16[:, IN_OFF[name]:IN_OFF[name] + IN_W[name]]
    zeros = lambda w: jnp.zeros((w16.shape[0], w), BF16)
    rope = lambda a: jnp.swapaxes(a.reshape(-1, RET_HEADS, 2, RET_QK_DIM // 2), 1, 2).reshape(a.shape)
    w_bf = jnp.concatenate([cols("dq"), cols("iq"), cols("fq"), cols("fk"), cols("fv"), cols("rv"),
                            cols("ik"), zeros(LANE - IDX_DIM)], axis=1)
    w_misc = jnp.concatenate([rope(cols("rq")), rope(cols("rk")), cols("rg"), cols("dc"),
                              cols("iw"), cols("ff"), zeros(LANE - IDX_HEADS - FOX_HEADS),
                              zeros(LANE)], axis=1)
    assert w_bf.shape[1] == BF_COLS and w_misc.shape[1] == MISC_COLS
    return w_bf, w_misc, cols("gates")


def _bf_col_scale():
    scale = np.ones((BF_COLS,), np.float32)
    scale[2 * 1024:3 * 1024] = FOX_HEAD_DIM ** -0.5 * LOG2E
    return jnp.asarray(scale)


def _mixer(x, b, t, mix_norm, w_in16, kv_norm, w_uk, w_uv, f_bias, gn_w, w_branch16, w_out16, tables, topk):
    n = b * t
    bias_tiles, rope_tabs, decay_tabs = tables
    w_bf, w_misc, w_gates = _split_w_in(w_in16)
    pbf, misc, c = _proj(x, mix_norm, w_bf, _bf_col_scale(), w_misc, kv_norm)
    pbf3 = pbf.reshape(b, t, BF_COLS)
    misc3 = misc.reshape(b, t, MISC_COLS)

    c3 = c.reshape(b, t, DSA_LATENT)
    wuk, wuv = _dsa_weights(w_uk, w_uv)
    ones_rows = jnp.zeros((b, 8, t), BF16).at[:, 0].set(1)
    ct3 = jnp.concatenate([jnp.swapaxes(c3, 1, 2), ones_rows], axis=1)
    dsa_o = _dsa(pbf3, misc3, c3, ct3, wuk, wuv, bias_tiles, topk)

    fox_o = _fox(pbf3, *_fox_bias_columns(*_cum_logf(misc3, f_bias)))

    ret_o = _retention(misc3, pbf3, *rope_tabs, *decay_tabs, gn_w)

    merged = _merge(x, mix_norm, dsa_o.reshape(n, -1), fox_o.reshape(n, -1), ret_o.reshape(n, -1), w_gates,
                    w_branch16)
    return _resproj(merged, w_out16, x)


def kernel(x, meta_tokens, t5_table, ffn1_norm, ffn1_w1, ffn1_w3, ffn1_w2, mix_norm, w_in, dsa_kv_norm,
           dsa_w_uk, dsa_w_uv, fox_f_bias, ret_gn_w, w_branch, w_out, ffn2_norm, ffn2_w1, ffn2_w3, ffn2_w2,
           final_norm):
    b, seq, d = x.shape
    assert d == D_MODEL and seq % BLOCK == 0
    t = seq + BLOCK
    depth = ffn1_norm.shape[0]
    topk = min(INDEX_TOPK, seq // 4)
    h = jnp.concatenate([
        jnp.zeros((b, PAD, d), x.dtype),
        jnp.broadcast_to(meta_tokens.astype(x.dtype)[None], (b, N_META, d)),
        x,
    ], axis=1).reshape(b * t, d)
    cos, sin = _rope_tables(t)
    decay, dq_tab, dk_tab, cd = _decay_tables()
    tables = (_t5_bias_tiles(t5_table), (cos, sin), (dq_tab, dk_tab, decay, cd))
    bf = lambda w, l: w[l].astype(BF16)
    for l in range(depth):
        h = _ffn(h, ffn1_norm[l], bf(ffn1_w1, l), bf(ffn1_w3, l), bf(ffn1_w2, l))
        h = _mixer(h, b, t, mix_norm[l], bf(w_in, l), dsa_kv_norm[l], dsa_w_uk[l], dsa_w_uv[l],
                   fox_f_bias[l], ret_gn_w[l], bf(w_branch, l), bf(w_out, l), tables, topk)
        h = _ffn(h, ffn2_norm[l], bf(ffn2_w1, l), bf(ffn2_w3, l), bf(ffn2_w2, l),
                 final_g=final_norm if l == depth - 1 else None)
    return h.reshape(b, t, d)[:, BLOCK:]
```

```python
import functools
import math

import numpy as np
import jax
import jax.numpy as jnp
from jax import lax
from jax.experimental import pallas as pl
from jax.experimental.pallas import tpu as pltpu

F32 = jnp.float32
BF16 = jnp.bfloat16

LANE = 128
VMEM_LIMIT = 56 * 1024 * 1024

D_MODEL = 2048
N_META = 16
BLOCK = 128
PAD = BLOCK - N_META
RMS_EPS = 1e-6
GN_EPS = 1e-5
NEG = -1e30

DSA_HEADS = 8
DSA_HEAD_DIM = 128
DSA_LATENT = 256
IDX_HEADS = 16
IDX_DIM = 64
INDEX_TOPK = 256
FOX_HEADS = 8
FOX_HEAD_DIM = 128
RET_HEADS = 8
RET_QK_DIM = 64
RET_V_DIM = 128
ROPE_BASE = 10000.0
BRANCH_WIDTH = 1024
N_BRANCH = 3
T5_BUCKETS = 32
T5_MAX_EXACT = 16
T5_MAX_DIST = 128

IN_WIDTHS = (1024, 256, 1024, 64, 16, 1024, 1024, 1024, 8, 512, 512, 1024, 1024, 6144)
IN_NAMES = ("dq", "dc", "iq", "ik", "iw", "fq", "fk", "fv", "ff", "rq", "rk", "rv", "rg", "gates")
IN_OFF = dict(zip(IN_NAMES, np.cumsum((0,) + IN_WIDTHS[:-1]).tolist()))
IN_W = dict(zip(IN_NAMES, IN_WIDTHS))

INT_MIN = -2 ** 31
LOG2E = math.log2(math.e)
SCORE_GROUP = 4

BF_COLS = 6 * 1024 + LANE
MISC_COLS = 2560
SMALL_IW = 0
SMALL_FF = IDX_HEADS


def _dot(a, b):
    return jnp.dot(a, b, preferred_element_type=F32)


def _dot_nt(a, b):
    return lax.dot_general(a, b, (((1,), (1,)), ((), ())), preferred_element_type=F32)


def _dot_tn(a, b):
    return lax.dot_general(a, b, (((0,), (0,)), ((), ())), preferred_element_type=F32)


def _pick_tile(n, target, mult=8):
    best = None
    for t in range(mult, min(n, target) + 1, mult):
        if n % t == 0:
            best = t
    assert best is not None, (n, target, mult)
    return best


def _params(sem):
    return pltpu.CompilerParams(dimension_semantics=sem, vmem_limit_bytes=VMEM_LIMIT)


def _rms(x, g):
    return x * lax.rsqrt(jnp.mean(x * x, axis=-1, keepdims=True) + RMS_EPS) * g


def _ffn_kernel(*refs, nj, final):
    if final:
        x_ref, g_ref, w1_ref, w3_ref, w2_ref, fg_ref, o_ref, xn_ref, acc_ref = refs
    else:
        x_ref, g_ref, w1_ref, w3_ref, w2_ref, o_ref, xn_ref, acc_ref = refs
    j = pl.program_id(1)

    @pl.when(j == 0)
    def _():
        xn_ref[...] = _rms(x_ref[...], g_ref[...]).astype(BF16)
        acc_ref[...] = jnp.zeros_like(acc_ref)

    xn = xn_ref[...]
    h1 = _dot(xn, w1_ref[...])
    h3 = _dot(xn, w3_ref[...])
    g = (h1 * jax.nn.sigmoid(h1) * h3).astype(BF16)
    acc_ref[...] += _dot(g, w2_ref[...])

    @pl.when(j == nj - 1)
    def _():
        y = x_ref[...] + 0.5 * acc_ref[...]
        if final:
            y = _rms(y, fg_ref[...])
        o_ref[...] = y


def _ffn(x, g, w1, w3, w2, final_g=None):
    n, d = x.shape
    f = w1.shape[1]
    tm = _pick_tile(n, 640)
    tf = _pick_tile(f, 512, LANE)
    nj = f // tf
    final = final_g is not None
    in_specs = [
        pl.BlockSpec((tm, d), lambda i, j: (i, 0)),
        pl.BlockSpec((1, d), lambda i, j: (0, 0)),
        pl.BlockSpec((d, tf), lambda i, j: (0, j)),
        pl.BlockSpec((d, tf), lambda i, j: (0, j)),
        pl.BlockSpec((tf, d), lambda i, j: (j, 0)),
    ]
    args = [x, g.reshape(1, d), w1, w3, w2]
    if final:
        in_specs.append(pl.BlockSpec((1, d), lambda i, j: (0, 0)))
        args.append(final_g.reshape(1, d))
    return pl.pallas_call(
        functools.partial(_ffn_kernel, nj=nj, final=final),
        grid=(n // tm, nj),
        in_specs=in_specs,
        out_specs=pl.BlockSpec((tm, d), lambda i, j: (i, 0)),
        out_shape=jax.ShapeDtypeStruct((n, d), F32),
        scratch_shapes=[pltpu.VMEM((tm, d), BF16), pltpu.VMEM((tm, d), F32)],
        compiler_params=_params(("parallel", "arbitrary")),
    )(*args)


def _proj_kernel(x_ref, g_ref, wa_ref, sa_ref, wb_ref, kvn_ref, oa_ref, ob_ref, oc_ref, xn_ref, *, na, dc_tile):
    j = pl.program_id(1)

    @pl.when(j == 0)
    def _():
        xn_ref[...] = _rms(x_ref[...], g_ref[...]).astype(BF16)

    @pl.when(j < na)
    def _():
        oa_ref[...] = (_dot(xn_ref[...], wa_ref[...]) * sa_ref[...]).astype(oa_ref.dtype)

    @pl.when(j >= na)
    def _():
        r = _dot(xn_ref[...], wb_ref[...])
        ob_ref[...] = r

        @pl.when(j == na + dc_tile)
        def _():
            oc_ref[...] = _rms(r[:, 0:DSA_LATENT], kvn_ref[...]).astype(oc_ref.dtype)


def _proj(x, g, w_bf, bf_scale, w_misc, kv_norm):
    n, d = x.shape
    ca, cb = w_bf.shape[1], w_misc.shape[1]
    tm = _pick_tile(n, 640)
    ta = _pick_tile(ca, 1024, LANE)
    tb = _pick_tile(cb, 512, LANE)
    na, nb = ca // ta, cb // tb
    dc_col = IN_W["rq"] + IN_W["rk"] + IN_W["rg"]
    assert dc_col % tb == 0 and tb >= DSA_LATENT
    a_idx = lambda j: jnp.minimum(j, na - 1)
    b_idx = lambda j: jnp.maximum(j - na, 0)
    return pl.pallas_call(
        functools.partial(_proj_kernel, na=na, dc_tile=dc_col // tb),
        grid=(n // tm, na + nb),
        in_specs=[
            pl.BlockSpec((tm, d), lambda i, j: (i, 0)),
            pl.BlockSpec((1, d), lambda i, j: (0, 0)),
            pl.BlockSpec((d, ta), lambda i, j: (0, a_idx(j))),
            pl.BlockSpec((1, ta), lambda i, j: (0, a_idx(j))),
            pl.BlockSpec((d, tb), lambda i, j: (0, b_idx(j))),
            pl.BlockSpec((1, DSA_LATENT), lambda i, j: (0, 0)),
        ],
        out_specs=[
            pl.BlockSpec((tm, ta), lambda i, j: (i, a_idx(j))),
            pl.BlockSpec((tm, tb), lambda i, j: (i, b_idx(j))),
            pl.BlockSpec((tm, DSA_LATENT), lambda i, j: (i, 0)),
        ],
        out_shape=[jax.ShapeDtypeStruct((n, ca), BF16), jax.ShapeDtypeStruct((n, cb), F32),
                   jax.ShapeDtypeStruct((n, DSA_LATENT), BF16)],
        scratch_shapes=[pltpu.VMEM((tm, d), BF16)],
        compiler_params=_params(("parallel", "arbitrary")),
    )(x, g.reshape(1, d), w_bf, bf_scale.reshape(1, ca), w_misc, kv_norm.reshape(1, DSA_LATENT))


def _split3(x):
    hi = x.astype(BF16)
    r = x - hi.astype(F32)
    mid = r.astype(BF16)
    lo = (r - mid.astype(F32)).astype(BF16)
    return hi, mid, lo


def _cum_kernel(s_ref, b_ref, eq_ref, ek_ref, oneq_ref, onek_ref, aq_ref, ak_ref, carry_ref):
    @pl.when(pl.program_id(1) == 0)
    def _():
        carry_ref[...] = jnp.zeros_like(carry_ref)

    z = s_ref[...] + b_ref[...]
    lf = jnp.minimum(z, 0.0) - jnp.log1p(jnp.exp(-jnp.abs(z)))
    r = lax.broadcasted_iota(jnp.int32, (BLOCK, BLOCK), 0)
    c = lax.broadcasted_iota(jnp.int32, (BLOCK, BLOCK), 1)
    tri = jnp.where(c <= r, 1.0, 0.0).astype(BF16)
    hi, mid, lo = _split3(lf)
    cum = (_dot(tri, lo) + _dot(tri, mid)) + _dot(tri, hi) + carry_ref[...]
    carry_ref[...] = cum[BLOCK - 1:BLOCK, :]
    pieces = jnp.concatenate(_split3(cum * LOG2E), axis=1)
    aq_ref[...] = (_dot(pieces, eq_ref[...]) + oneq_ref[...]).astype(aq_ref.dtype)
    ak_ref[...] = (_dot(pieces, ek_ref[...]) + onek_ref[...]).astype(ak_ref.dtype)


def _fox_bias_columns(misc3, f_bias):
    b, t, _ = misc3.shape
    width = FOX_HEADS * FOX_HEAD_DIM
    blk = (IN_W["rq"] + IN_W["rk"] + IN_W["rg"] + IN_W["dc"]) // LANE
    bias = jnp.zeros((1, LANE), F32).at[0, SMALL_FF:SMALL_FF + FOX_HEADS].set(f_bias)
    eq = np.zeros((3 * LANE, width), np.float32)
    ek = np.zeros((3 * LANE, width), np.float32)
    oneq = np.zeros((1, width), np.float32)
    onek = np.zeros((1, width), np.float32)
    for h in range(FOX_HEADS):
        for piece in range(3):
            eq[piece * LANE + SMALL_FF + h, h * FOX_HEAD_DIM + piece] = 1.0
            ek[piece * LANE + SMALL_FF + h, h * FOX_HEAD_DIM + 3 + piece] = -1.0
            oneq[0, h * FOX_HEAD_DIM + 3 + piece] = 1.0
            onek[0, h * FOX_HEAD_DIM + piece] = 1.0
    const = lambda bi, i: (0, 0)
    out = pl.BlockSpec((None, BLOCK, width), lambda bi, i: (bi, i, 0))
    return pl.pallas_call(
        _cum_kernel,
        grid=(b, t // BLOCK),
        in_specs=[
            pl.BlockSpec((None, BLOCK, LANE), lambda bi, i: (bi, i, blk)),
            pl.BlockSpec((1, LANE), const),
            pl.BlockSpec((3 * LANE, width), const),
            pl.BlockSpec((3 * LANE, width), const),
            pl.BlockSpec((1, width), const),
            pl.BlockSpec((1, width), const),
        ],
        out_specs=[out, out],
        out_shape=[jax.ShapeDtypeStruct((b, t, width), BF16)] * 2,
        scratch_shapes=[pltpu.VMEM((1, LANE), F32)],
        compiler_params=_params(("parallel", "arbitrary")),
    )(misc3, bias, jnp.asarray(eq, BF16), jnp.asarray(ek, BF16), jnp.asarray(oneq), jnp.asarray(onek))


def _bit_transpose32(words):
    a = list(words)
    j, m = 16, 0x0000FFFF
    while j:
        mask = jnp.int32(m if m < 2 ** 31 else m - 2 ** 32)
        for k in range(32):
            if not k & j:
                t = (lax.shift_right_logical(a[k], jnp.int32(j)) ^ a[k + j]) & mask
                a[k + j] = a[k + j] ^ t
                a[k] = a[k] ^ (t << j)
        j >>= 1
        m = (m ^ (m << j)) & 0xFFFFFFFF
    return a


FAR_BLOCKS = 4


def _dsa_kernel(dq_ref, iq_ref, sm_ref, ik_ref, c_ref, ct_ref, wuk_ref, wuv_ref, bias_ref,
                o_ref, s_ref, qlat_ref, p_ref, m_ref, acc_ref, lg0_ref, lg1_ref, lgn_ref, iqt_ref, wt_ref,
                planes_ref, alive_ref, p1_ref, *, topk, nblk):
    i = pl.program_id(1)
    key_i = lax.broadcasted_iota(jnp.int32, (BLOCK, BLOCK), 0)
    qry_i = lax.broadcasted_iota(jnp.int32, (BLOCK, BLOCK), 1)
    hs = lambda h: slice(h * BLOCK, (h + 1) * BLOCK)

    iq_t = jnp.transpose(iq_ref[...].astype(F32)).astype(BF16)
    for h in range(IDX_HEADS):
        iqt_ref[0:IDX_DIM, hs(h)] = iq_t[h * IDX_DIM:(h + 1) * IDX_DIM, :]
    iqt_ref[IDX_DIM:LANE, :] = jnp.zeros((LANE - IDX_DIM, IDX_HEADS * BLOCK), BF16)
    wt_ref[...] = jnp.transpose(sm_ref[...])[SMALL_IW:SMALL_IW + IDX_HEADS, :] * (IDX_HEADS ** -0.5 * IDX_DIM ** -0.5)

    for h in range(DSA_HEADS):
        ql = _dot_nt(wuk_ref[h], dq_ref[:, h * DSA_HEAD_DIM:(h + 1) * DSA_HEAD_DIM]) * (DSA_HEAD_DIM ** -0.5 * LOG2E)
        qlat_ref[:, hs(h)] = ql.astype(BF16)

    ngrp = (i + SCORE_GROUP) // SCORE_GROUP
    npair = (ngrp + 1) // 2
    for u in range(1, 2 * SCORE_GROUP):
        s_ref[i + u] = jnp.full((BLOCK, BLOCK), INT_MIN, jnp.int32)

    def score_group(g, _):
        j0 = jnp.minimum(g * SCORE_GROUP, nblk - SCORE_GROUP)
        kt = ik_ref[pl.ds(pl.multiple_of(j0 * BLOCK, BLOCK), SCORE_GROUP * BLOCK), :]
        r = _dot(kt, iqt_ref[...])
        t = i * BLOCK + qry_i
        for u in range(SCORE_GROUP):
            acc = jnp.zeros((BLOCK, BLOCK), F32)
            for h in range(IDX_HEADS):
                acc = acc + jnp.maximum(r[u * BLOCK:(u + 1) * BLOCK, hs(h)], 0.0) * wt_ref[h:h + 1, :]
            s = (j0 + u) * BLOCK + key_i
            ok = (s <= t) & ((s >= PAD) | (s == t))
            bits = lax.bitcast_convert_type(acc, jnp.int32)
            bits = jnp.where(bits == INT_MIN, 0, bits)
            key = bits ^ ((bits >> 31) & 0x7FFFFFFF)
            s_ref[j0 + u] = jnp.where(ok, key, INT_MIN)
        return 0

    lax.fori_loop(0, ngrp, score_group, 0)

    def slice_group(g, _):
        for half in range(2):
            words = []
            for u in (2 * half, 2 * half + 1):
                tile = s_ref[g * SCORE_GROUP + u] ^ INT_MIN
                words += [tile[r * 8:(r + 1) * 8, :] for r in range(BLOCK // 8)]
            planes = _bit_transpose32(words)
            for bit in range(32):
                planes_ref[2 * g + half, bit] = planes[bit]
            planes_ref[2 * g + half, 32] = jnp.full((8, BLOCK), -1, jnp.int32)
        return 0

    lax.fori_loop(0, 2 * npair, slice_group, 0)
    alive_ref[...] = jnp.full(alive_ref.shape, -1, jnp.int32)

    def radix_pass(t, carry):
        thr, need, drop_prev = carry
        bit = 31 - t

        def body(pr, accs):
            accs = list(accs)
            for u in range(4):
                st = 4 * pr + u
                alive = alive_ref[st] & (planes_ref[st, bit + 1] ^ drop_prev)
                alive_ref[st] = alive
                accs[u] = accs[u] + lax.population_count(alive & planes_ref[st, bit])
            return tuple(accs)

        zero = jnp.zeros((8, BLOCK), jnp.int32)
        accs = lax.fori_loop(0, npair, body, (zero,) * 4)
        cnt = ((accs[0] + accs[1]) + (accs[2] + accs[3])).astype(F32)
        cnt = jnp.sum(cnt, axis=0, keepdims=True).astype(jnp.int32)
        take = cnt >= need
        thr = thr | jnp.where(take, jnp.int32(1) << bit, 0)
        need = jnp.where(take, need, need - cnt)
        return thr, need, jnp.where(take, 0, -1)

    zero = jnp.zeros((1, BLOCK), jnp.int32)
    thr, _, _ = lax.fori_loop(0, 32, radix_pass, (zero, zero + topk, zero))
    thr = jnp.maximum(thr ^ INT_MIN, INT_MIN + 1)

    m_ref[...] = jnp.full(m_ref.shape, NEG, F32)
    acc_ref[...] = jnp.zeros_like(acc_ref)

    def logits(j0, nb, dst_ref):
        w = nb * BLOCK
        dst_ref[0:w, :] = _dot(c_ref[pl.ds(pl.multiple_of(j0 * BLOCK, BLOCK), w), :], qlat_ref[...])

    def softmax_pv(j0, nb, lg_ref, pb_ref, bias_off=None):
        w = nb * BLOCK
        ckt = ct_ref[:, pl.ds(pl.multiple_of(j0 * BLOCK, BLOCK), w)]
        masked = jnp.where(s_ref[pl.ds(j0, nb)].reshape(w, BLOCK) >= thr, 0.0, NEG)
        alphas = []
        for h in range(DSA_HEADS):
            x = lg_ref[0:w, hs(h)] + masked
            if bias_off is not None:
                x = x + bias_ref[h, pl.ds(bias_off, w), :]
            m_prev = m_ref[h:h + 1, :]
            m_new = jnp.maximum(m_prev, jnp.max(x, axis=0, keepdims=True))
            alphas.append(jnp.exp2(m_prev - m_new))
            m_ref[h:h + 1, :] = m_new
            pb_ref[0:w, hs(h)] = jnp.exp2(x - m_new).astype(BF16)
        pv = _dot(ckt, pb_ref[0:w, :])
        for h in range(DSA_HEADS):
            acc_ref[:, hs(h)] = acc_ref[:, hs(h)] * alphas[h] + pv[:, hs(h)]

    near0 = jnp.maximum(i - 1, 0)
    logits(near0, 2, lgn_ref)

    n_far = jnp.maximum(i - 1, 0)
    n_big = n_far // FAR_BLOCKS
    big = lambda g: g * FAR_BLOCKS

    @pl.when(n_big > 0)
    def _():
        logits(0, FAR_BLOCKS, lg0_ref)

        def pair_of_tiles(g, _):
            logits(big(2 * g + 1), FAR_BLOCKS, lg1_ref)
            softmax_pv(big(2 * g), FAR_BLOCKS, lg0_ref, p_ref)
            logits(big(jnp.minimum(2 * g + 2, n_big - 1)), FAR_BLOCKS, lg0_ref)
            softmax_pv(big(2 * g + 1), FAR_BLOCKS, lg1_ref, p1_ref)
            return 0

        lax.fori_loop(0, n_big // 2, pair_of_tiles, 0)

        @pl.when(n_big % 2 == 1)
        def _():
            softmax_pv(big(n_big - 1), FAR_BLOCKS, lg0_ref, p_ref)

    def far_one(j, _):
        logits(j, 1, lg0_ref)
        softmax_pv(j, 1, lg0_ref, p_ref)
        return 0

    lax.fori_loop(n_big * FAR_BLOCKS, n_far, far_one, 0)
    softmax_pv(near0, 2, lgn_ref, p1_ref, bias_off=pl.multiple_of(jnp.where(i >= 1, 0, BLOCK), BLOCK))

    for h in range(DSA_HEADS):
        inv_l = 1.0 / acc_ref[DSA_LATENT:DSA_LATENT + 1, hs(h)]
        o_lat = (acc_ref[0:DSA_LATENT, hs(h)] * inv_l).astype(BF16)
        o = _dot_tn(o_lat, wuv_ref[h])
        o_ref[:, h * DSA_HEAD_DIM:(h + 1) * DSA_HEAD_DIM] = o.astype(o_ref.dtype)


def _dsa(pbf3, misc3, c3, ct3, wuk, wuv, bias, topk):
    b, t, _ = pbf3.shape
    nblk = t // BLOCK
    hq = DSA_HEADS * DSA_HEAD_DIM
    ik_blk = 6 * 1024 // LANE
    small_blk = (IN_W["rq"] + IN_W["rk"] + IN_W["rg"] + IN_W["dc"]) // LANE
    n_sets = 4 * (((nblk + SCORE_GROUP - 1) // SCORE_GROUP + 1) // 2)
    n_tiles = nblk + 2 * SCORE_GROUP - 1
    const3 = lambda bi, i: (0, 0, 0)
    return pl.pallas_call(
        functools.partial(_dsa_kernel, topk=topk, nblk=nblk),
        grid=(b, nblk),
        in_specs=[
            pl.BlockSpec((None, BLOCK, hq), lambda bi, i: (bi, i, 0)),
            pl.BlockSpec((None, BLOCK, IDX_HEADS * IDX_DIM), lambda bi, i: (bi, i, 1)),
            pl.BlockSpec((None, BLOCK, LANE), lambda bi, i: (bi, i, small_blk)),
            pl.BlockSpec((None, t, LANE), lambda bi, i: (bi, 0, ik_blk)),
            pl.BlockSpec((None, t, DSA_LATENT), lambda bi, i: (bi, 0, 0)),
            pl.BlockSpec((None, DSA_LATENT + 8, t), lambda bi, i: (bi, 0, 0)),
            pl.BlockSpec((DSA_HEADS, DSA_LATENT, DSA_HEAD_DIM), const3),
            pl.BlockSpec((DSA_HEADS, DSA_LATENT, DSA_HEAD_DIM), const3),
            pl.BlockSpec((DSA_HEADS, 3 * BLOCK, BLOCK), const3),
        ],
        out_specs=pl.BlockSpec((None, BLOCK, hq), lambda bi, i: (bi, i, 0)),
        out_shape=jax.ShapeDtypeStruct((b, t, hq), BF16),
        scratch_shapes=[
            pltpu.VMEM((n_tiles, BLOCK, BLOCK), jnp.int32),
            pltpu.VMEM((DSA_LATENT, DSA_HEADS * BLOCK), BF16),
            pltpu.VMEM((FAR_BLOCKS * BLOCK, DSA_HEADS * BLOCK), BF16),
            pltpu.VMEM((DSA_HEADS, BLOCK), F32),
            pltpu.VMEM((DSA_LATENT + 8, DSA_HEADS * BLOCK), F32),
            pltpu.VMEM((FAR_BLOCKS * BLOCK, DSA_HEADS * BLOCK), F32),
            pltpu.VMEM((FAR_BLOCKS * BLOCK, DSA_HEADS * BLOCK), F32),
            pltpu.VMEM((2 * BLOCK, DSA_HEADS * BLOCK), F32),
            pltpu.VMEM((LANE, IDX_HEADS * BLOCK), BF16),
            pltpu.VMEM((IDX_HEADS, BLOCK), F32),
            pltpu.VMEM((n_sets, 33, 8, BLOCK), jnp.int32),
            pltpu.VMEM((n_sets, 8, BLOCK), jnp.int32),
            pltpu.VMEM((FAR_BLOCKS * BLOCK, DSA_HEADS * BLOCK), BF16),
        ],
        compiler_params=_params(("parallel", "arbitrary")),
    )(pbf3, pbf3, misc3, pbf3, c3, ct3, wuk, wuv, bias)


def _dsa_weights(w_uk, w_uv):
    return jnp.transpose(w_uk, (1, 0, 2)).astype(BF16), jnp.transpose(w_uv, (1, 0, 2)).astype(BF16)


FOX_ROWS = 64


def _fox_kernel(q_ref, qa_ref, k_ref, ka_ref, v_ref, o_ref, m_ref, l_ref, acc_ref, lg_ref, p_ref, *, tq):
    i = pl.program_id(2)
    qq = jnp.concatenate([q_ref[...], qa_ref[...]], axis=1)
    m_ref[...] = jnp.full(m_ref.shape, NEG, F32)
    l_ref[...] = jnp.zeros_like(l_ref)
    acc_ref[...] = jnp.zeros_like(acc_ref)
    rows = min(FOX_ROWS, tq)
    chunks = [slice(c, c + rows) for c in range(0, tq, rows)]

    def qk(j, slot):
        off = pl.multiple_of(j * tq, tq)
        kk = jnp.concatenate([k_ref[pl.ds(off, tq), :], ka_ref[pl.ds(off, tq), :]], axis=1)
        lg_ref[slot] = _dot_nt(kk, qq)

    def softmax_pv(j, slot, mode):
        def logits(rs):
            x = lg_ref[slot, rs, :]
            if mode == "none":
                return x
            s = rs.start + lax.broadcasted_iota(jnp.int32, (rows, 1), 0)
            if mode == "pad":
                return jnp.where(s >= PAD, x, NEG)
            t = lax.broadcasted_iota(jnp.int32, (1, tq), 1)
            return jnp.where((s <= t) & ((s + j * tq >= PAD) | (s == t)), x, NEG)

        fold = lambda a: a.reshape(rows // 8, 8, tq)
        mx = jnp.full((8, tq), NEG, F32)
        for rs in chunks:
            mx = jnp.maximum(mx, jnp.max(fold(logits(rs)), axis=0))
        m_prev = m_ref[...]
        m_new = jnp.maximum(m_prev, jnp.max(mx, axis=0, keepdims=True))
        alpha = jnp.exp2(m_prev - m_new)
        tot = jnp.zeros((8, tq), F32)
        for rs in chunks:
            p = jnp.exp2(logits(rs) - m_new)
            tot = tot + jnp.sum(fold(p), axis=0)
            p_ref[rs, :] = p.astype(BF16)
        l_ref[...] = alpha * l_ref[...] + jnp.sum(tot, axis=0, keepdims=True)
        off = pl.multiple_of(j * tq, tq)
        acc_ref[...] = acc_ref[...] * alpha + _dot_tn(v_ref[pl.ds(off, tq), :], p_ref[...])
        m_ref[...] = m_new

    qk(i, 2)

    @pl.when(i > 0)
    def _():
        qk(0, 0)
        qk(jnp.minimum(1, i), 1)
        softmax_pv(0, 0, "pad")
        n_mid = i - 1

        def pair(g, _):
            a = 2 * g + 1
            qk(jnp.minimum(a + 1, i), 0)
            softmax_pv(a, 1, "none")
            qk(jnp.minimum(a + 2, i), 1)
            softmax_pv(a + 1, 0, "none")
            return 0

        lax.fori_loop(0, n_mid // 2, pair, 0)

        @pl.when(n_mid % 2 == 1)
        def _():
            softmax_pv(i - 1, 1, "none")

    softmax_pv(i, 2, "diag")

    o_ref[...] = jnp.transpose(acc_ref[...] * (1.0 / l_ref[...])).astype(o_ref.dtype)


def _fox(pbf3, aug_q, aug_k):
    b, t, _ = pbf3.shape
    nblk = t // BLOCK
    tq = BLOCK * max(k for k in (1, 5) if nblk % k == 0)
    d = FOX_HEAD_DIM
    qb, kb, vb = 2 * 1024 // d, 3 * 1024 // d, 4 * 1024 // d
    return pl.pallas_call(
        functools.partial(_fox_kernel, tq=tq),
        grid=(b, FOX_HEADS, t // tq),
        in_specs=[
            pl.BlockSpec((None, tq, d), lambda bi, h, i: (bi, i, qb + h)),
            pl.BlockSpec((None, tq, d), lambda bi, h, i: (bi, i, h)),
            pl.BlockSpec((None, t, d), lambda bi, h, i: (bi, 0, kb + h)),
            pl.BlockSpec((None, t, d), lambda bi, h, i: (bi, 0, h)),
            pl.BlockSpec((None, t, d), lambda bi, h, i: (bi, 0, vb + h)),
        ],
        out_specs=pl.BlockSpec((None, tq, d), lambda bi, h, i: (bi, i, h)),
        out_shape=jax.ShapeDtypeStruct((b, t, FOX_HEADS * d), BF16),
        scratch_shapes=[pltpu.VMEM((1, tq), F32), pltpu.VMEM((1, tq), F32), pltpu.VMEM((d, tq), F32),
                        pltpu.VMEM((3, tq, tq), F32), pltpu.VMEM((tq, tq), BF16)],
        compiler_params=_params(("parallel", "parallel", "arbitrary")),
    )(pbf3, aug_q, pbf3, aug_k, pbf3)


def _ret_kernel(rq_ref, rk_ref, rv_ref, rg_ref, cos_ref, sin_ref, dq_ref, dk_ref, decay_ref, cd_ref, gn_ref,
                o_ref, state_ref):
    i = pl.program_id(1)

    @pl.when(i == 0)
    def _():
        state_ref[...] = jnp.zeros_like(state_ref)

    half = RET_HEADS * RET_QK_DIM // 2
    cos = cos_ref[...]
    sin = sin_ref[...]

    def rope(x):
        x1, x2 = x[:, :half], x[:, half:]
        return jnp.concatenate([x1 * cos - x2 * sin, x1 * sin + x2 * cos], axis=1)

    q = rope(rq_ref[...])
    k = rope(rk_ref[...]) * RET_QK_DIM ** -0.5
    pos = i * BLOCK + lax.broadcasted_iota(jnp.int32, (BLOCK, 1), 0)
    k = jnp.where(pos >= PAD, k, 0.0)
    qb = q.astype(BF16)
    kb = k.astype(BF16)
    q_dec = (q * dq_ref[...]).astype(BF16)
    k_dec = (k * dk_ref[...]).astype(BF16)
    head_of_col = (lax.broadcasted_iota(jnp.int32, (1, 2 * half), 1) % half) // (RET_QK_DIM // 2)
    zero = jnp.zeros((), BF16)
    for h in range(RET_HEADS):
        mine = head_of_col == h
        vh = rv_ref[:, h * RET_V_DIM:(h + 1) * RET_V_DIM]
        s = (_dot_nt(jnp.where(mine, qb, zero), kb) * decay_ref[h]).astype(BF16)
        st = state_ref[h]
        o = _dot(s, vh) + _dot(jnp.where(mine, q_dec, zero), st.astype(BF16))
        state_ref[h] = cd_ref[h] * st + _dot_tn(k_dec, vh)
        mu = jnp.mean(o, axis=1, keepdims=True)
        var = jnp.mean(jnp.square(o - mu), axis=1, keepdims=True)
        sl = slice(h * RET_V_DIM, (h + 1) * RET_V_DIM)
        o = (o - mu) * lax.rsqrt(var + GN_EPS) * gn_ref[:, sl]
        g = rg_ref[:, sl]
        o_ref[:, sl] = (g * jax.nn.sigmoid(g) * o).astype(o_ref.dtype)


def _retention(misc3, pbf3, cos, sin, dq_tab, dk_tab, decay, cd, gn_w):
    b, t, _ = misc3.shape
    qk = RET_HEADS * RET_QK_DIM
    vw = RET_HEADS * RET_V_DIM
    const2 = lambda bi, i: (0, 0)
    const3 = lambda bi, i: (0, 0, 0)
    return pl.pallas_call(
        _ret_kernel,
        grid=(b, t // BLOCK),
        in_specs=[
            pl.BlockSpec((None, BLOCK, qk), lambda bi, i: (bi, i, 0)),
            pl.BlockSpec((None, BLOCK, qk), lambda bi, i: (bi, i, 1)),
            pl.BlockSpec((None, BLOCK, vw), lambda bi, i: (bi, i, 5)),
            pl.BlockSpec((None, BLOCK, vw), lambda bi, i: (bi, i, 1)),
            pl.BlockSpec((BLOCK, qk // 2), lambda bi, i: (i, 0)),
            pl.BlockSpec((BLOCK, qk // 2), lambda bi, i: (i, 0)),
            pl.BlockSpec((BLOCK, qk), const2),
            pl.BlockSpec((BLOCK, qk), const2),
            pl.BlockSpec((RET_HEADS, BLOCK, BLOCK), const3),
            pl.BlockSpec((RET_HEADS, 1, RET_V_DIM), const3),
            pl.BlockSpec((1, vw), const2),
        ],
        out_specs=pl.BlockSpec((None, BLOCK, vw), lambda bi, i: (bi, i, 0)),
        out_shape=jax.ShapeDtypeStruct((b, t, vw), BF16),
        scratch_shapes=[pltpu.VMEM((RET_HEADS, qk, RET_V_DIM), F32)],
        compiler_params=_params(("parallel", "arbitrary")),
    )(misc3, misc3, pbf3, misc3, cos, sin, dq_tab, dk_tab, decay, cd, gn_w.reshape(1, vw))


def _merge_kernel(x_ref, g_ref, b0_ref, b1_ref, b2_ref, wg0_ref, wg1_ref, wg2_ref, wb_ref, o_ref, xn_ref):
    @pl.when(pl.program_id(1) == 0)
    def _():
        xn_ref[...] = _rms(x_ref[...], g_ref[...]).astype(BF16)

    xn = xn_ref[...]
    acc = jax.nn.sigmoid(_dot(xn, wg0_ref[...])) * _dot(b0_ref[...], wb_ref[0])
    acc = acc + jax.nn.sigmoid(_dot(xn, wg1_ref[...])) * _dot(b1_ref[...], wb_ref[1])
    acc = acc + jax.nn.sigmoid(_dot(xn, wg2_ref[...])) * _dot(b2_ref[...], wb_ref[2])
    o_ref[...] = acc.astype(o_ref.dtype)


def _merge(x, g, b0, b1, b2, w_gates, w_branch):
    n, bw = b0.shape
    dm = x.shape[1]
    d = w_branch.shape[2]
    tm = _pick_tile(n, 640)
    tn = _pick_tile(d, 512, LANE)
    nj = d // tn
    bspec = pl.BlockSpec((tm, bw), lambda i, j: (i, 0))
    gspec = lambda r: pl.BlockSpec((dm, tn), lambda i, j: (0, r * nj + j))
    return pl.pallas_call(
        _merge_kernel,
        grid=(n // tm, nj),
        in_specs=[pl.BlockSpec((tm, dm), lambda i, j: (i, 0)), pl.BlockSpec((1, dm), lambda i, j: (0, 0)),
                  bspec, bspec, bspec, gspec(0), gspec(1), gspec(2),
                  pl.BlockSpec((N_BRANCH, bw, tn), lambda i, j: (0, 0, j))],
        out_specs=pl.BlockSpec((tm, tn), lambda i, j: (i, j)),
        out_shape=jax.ShapeDtypeStruct((n, d), BF16),
        scratch_shapes=[pltpu.VMEM((tm, dm), BF16)],
        compiler_params=_params(("parallel", "arbitrary")),
    )(x, g.reshape(1, dm), b0, b1, b2, w_gates, w_gates, w_gates, w_branch)


def _resproj_kernel(a_ref, w_ref, r_ref, o_ref):
    o_ref[...] = r_ref[...] + _dot(a_ref[...], w_ref[...])


def _resproj(a, w, res):
    n, k = a.shape
    d = w.shape[1]
    tm = _pick_tile(n, 640)
    tn = _pick_tile(d, 512, LANE)
    return pl.pallas_call(
        _resproj_kernel,
        grid=(n // tm, d // tn),
        in_specs=[
            pl.BlockSpec((tm, k), lambda i, j: (i, 0)),
            pl.BlockSpec((k, tn), lambda i, j: (0, j)),
            pl.BlockSpec((tm, tn), lambda i, j: (i, j)),
        ],
        out_specs=pl.BlockSpec((tm, tn), lambda i, j: (i, j)),
        out_shape=jax.ShapeDtypeStruct((n, d), F32),
        compiler_params=_params(("parallel", "arbitrary")),
    )(a, w, res)


def _t5_bucket(dist):
    d = jnp.maximum(dist, 1).astype(F32)
    large = T5_MAX_EXACT + (jnp.log(d / T5_MAX_EXACT) / math.log(T5_MAX_DIST / T5_MAX_EXACT)
                            * (T5_BUCKETS - T5_MAX_EXACT)).astype(jnp.int32)
    large = jnp.minimum(large, T5_BUCKETS - 1)
    return jnp.where(dist < T5_MAX_EXACT, dist, large)


def _t5_bias_tiles(t5_table):
    k = jnp.arange(BLOCK, dtype=jnp.int32)[:, None]
    q = jnp.arange(BLOCK, dtype=jnp.int32)[None, :]
    table = t5_table.astype(F32)
    rel = table - table[T5_BUCKETS - 1]

    def lookup(dist):
        onehot = (_t5_bucket(dist)[..., None] == jnp.arange(T5_BUCKETS, dtype=jnp.int32)).astype(F32)
        return jnp.einsum("kqb,bh->hkq", onehot, rel, precision=lax.Precision.HIGHEST)

    bdiag = lookup(jnp.maximum(q - k, 0))
    bprev = lookup(BLOCK + q - k)
    return jnp.concatenate([bprev, bdiag, jnp.zeros_like(bdiag)], axis=1) * LOG2E


def _rope_tables(t):
    half = RET_QK_DIM // 2
    inv = ROPE_BASE ** (-jnp.arange(half, dtype=F32) / half)
    ang = jnp.arange(t, dtype=jnp.int32).astype(F32)[:, None] * inv[None, :]
    return jnp.tile(jnp.cos(ang), (1, RET_HEADS)), jnp.tile(jnp.sin(ang), (1, RET_HEADS))


def _decay_tables():
    log_gamma = jnp.log1p(-(2.0 ** (-5.0 - jnp.arange(RET_HEADS, dtype=F32))))
    i = jnp.arange(BLOCK, dtype=F32)
    gap = i[:, None] - i[None, :]
    decay = jnp.where(gap[None] >= 0, jnp.exp(jnp.maximum(gap, 0.0)[None] * log_gamma[:, None, None]), 0.0)
    head_of_col = (np.arange(RET_HEADS * RET_QK_DIM) % (RET_HEADS * RET_QK_DIM // 2)) // (RET_QK_DIM // 2)
    lg_col = log_gamma[head_of_col]
    dk_tab = jnp.exp((BLOCK - 1 - i)[:, None] * lg_col[None, :])
    dq_tab = jnp.exp((i + 1.0)[:, None] * lg_col[None, :])
    cd = jnp.broadcast_to(jnp.exp(BLOCK * log_gamma)[:, None, None], (RET_HEADS, 1, RET_V_DIM))
    return decay.astype(F32), dq_tab, dk_tab, cd


def _split_w_in(w16):
    cols = lambda name: w16[:, IN_OFF[name]:IN_OFF[name] + IN_W[name]]
    zeros = lambda w: jnp.zeros((w16.shape[0], w), BF16)
    rope = lambda a: jnp.swapaxes(a.reshape(-1, RET_HEADS, 2, RET_QK_DIM // 2), 1, 2).reshape(a.shape)
    w_bf = jnp.concatenate([cols("dq"), cols("iq"), cols("fq"), cols("fk"), cols("fv"), cols("rv"),
                            cols("ik"), zeros(LANE - IDX_DIM)], axis=1)
    w_misc = jnp.concatenate([rope(cols("rq")), rope(cols("rk")), cols("rg"), cols("dc"),
                              cols("iw"), cols("ff"), zeros(LANE - IDX_HEADS - FOX_HEADS),
                              zeros(LANE)], axis=1)
    assert w_bf.shape[1] == BF_COLS and w_misc.shape[1] == MISC_COLS
    return w_bf, w_misc, cols("gates")


def _bf_col_scale():
    scale = np.ones((BF_COLS,), np.float32)
    scale[2 * 1024:3 * 1024] = FOX_HEAD_DIM ** -0.5 * LOG2E
    return jnp.asarray(scale)


def _mixer(x, b, t, mix_norm, w_in16, kv_norm, w_uk, w_uv, f_bias, gn_w, w_branch16, w_out16, tables, topk):
    n = b * t
    bias_tiles, rope_tabs, decay_tabs = tables
    w_bf, w_misc, w_gates = _split_w_in(w_in16)
    pbf, misc, c = _proj(x, mix_norm, w_bf, _bf_col_scale(), w_misc, kv_norm)
    pbf3 = pbf.reshape(b, t, BF_COLS)
    misc3 = misc.reshape(b, t, MISC_COLS)

    c3 = c.reshape(b, t, DSA_LATENT)
    wuk, wuv = _dsa_weights(w_uk, w_uv)
    ones_rows = jnp.zeros((b, 8, t), BF16).at[:, 0].set(1)
    ct3 = jnp.concatenate([jnp.swapaxes(c3, 1, 2), ones_rows], axis=1)
    dsa_o = _dsa(pbf3, misc3, c3, ct3, wuk, wuv, bias_tiles, topk)

    fox_o = _fox(pbf3, *_fox_bias_columns(misc3, f_bias))

    ret_o = _retention(misc3, pbf3, *rope_tabs, *decay_tabs, gn_w)

    merged = _merge(x, mix_norm, dsa_o.reshape(n, -1), fox_o.reshape(n, -1), ret_o.reshape(n, -1), w_gates,
                    w_branch16)
    return _resproj(merged, w_out16, x)


def kernel(x, meta_tokens, t5_table, ffn1_norm, ffn1_w1, ffn1_w3, ffn1_w2, mix_norm, w_in, dsa_kv_norm,
           dsa_w_uk, dsa_w_uv, fox_f_bias, ret_gn_w, w_branch, w_out, ffn2_norm, ffn2_w1, ffn2_w3, ffn2_w2,
           final_norm):
    b, seq, d = x.shape
    assert d == D_MODEL and seq % BLOCK == 0
    t = seq + BLOCK
    depth = ffn1_norm.shape[0]
    topk = min(INDEX_TOPK, seq // 4)
    h = jnp.concatenate([
        jnp.zeros((b, PAD, d), x.dtype),
        jnp.broadcast_to(meta_tokens.astype(x.dtype)[None], (b, N_META, d)),
        x,
    ], axis=1).reshape(b * t, d)
    cos, sin = _rope_tables(t)
    decay, dq_tab, dk_tab, cd = _decay_tables()
    tables = (_t5_bias_tiles(t5_table), (cos, sin), (dq_tab, dk_tab, decay, cd))
    bf = lambda w, l: w[l].astype(BF16)
    for l in range(depth):
        h = _ffn(h, ffn1_norm[l], bf(ffn1_w1, l), bf(ffn1_w3, l), bf(ffn1_w2, l))
        h = _mixer(h, b, t, mix_norm[l], bf(w_in, l), dsa_kv_norm[l], dsa_w_uk[l], dsa_w_uv[l],
                   fox_f_bias[l], ret_gn_w[l], bf(w_branch, l), bf(w_out, l), tables, topk)
        h = _ffn(h, ffn2_norm[l], bf(ffn2_w1, l), bf(ffn2_w3, l), bf(ffn2_w2, l),
                 final_g=final_norm if l == depth - 1 else None)
    return h.reshape(b, t, d)[:, BLOCK:]
```

```python
import functools
import math

import numpy as np
import jax
import jax.numpy as jnp
from jax import lax
from jax.experimental import pallas as pl
from jax.experimental.pallas import tpu as pltpu

F32 = jnp.float32
BF16 = jnp.bfloat16

LANE = 128
VMEM_LIMIT = 56 * 1024 * 1024

D_MODEL = 2048
N_META = 16
BLOCK = 128
PAD = BLOCK - N_META
RMS_EPS = 1e-6
GN_EPS = 1e-5
NEG = -1e30

DSA_HEADS = 8
DSA_HEAD_DIM = 128
DSA_LATENT = 256
IDX_HEADS = 16
IDX_DIM = 64
INDEX_TOPK = 256
FOX_HEADS = 8
FOX_HEAD_DIM = 128
RET_HEADS = 8
RET_QK_DIM = 64
RET_V_DIM = 128
ROPE_BASE = 10000.0
N_BRANCH = 3
T5_BUCKETS = 32
T5_MAX_EXACT = 16
T5_MAX_DIST = 128

IN_WIDTHS = (1024, 256, 1024, 64, 16, 1024, 1024, 1024, 8, 512, 512, 1024, 1024, 6144)
IN_NAMES = ("dq", "dc", "iq", "ik", "iw", "fq", "fk", "fv", "ff", "rq", "rk", "rv", "rg", "gates")
IN_OFF = dict(zip(IN_NAMES, np.cumsum((0,) + IN_WIDTHS[:-1]).tolist()))
IN_W = dict(zip(IN_NAMES, IN_WIDTHS))

INT_MIN = -2 ** 31
LOG2E = math.log2(math.e)
SCORE_GROUP = 4

BF_COLS = 6 * 1024 + LANE
MISC_COLS = 2560
SMALL_IW = 0
SMALL_FF = IDX_HEADS


def _dot(a, b):
    return jnp.dot(a, b, preferred_element_type=F32)


def _dot_nt(a, b):
    return lax.dot_general(a, b, (((1,), (1,)), ((), ())), preferred_element_type=F32)


def _dot_tn(a, b):
    return lax.dot_general(a, b, (((0,), (0,)), ((), ())), preferred_element_type=F32)


def _pick_tile(n, target, mult=8):
    best = None
    for t in range(mult, min(n, target) + 1, mult):
        if n % t == 0:
            best = t
    assert best is not None, (n, target, mult)
    return best


def _params(sem):
    return pltpu.CompilerParams(dimension_semantics=sem, vmem_limit_bytes=VMEM_LIMIT)


def _rms(x, g):
    return x * lax.rsqrt(jnp.mean(x * x, axis=-1, keepdims=True) + RMS_EPS) * g


def _ffn_kernel(*refs, nj, final):
    if final:
        x_ref, g_ref, w1_ref, w3_ref, w2_ref, fg_ref, o_ref, xn_ref, acc_ref = refs
    else:
        x_ref, g_ref, w1_ref, w3_ref, w2_ref, o_ref, xn_ref, acc_ref = refs
    j = pl.program_id(1)

    @pl.when(j == 0)
    def _():
        xn_ref[...] = _rms(x_ref[...], g_ref[...]).astype(BF16)
        acc_ref[...] = jnp.zeros_like(acc_ref)

    xn = xn_ref[...]
    h1 = _dot(xn, w1_ref[...])
    h3 = _dot(xn, w3_ref[...])
    g = (h1 * jax.nn.sigmoid(h1) * h3).astype(BF16)
    acc_ref[...] += _dot(g, w2_ref[...])

    @pl.when(j == nj - 1)
    def _():
        y = x_ref[...] + 0.5 * acc_ref[...]
        if final:
            y = _rms(y, fg_ref[...])
        o_ref[...] = y


def _ffn(x, g, w1, w3, w2, final_g=None):
    n, d = x.shape
    f = w1.shape[1]
    tm = _pick_tile(n, 640)
    tf = _pick_tile(f, 512, LANE)
    nj = f // tf
    final = final_g is not None
    in_specs = [
        pl.BlockSpec((tm, d), lambda i, j: (i, 0)),
        pl.BlockSpec((1, d), lambda i, j: (0, 0)),
        pl.BlockSpec((d, tf), lambda i, j: (0, j)),
        pl.BlockSpec((d, tf), lambda i, j: (0, j)),
        pl.BlockSpec((tf, d), lambda i, j: (j, 0)),
    ]
    args = [x, g.reshape(1, d), w1, w3, w2]
    if final:
        in_specs.append(pl.BlockSpec((1, d), lambda i, j: (0, 0)))
        args.append(final_g.reshape(1, d))
    return pl.pallas_call(
        functools.partial(_ffn_kernel, nj=nj, final=final),
        grid=(n // tm, nj),
        in_specs=in_specs,
        out_specs=pl.BlockSpec((tm, d), lambda i, j: (i, 0)),
        out_shape=jax.ShapeDtypeStruct((n, d), F32),
        scratch_shapes=[pltpu.VMEM((tm, d), BF16), pltpu.VMEM((tm, d), F32)],
        compiler_params=_params(("parallel", "arbitrary")),
    )(*args)


def _proj_kernel(x_ref, g_ref, wa_ref, sa_ref, wb_ref, kvn_ref, oa_ref, ob_ref, oc_ref, xn_ref, *, na, dc_tile):
    j = pl.program_id(1)

    @pl.when(j == 0)
    def _():
        xn_ref[...] = _rms(x_ref[...], g_ref[...]).astype(BF16)

    @pl.when(j < na)
    def _():
        oa_ref[...] = (_dot(xn_ref[...], wa_ref[...]) * sa_ref[...]).astype(oa_ref.dtype)

    @pl.when(j >= na)
    def _():
        r = _dot(xn_ref[...], wb_ref[...])
        ob_ref[...] = r

        @pl.when(j == na + dc_tile)
        def _():
            oc_ref[...] = _rms(r[:, 0:DSA_LATENT], kvn_ref[...]).astype(oc_ref.dtype)


def _proj(x, g, w_bf, bf_scale, w_misc, kv_norm):
    n, d = x.shape
    ca, cb = w_bf.shape[1], w_misc.shape[1]
    tm = _pick_tile(n, 1280)
    ta = _pick_tile(ca, 1024, LANE)
    tb = _pick_tile(cb, 512, LANE)
    na, nb = ca // ta, cb // tb
    dc_col = IN_W["rq"] + IN_W["rk"] + IN_W["rg"]
    assert dc_col % tb == 0 and tb >= DSA_LATENT
    a_idx = lambda j: jnp.minimum(j, na - 1)
    b_idx = lambda j: jnp.maximum(j - na, 0)
    return pl.pallas_call(
        functools.partial(_proj_kernel, na=na, dc_tile=dc_col // tb),
        grid=(n // tm, na + nb),
        in_specs=[
            pl.BlockSpec((tm, d), lambda i, j: (i, 0)),
            pl.BlockSpec((1, d), lambda i, j: (0, 0)),
            pl.BlockSpec((d, ta), lambda i, j: (0, a_idx(j))),
            pl.BlockSpec((1, ta), lambda i, j: (0, a_idx(j))),
            pl.BlockSpec((d, tb), lambda i, j: (0, b_idx(j))),
            pl.BlockSpec((1, DSA_LATENT), lambda i, j: (0, 0)),
        ],
        out_specs=[
            pl.BlockSpec((tm, ta), lambda i, j: (i, a_idx(j))),
            pl.BlockSpec((tm, tb), lambda i, j: (i, b_idx(j))),
            pl.BlockSpec((tm, DSA_LATENT), lambda i, j: (i, 0)),
        ],
        out_shape=[jax.ShapeDtypeStruct((n, ca), BF16), jax.ShapeDtypeStruct((n, cb), F32),
                   jax.ShapeDtypeStruct((n, DSA_LATENT), BF16)],
        scratch_shapes=[pltpu.VMEM((tm, d), BF16)],
        compiler_params=_params(("parallel", "arbitrary")),
    )(x, g.reshape(1, d), w_bf, bf_scale.reshape(1, ca), w_misc, kv_norm.reshape(1, DSA_LATENT))


def _split3(x):
    hi = x.astype(BF16)
    r = x - hi.astype(F32)
    mid = r.astype(BF16)
    lo = (r - mid.astype(F32)).astype(BF16)
    return hi, mid, lo


def _cum_kernel(s_ref, b_ref, eq_ref, ek_ref, oneq_ref, onek_ref, aq_ref, ak_ref, carry_ref):
    @pl.when(pl.program_id(1) == 0)
    def _():
        carry_ref[...] = jnp.zeros_like(carry_ref)

    z = s_ref[...] + b_ref[...]
    lf = jnp.minimum(z, 0.0) - jnp.log1p(jnp.exp(-jnp.abs(z)))
    r = lax.broadcasted_iota(jnp.int32, (BLOCK, BLOCK), 0)
    c = lax.broadcasted_iota(jnp.int32, (BLOCK, BLOCK), 1)
    tri = jnp.where(c <= r, 1.0, 0.0).astype(BF16)
    hi, mid, lo = _split3(lf)
    cum = (_dot(tri, lo) + _dot(tri, mid)) + _dot(tri, hi) + carry_ref[...]
    carry_ref[...] = cum[BLOCK - 1:BLOCK, :]
    pieces = jnp.concatenate(_split3(cum * LOG2E), axis=1)
    aq_ref[...] = (_dot(pieces, eq_ref[...]) + oneq_ref[...]).astype(aq_ref.dtype)
    ak_ref[...] = (_dot(pieces, ek_ref[...]) + onek_ref[...]).astype(ak_ref.dtype)


def _fox_bias_columns(misc3, f_bias):
    b, t, _ = misc3.shape
    width = FOX_HEADS * FOX_HEAD_DIM
    blk = (IN_W["rq"] + IN_W["rk"] + IN_W["rg"] + IN_W["dc"]) // LANE
    bias = jnp.zeros((1, LANE), F32).at[0, SMALL_FF:SMALL_FF + FOX_HEADS].set(f_bias)
    eq = np.zeros((3 * LANE, width), np.float32)
    ek = np.zeros((3 * LANE, width), np.float32)
    oneq = np.zeros((1, width), np.float32)
    onek = np.zeros((1, width), np.float32)
    for h in range(FOX_HEADS):
        for piece in range(3):
            eq[piece * LANE + SMALL_FF + h, h * FOX_HEAD_DIM + piece] = 1.0
            ek[piece * LANE + SMALL_FF + h, h * FOX_HEAD_DIM + 3 + piece] = -1.0
            oneq[0, h * FOX_HEAD_DIM + 3 + piece] = 1.0
            onek[0, h * FOX_HEAD_DIM + piece] = 1.0
    const = lambda bi, i: (0, 0)
    out = pl.BlockSpec((None, BLOCK, width), lambda bi, i: (bi, i, 0))
    return pl.pallas_call(
        _cum_kernel,
        grid=(b, t // BLOCK),
        in_specs=[
            pl.BlockSpec((None, BLOCK, LANE), lambda bi, i: (bi, i, blk)),
            pl.BlockSpec((1, LANE), const),
            pl.BlockSpec((3 * LANE, width), const),
            pl.BlockSpec((3 * LANE, width), const),
            pl.BlockSpec((1, width), const),
            pl.BlockSpec((1, width), const),
        ],
        out_specs=[out, out],
        out_shape=[jax.ShapeDtypeStruct((b, t, width), BF16)] * 2,
        scratch_shapes=[pltpu.VMEM((1, LANE), F32)],
        compiler_params=_params(("parallel", "arbitrary")),
    )(misc3, bias, jnp.asarray(eq, BF16), jnp.asarray(ek, BF16), jnp.asarray(oneq), jnp.asarray(onek))


def _bit_transpose32(words):
    a = list(words)
    j, m = 16, 0x0000FFFF
    while j:
        mask = jnp.int32(m if m < 2 ** 31 else m - 2 ** 32)
        for k in range(32):
            if not k & j:
                t = (lax.shift_right_logical(a[k], jnp.int32(j)) ^ a[k + j]) & mask
                a[k + j] = a[k + j] ^ t
                a[k] = a[k] ^ (t << j)
        j >>= 1
        m = (m ^ (m << j)) & 0xFFFFFFFF
    return a


FAR_BLOCKS = 4


def _dsa_kernel(dq_ref, iq_ref, sm_ref, ik_ref, c_ref, ct_ref, wuk_ref, wuv_ref, bias_ref,
                o_ref, s_ref, qlat_ref, p_ref, m_ref, acc_ref, lg0_ref, lg1_ref, lgn_ref, iqt_ref, wt_ref,
                planes_ref, alive_ref, p1_ref, *, topk, nblk):
    i = pl.program_id(1)
    key_i = lax.broadcasted_iota(jnp.int32, (BLOCK, BLOCK), 0)
    qry_i = lax.broadcasted_iota(jnp.int32, (BLOCK, BLOCK), 1)
    hs = lambda h: slice(h * BLOCK, (h + 1) * BLOCK)

    iq_t = jnp.transpose(iq_ref[...].astype(F32)).astype(BF16)
    for h in range(IDX_HEADS):
        iqt_ref[0:IDX_DIM, hs(h)] = iq_t[h * IDX_DIM:(h + 1) * IDX_DIM, :]
    iqt_ref[IDX_DIM:LANE, :] = jnp.zeros((LANE - IDX_DIM, IDX_HEADS * BLOCK), BF16)
    wt_ref[...] = jnp.transpose(sm_ref[...])[SMALL_IW:SMALL_IW + IDX_HEADS, :] * (IDX_HEADS ** -0.5 * IDX_DIM ** -0.5)

    for h in range(DSA_HEADS):
        ql = _dot_nt(wuk_ref[h], dq_ref[:, h * DSA_HEAD_DIM:(h + 1) * DSA_HEAD_DIM]) * (DSA_HEAD_DIM ** -0.5 * LOG2E)
        qlat_ref[:, hs(h)] = ql.astype(BF16)

    ngrp = (i + SCORE_GROUP) // SCORE_GROUP
    npair = (ngrp + 1) // 2
    for u in range(1, 2 * SCORE_GROUP):
        s_ref[i + u] = jnp.full((BLOCK, BLOCK), INT_MIN, jnp.int32)

    def score_group(g, _):
        j0 = jnp.minimum(g * SCORE_GROUP, nblk - SCORE_GROUP)
        kt = ik_ref[pl.ds(pl.multiple_of(j0 * BLOCK, BLOCK), SCORE_GROUP * BLOCK), :]
        r = _dot(kt, iqt_ref[...])
        t = i * BLOCK + qry_i
        for u in range(SCORE_GROUP):
            acc = jnp.zeros((BLOCK, BLOCK), F32)
            for h in range(IDX_HEADS):
                acc = acc + jnp.maximum(r[u * BLOCK:(u + 1) * BLOCK, hs(h)], 0.0) * wt_ref[h:h + 1, :]
            s = (j0 + u) * BLOCK + key_i
            ok = (s <= t) & ((s >= PAD) | (s == t))
            bits = lax.bitcast_convert_type(acc, jnp.int32)
            bits = jnp.where(bits == INT_MIN, 0, bits)
            key = bits ^ ((bits >> 31) & 0x7FFFFFFF)
            s_ref[j0 + u] = jnp.where(ok, key, INT_MIN)
        return 0

    lax.fori_loop(0, ngrp, score_group, 0)

    def slice_group(g, _):
        for half in range(2):
            words = []
            for u in (2 * half, 2 * half + 1):
                tile = s_ref[g * SCORE_GROUP + u] ^ INT_MIN
                words += [tile[r * 8:(r + 1) * 8, :] for r in range(BLOCK // 8)]
            planes = _bit_transpose32(words)
            for bit in range(32):
                planes_ref[2 * g + half, bit] = planes[bit]
            planes_ref[2 * g + half, 32] = jnp.full((8, BLOCK), -1, jnp.int32)
        return 0

    lax.fori_loop(0, 2 * npair, slice_group, 0)
    alive_ref[...] = jnp.full(alive_ref.shape, -1, jnp.int32)

    def radix_pass(t, carry):
        thr, need, drop_prev = carry
        bit = 31 - t

        def body(pr, accs):
            accs = list(accs)
            for u in range(4):
                st = 4 * pr + u
                alive = alive_ref[st] & (planes_ref[st, bit + 1] ^ drop_prev)
                alive_ref[st] = alive
                accs[u] = accs[u] + lax.population_count(alive & planes_ref[st, bit])
            return tuple(accs)

        zero = jnp.zeros((8, BLOCK), jnp.int32)
        accs = lax.fori_loop(0, npair, body, (zero,) * 4)
        cnt = ((accs[0] + accs[1]) + (accs[2] + accs[3])).astype(F32)
        cnt = jnp.sum(cnt, axis=0, keepdims=True).astype(jnp.int32)
        take = cnt >= need
        thr = thr | jnp.where(take, jnp.int32(1) << bit, 0)
        need = jnp.where(take, need, need - cnt)
        return thr, need, jnp.where(take, 0, -1)

    zero = jnp.zeros((1, BLOCK), jnp.int32)
    thr, _, _ = lax.fori_loop(0, 32, radix_pass, (zero, zero + topk, zero))
    thr = jnp.maximum(thr ^ INT_MIN, INT_MIN + 1)

    m_ref[...] = jnp.full(m_ref.shape, NEG, F32)
    acc_ref[...] = jnp.zeros_like(acc_ref)

    def logits(j0, nb, dst_ref):
        w = nb * BLOCK
        dst_ref[0:w, :] = _dot(c_ref[pl.ds(pl.multiple_of(j0 * BLOCK, BLOCK), w), :], qlat_ref[...])

    def softmax_pv(j0, nb, lg_ref, pb_ref, bias_off=None):
        w = nb * BLOCK
        ckt = ct_ref[:, pl.ds(pl.multiple_of(j0 * BLOCK, BLOCK), w)]
        masked = jnp.where(s_ref[pl.ds(j0, nb)].reshape(w, BLOCK) >= thr, 0.0, NEG)
        alphas = []
        for h in range(DSA_HEADS):
            x = lg_ref[0:w, hs(h)] + masked
            if bias_off is not None:
                x = x + bias_ref[h, pl.ds(bias_off, w), :]
            m_prev = m_ref[h:h + 1, :]
            m_new = jnp.maximum(m_prev, jnp.max(x, axis=0, keepdims=True))
            alphas.append(jnp.exp2(m_prev - m_new))
            m_ref[h:h + 1, :] = m_new
            pb_ref[0:w, hs(h)] = jnp.exp2(x - m_new).astype(BF16)
        pv = _dot(ckt, pb_ref[0:w, :])
        for h in range(DSA_HEADS):
            acc_ref[:, hs(h)] = acc_ref[:, hs(h)] * alphas[h] + pv[:, hs(h)]

    near0 = jnp.maximum(i - 1, 0)
    logits(near0, 2, lgn_ref)

    n_far = jnp.maximum(i - 1, 0)
    n_big = n_far // FAR_BLOCKS
    big = lambda g: g * FAR_BLOCKS

    @pl.when(n_big > 0)
    def _():
        logits(0, FAR_BLOCKS, lg0_ref)

        def pair_of_tiles(g, _):
            logits(big(2 * g + 1), FAR_BLOCKS, lg1_ref)
            softmax_pv(big(2 * g), FAR_BLOCKS, lg0_ref, p_ref)
            logits(big(jnp.minimum(2 * g + 2, n_big - 1)), FAR_BLOCKS, lg0_ref)
            softmax_pv(big(2 * g + 1), FAR_BLOCKS, lg1_ref, p1_ref)
            return 0

        lax.fori_loop(0, n_big // 2, pair_of_tiles, 0)

        @pl.when(n_big % 2 == 1)
        def _():
            softmax_pv(big(n_big - 1), FAR_BLOCKS, lg0_ref, p_ref)

    def far_one(j, _):
        logits(j, 1, lg0_ref)
        softmax_pv(j, 1, lg0_ref, p_ref)
        return 0

    lax.fori_loop(n_big * FAR_BLOCKS, n_far, far_one, 0)
    softmax_pv(near0, 2, lgn_ref, p1_ref, bias_off=pl.multiple_of(jnp.where(i >= 1, 0, BLOCK), BLOCK))

    for h in range(DSA_HEADS):
        inv_l = 1.0 / acc_ref[DSA_LATENT:DSA_LATENT + 1, hs(h)]
        o_lat = (acc_ref[0:DSA_LATENT, hs(h)] * inv_l).astype(BF16)
        o = _dot_tn(o_lat, wuv_ref[h])
        o_ref[:, h * DSA_HEAD_DIM:(h + 1) * DSA_HEAD_DIM] = o.astype(o_ref.dtype)


def _dsa(pbf3, misc3, c3, ct3, wuk, wuv, bias, topk):
    b, t, _ = pbf3.shape
    nblk = t // BLOCK
    hq = DSA_HEADS * DSA_HEAD_DIM
    ik_blk = 6 * 1024 // LANE
    small_blk = (IN_W["rq"] + IN_W["rk"] + IN_W["rg"] + IN_W["dc"]) // LANE
    n_sets = 4 * (((nblk + SCORE_GROUP - 1) // SCORE_GROUP + 1) // 2)
    n_tiles = nblk + 2 * SCORE_GROUP - 1
    const3 = lambda bi, i: (0, 0, 0)
    return pl.pallas_call(
        functools.partial(_dsa_kernel, topk=topk, nblk=nblk),
        grid=(b, nblk),
        in_specs=[
            pl.BlockSpec((None, BLOCK, hq), lambda bi, i: (bi, i, 0)),
            pl.BlockSpec((None, BLOCK, IDX_HEADS * IDX_DIM), lambda bi, i: (bi, i, 1)),
            pl.BlockSpec((None, BLOCK, LANE), lambda bi, i: (bi, i, small_blk)),
            pl.BlockSpec((None, t, LANE), lambda bi, i: (bi, 0, ik_blk)),
            pl.BlockSpec((None, t, DSA_LATENT), lambda bi, i: (bi, 0, 0)),
            pl.BlockSpec((None, DSA_LATENT + 8, t), lambda bi, i: (bi, 0, 0)),
            pl.BlockSpec((DSA_HEADS, DSA_LATENT, DSA_HEAD_DIM), const3),
            pl.BlockSpec((DSA_HEADS, DSA_LATENT, DSA_HEAD_DIM), const3),
            pl.BlockSpec((DSA_HEADS, 3 * BLOCK, BLOCK), const3),
        ],
        out_specs=pl.BlockSpec((None, BLOCK, hq), lambda bi, i: (bi, i, 0)),
        out_shape=jax.ShapeDtypeStruct((b, t, hq), BF16),
        scratch_shapes=[
            pltpu.VMEM((n_tiles, BLOCK, BLOCK), jnp.int32),
            pltpu.VMEM((DSA_LATENT, DSA_HEADS * BLOCK), BF16),
            pltpu.VMEM((FAR_BLOCKS * BLOCK, DSA_HEADS * BLOCK), BF16),
            pltpu.VMEM((DSA_HEADS, BLOCK), F32),
            pltpu.VMEM((DSA_LATENT + 8, DSA_HEADS * BLOCK), F32),
            pltpu.VMEM((FAR_BLOCKS * BLOCK, DSA_HEADS * BLOCK), F32),
            pltpu.VMEM((FAR_BLOCKS * BLOCK, DSA_HEADS * BLOCK), F32),
            pltpu.VMEM((2 * BLOCK, DSA_HEADS * BLOCK), F32),
            pltpu.VMEM((LANE, IDX_HEADS * BLOCK), BF16),
            pltpu.VMEM((IDX_HEADS, BLOCK), F32),
            pltpu.VMEM((n_sets, 33, 8, BLOCK), jnp.int32),
            pltpu.VMEM((n_sets, 8, BLOCK), jnp.int32),
            pltpu.VMEM((FAR_BLOCKS * BLOCK, DSA_HEADS * BLOCK), BF16),
        ],
        compiler_params=_params(("parallel", "arbitrary")),
    )(pbf3, pbf3, misc3, pbf3, c3, ct3, wuk, wuv, bias)


def _dsa_weights(w_uk, w_uv):
    return jnp.transpose(w_uk, (1, 0, 2)).astype(BF16), jnp.transpose(w_uv, (1, 0, 2)).astype(BF16)


FOX_ROWS = 64


def _fox_kernel(q_ref, qa_ref, k_ref, ka_ref, v_ref, o_ref, m_ref, l_ref, acc_ref, lg_ref, p_ref, *, tq):
    i = pl.program_id(2)
    qq = jnp.concatenate([q_ref[...], qa_ref[...]], axis=1)
    m_ref[...] = jnp.full(m_ref.shape, NEG, F32)
    l_ref[...] = jnp.zeros_like(l_ref)
    acc_ref[...] = jnp.zeros_like(acc_ref)
    rows = min(FOX_ROWS, tq)
    chunks = [slice(c, c + rows) for c in range(0, tq, rows)]

    def qk(j, slot):
        off = pl.multiple_of(j * tq, tq)
        kk = jnp.concatenate([k_ref[pl.ds(off, tq), :], ka_ref[pl.ds(off, tq), :]], axis=1)
        lg_ref[slot] = _dot_nt(kk, qq)

    def softmax_pv(j, slot, mode):
        def logits(rs):
            x = lg_ref[slot, rs, :]
            if mode == "none":
                return x
            s = rs.start + lax.broadcasted_iota(jnp.int32, (rows, 1), 0)
            if mode == "pad":
                return jnp.where(s >= PAD, x, NEG)
            t = lax.broadcasted_iota(jnp.int32, (1, tq), 1)
            return jnp.where((s <= t) & ((s + j * tq >= PAD) | (s == t)), x, NEG)

        fold = lambda a: a.reshape(rows // 8, 8, tq)
        mx = jnp.full((8, tq), NEG, F32)
        for rs in chunks:
            mx = jnp.maximum(mx, jnp.max(fold(logits(rs)), axis=0))
        m_prev = m_ref[...]
        m_new = jnp.maximum(m_prev, jnp.max(mx, axis=0, keepdims=True))
        alpha = jnp.exp2(m_prev - m_new)
        tot = jnp.zeros((8, tq), F32)
        for rs in chunks:
            p = jnp.exp2(logits(rs) - m_new)
            tot = tot + jnp.sum(fold(p), axis=0)
            p_ref[rs, :] = p.astype(BF16)
        l_ref[...] = alpha * l_ref[...] + jnp.sum(tot, axis=0, keepdims=True)
        off = pl.multiple_of(j * tq, tq)
        acc_ref[...] = acc_ref[...] * alpha + _dot_tn(v_ref[pl.ds(off, tq), :], p_ref[...])
        m_ref[...] = m_new

    qk(i, 2)

    @pl.when(i > 0)
    def _():
        qk(0, 0)
        qk(jnp.minimum(1, i), 1)
        softmax_pv(0, 0, "pad")
        n_mid = i - 1

        def pair(g, _):
            a = 2 * g + 1
            qk(jnp.minimum(a + 1, i), 0)
            softmax_pv(a, 1, "none")
            qk(jnp.minimum(a + 2, i), 1)
            softmax_pv(a + 1, 0, "none")
            return 0

        lax.fori_loop(0, n_mid // 2, pair, 0)

        @pl.when(n_mid % 2 == 1)
        def _():
            softmax_pv(i - 1, 1, "none")

    softmax_pv(i, 2, "diag")

    o_ref[...] = jnp.transpose(acc_ref[...] * (1.0 / l_ref[...])).astype(o_ref.dtype)


def _fox(pbf3, aug_q, aug_k):
    b, t, _ = pbf3.shape
    nblk = t // BLOCK
    tq = BLOCK * max(k for k in (1, 5) if nblk % k == 0)
    d = FOX_HEAD_DIM
    qb, kb, vb = 2 * 1024 // d, 3 * 1024 // d, 4 * 1024 // d
    return pl.pallas_call(
        functools.partial(_fox_kernel, tq=tq),
        grid=(b, FOX_HEADS, t // tq),
        in_specs=[
            pl.BlockSpec((None, tq, d), lambda bi, h, i: (bi, i, qb + h)),
            pl.BlockSpec((None, tq, d), lambda bi, h, i: (bi, i, h)),
            pl.BlockSpec((None, t, d), lambda bi, h, i: (bi, 0, kb + h)),
            pl.BlockSpec((None, t, d), lambda bi, h, i: (bi, 0, h)),
            pl.BlockSpec((None, t, d), lambda bi, h, i: (bi, 0, vb + h)),
        ],
        out_specs=pl.BlockSpec((None, tq, d), lambda bi, h, i: (bi, i, h)),
        out_shape=jax.ShapeDtypeStruct((b, t, FOX_HEADS * d), BF16),
        scratch_shapes=[pltpu.VMEM((1, tq), F32), pltpu.VMEM((1, tq), F32), pltpu.VMEM((d, tq), F32),
                        pltpu.VMEM((3, tq, tq), F32), pltpu.VMEM((tq, tq), BF16)],
        compiler_params=_params(("parallel", "parallel", "arbitrary")),
    )(pbf3, aug_q, pbf3, aug_k, pbf3)


def _ret_kernel(rq_ref, rk_ref, rv_ref, rg_ref, cos_ref, sin_ref, dq_ref, dk_ref, decay_ref, cd_ref, gn_ref,
                o_ref, state_ref):
    i = pl.program_id(1)

    @pl.when(i == 0)
    def _():
        state_ref[...] = jnp.zeros_like(state_ref)

    half = RET_HEADS * RET_QK_DIM // 2
    cos = cos_ref[...]
    sin = sin_ref[...]

    def rope(x):
        x1, x2 = x[:, :half], x[:, half:]
        return jnp.concatenate([x1 * cos - x2 * sin, x1 * sin + x2 * cos], axis=1)

    q = rope(rq_ref[...])
    k = rope(rk_ref[...]) * RET_QK_DIM ** -0.5
    pos = i * BLOCK + lax.broadcasted_iota(jnp.int32, (BLOCK, 1), 0)
    k = jnp.where(pos >= PAD, k, 0.0)
    qb = q.astype(BF16)
    kb = k.astype(BF16)
    q_dec = (q * dq_ref[...]).astype(BF16)
    k_dec = (k * dk_ref[...]).astype(BF16)
    head_of_col = (lax.broadcasted_iota(jnp.int32, (1, 2 * half), 1) % half) // (RET_QK_DIM // 2)
    zero = jnp.zeros((), BF16)
    for h in range(RET_HEADS):
        mine = head_of_col == h
        vh = rv_ref[:, h * RET_V_DIM:(h + 1) * RET_V_DIM]
        s = (_dot_nt(jnp.where(mine, qb, zero), kb) * decay_ref[h]).astype(BF16)
        st = state_ref[h]
        o = _dot(s, vh) + _dot(jnp.where(mine, q_dec, zero), st.astype(BF16))
        state_ref[h] = cd_ref[h] * st + _dot_tn(k_dec, vh)
        mu = jnp.mean(o, axis=1, keepdims=True)
        var = jnp.mean(jnp.square(o - mu), axis=1, keepdims=True)
        sl = slice(h * RET_V_DIM, (h + 1) * RET_V_DIM)
        o = (o - mu) * lax.rsqrt(var + GN_EPS) * gn_ref[:, sl]
        g = rg_ref[:, sl]
        o_ref[:, sl] = (g * jax.nn.sigmoid(g) * o).astype(o_ref.dtype)


def _retention(misc3, pbf3, cos, sin, dq_tab, dk_tab, decay, cd, gn_w):
    b, t, _ = misc3.shape
    qk = RET_HEADS * RET_QK_DIM
    vw = RET_HEADS * RET_V_DIM
    const2 = lambda bi, i: (0, 0)
    const3 = lambda bi, i: (0, 0, 0)
    return pl.pallas_call(
        _ret_kernel,
        grid=(b, t // BLOCK),
        in_specs=[
            pl.BlockSpec((None, BLOCK, qk), lambda bi, i: (bi, i, 0)),
            pl.BlockSpec((None, BLOCK, qk), lambda bi, i: (bi, i, 1)),
            pl.BlockSpec((None, BLOCK, vw), lambda bi, i: (bi, i, 5)),
            pl.BlockSpec((None, BLOCK, vw), lambda bi, i: (bi, i, 1)),
            pl.BlockSpec((BLOCK, qk // 2), lambda bi, i: (i, 0)),
            pl.BlockSpec((BLOCK, qk // 2), lambda bi, i: (i, 0)),
            pl.BlockSpec((BLOCK, qk), const2),
            pl.BlockSpec((BLOCK, qk), const2),
            pl.BlockSpec((RET_HEADS, BLOCK, BLOCK), const3),
            pl.BlockSpec((RET_HEADS, 1, RET_V_DIM), const3),
            pl.BlockSpec((1, vw), const2),
        ],
        out_specs=pl.BlockSpec((None, BLOCK, vw), lambda bi, i: (bi, i, 0)),
        out_shape=jax.ShapeDtypeStruct((b, t, vw), BF16),
        scratch_shapes=[pltpu.VMEM((RET_HEADS, qk, RET_V_DIM), F32)],
        compiler_params=_params(("parallel", "arbitrary")),
    )(misc3, misc3, pbf3, misc3, cos, sin, dq_tab, dk_tab, decay, cd, gn_w.reshape(1, vw))


def _merge_kernel(x_ref, g_ref, b0_ref, b1_ref, b2_ref, wg0_ref, wg1_ref, wg2_ref, wb_ref, o_ref, xn_ref):
    @pl.when(pl.program_id(1) == 0)
    def _():
        xn_ref[...] = _rms(x_ref[...], g_ref[...]).astype(BF16)

    xn = xn_ref[...]
    acc = jax.nn.sigmoid(_dot(xn, wg0_ref[...])) * _dot(b0_ref[...], wb_ref[0])
    acc = acc + jax.nn.sigmoid(_dot(xn, wg1_ref[...])) * _dot(b1_ref[...], wb_ref[1])
    acc = acc + jax.nn.sigmoid(_dot(xn, wg2_ref[...])) * _dot(b2_ref[...], wb_ref[2])
    o_ref[...] = acc.astype(o_ref.dtype)


def _merge(x, g, b0, b1, b2, w_gates, w_branch):
    n, bw = b0.shape
    dm = x.shape[1]
    d = w_branch.shape[2]
    tm = _pick_tile(n, 640)
    tn = _pick_tile(d, 512, LANE)
    nj = d // tn
    bspec = pl.BlockSpec((tm, bw), lambda i, j: (i, 0))
    gspec = lambda r: pl.BlockSpec((dm, tn), lambda i, j: (0, r * nj + j))
    return pl.pallas_call(
        _merge_kernel,
        grid=(n // tm, nj),
        in_specs=[pl.BlockSpec((tm, dm), lambda i, j: (i, 0)), pl.BlockSpec((1, dm), lambda i, j: (0, 0)),
                  bspec, bspec, bspec, gspec(0), gspec(1), gspec(2),
                  pl.BlockSpec((N_BRANCH, bw, tn), lambda i, j: (0, 0, j))],
        out_specs=pl.BlockSpec((tm, tn), lambda i, j: (i, j)),
        out_shape=jax.ShapeDtypeStruct((n, d), BF16),
        scratch_shapes=[pltpu.VMEM((tm, dm), BF16)],
        compiler_params=_params(("parallel", "arbitrary")),
    )(x, g.reshape(1, dm), b0, b1, b2, w_gates, w_gates, w_gates, w_branch)


def _resproj_kernel(a_ref, w_ref, r_ref, o_ref):
    o_ref[...] = r_ref[...] + _dot(a_ref[...], w_ref[...])


def _resproj(a, w, res):
    n, k = a.shape
    d = w.shape[1]
    tm = _pick_tile(n, 1280)
    tn = _pick_tile(d, 1024, LANE)
    return pl.pallas_call(
        _resproj_kernel,
        grid=(n // tm, d // tn),
        in_specs=[
            pl.BlockSpec((tm, k), lambda i, j: (i, 0)),
            pl.BlockSpec((k, tn), lambda i, j: (0, j)),
            pl.BlockSpec((tm, tn), lambda i, j: (i, j)),
        ],
        out_specs=pl.BlockSpec((tm, tn), lambda i, j: (i, j)),
        out_shape=jax.ShapeDtypeStruct((n, d), F32),
        compiler_params=_params(("parallel", "arbitrary")),
    )(a, w, res)


def _t5_bucket(dist):
    d = jnp.maximum(dist, 1).astype(F32)
    large = T5_MAX_EXACT + (jnp.log(d / T5_MAX_EXACT) / math.log(T5_MAX_DIST / T5_MAX_EXACT)
                            * (T5_BUCKETS - T5_MAX_EXACT)).astype(jnp.int32)
    large = jnp.minimum(large, T5_BUCKETS - 1)
    return jnp.where(dist < T5_MAX_EXACT, dist, large)


def _t5_bias_tiles(t5_table):
    k = jnp.arange(BLOCK, dtype=jnp.int32)[:, None]
    q = jnp.arange(BLOCK, dtype=jnp.int32)[None, :]
    table = t5_table.astype(F32)
    rel = table - table[T5_BUCKETS - 1]

    def lookup(dist):
        onehot = (_t5_bucket(dist)[..., None] == jnp.arange(T5_BUCKETS, dtype=jnp.int32)).astype(F32)
        return jnp.einsum("kqb,bh->hkq", onehot, rel, precision=lax.Precision.HIGHEST)

    bdiag = lookup(jnp.maximum(q - k, 0))
    bprev = lookup(BLOCK + q - k)
    return jnp.concatenate([bprev, bdiag, jnp.zeros_like(bdiag)], axis=1) * LOG2E


def _rope_tables(t):
    half = RET_QK_DIM // 2
    inv = ROPE_BASE ** (-jnp.arange(half, dtype=F32) / half)
    ang = jnp.arange(t, dtype=jnp.int32).astype(F32)[:, None] * inv[None, :]
    return jnp.tile(jnp.cos(ang), (1, RET_HEADS)), jnp.tile(jnp.sin(ang), (1, RET_HEADS))


def _decay_tables():
    log_gamma = jnp.log1p(-(2.0 ** (-5.0 - jnp.arange(RET_HEADS, dtype=F32))))
    i = jnp.arange(BLOCK, dtype=F32)
    gap = i[:, None] - i[None, :]
    decay = jnp.where(gap[None] >= 0, jnp.exp(jnp.maximum(gap, 0.0)[None] * log_gamma[:, None, None]), 0.0)
    head_of_col = (np.arange(RET_HEADS * RET_QK_DIM) % (RET_HEADS * RET_QK_DIM // 2)) // (RET_QK_DIM // 2)
    lg_col = log_gamma[head_of_col]
    dk_tab = jnp.exp((BLOCK - 1 - i)[:, None] * lg_col[None, :])
    dq_tab = jnp.exp((i + 1.0)[:, None] * lg_col[None, :])
    cd = jnp.broadcast_to(jnp.exp(BLOCK * log_gamma)[:, None, None], (RET_HEADS, 1, RET_V_DIM))
    return decay.astype(F32), dq_tab, dk_tab, cd


def _split_w_in(w16):
    cols = lambda name: w16[:, IN_OFF[name]:IN_OFF[name] + IN_W[name]]
    zeros = lambda w: jnp.zeros((w16.shape[0], w), BF16)
    rope = lambda a: jnp.swapaxes(a.reshape(-1, RET_HEADS, 2, RET_QK_DIM // 2), 1, 2).reshape(a.shape)
    w_bf = jnp.concatenate([cols("dq"), cols("iq"), cols("fq"), cols("fk"), cols("fv"), cols("rv"),
                            cols("ik"), zeros(LANE - IDX_DIM)], axis=1)
    w_misc = jnp.concatenate([rope(cols("rq")), rope(cols("rk")), cols("rg"), cols("dc"),
                              cols("iw"), cols("ff"), zeros(LANE - IDX_HEADS - FOX_HEADS),
                              zeros(LANE)], axis=1)
    assert w_bf.shape[1] == BF_COLS and w_misc.shape[1] == MISC_COLS
    return w_bf, w_misc, cols("gates")


def _bf_col_scale():
    scale = np.ones((BF_COLS,), np.float32)
    scale[2 * 1024:3 * 1024] = FOX_HEAD_DIM ** -0.5 * LOG2E
    return jnp.asarray(scale)


def _mixer(x, b, t, mix_norm, w_in16, kv_norm, w_uk, w_uv, f_bias, gn_w, w_branch16, w_out16, tables, topk):
    n = b * t
    bias_tiles, rope_tabs, decay_tabs = tables
    w_bf, w_misc, w_gates = _split_w_in(w_in16)
    pbf, misc, c = _proj(x, mix_norm, w_bf, _bf_col_scale(), w_misc, kv_norm)
    pbf3 = pbf.reshape(b, t, BF_COLS)
    misc3 = misc.reshape(b, t, MISC_COLS)

    c3 = c.reshape(b, t, DSA_LATENT)
    wuk, wuv = _dsa_weights(w_uk, w_uv)
    ones_rows = jnp.zeros((b, 8, t), BF16).at[:, 0].set(1)
    ct3 = jnp.concatenate([jnp.swapaxes(c3, 1, 2), ones_rows], axis=1)
    dsa_o = _dsa(pbf3, misc3, c3, ct3, wuk, wuv, bias_tiles, topk)

    fox_o = _fox(pbf3, *_fox_bias_columns(misc3, f_bias))

    ret_o = _retention(misc3, pbf3, *rope_tabs, *decay_tabs, gn_w)

    merged = _merge(x, mix_norm, dsa_o.reshape(n, -1), fox_o.reshape(n, -1), ret_o.reshape(n, -1), w_gates,
                    w_branch16)
    return _resproj(merged, w_out16, x)


def kernel(x, meta_tokens, t5_table, ffn1_norm, ffn1_w1, ffn1_w3, ffn1_w2, mix_norm, w_in, dsa_kv_norm,
           dsa_w_uk, dsa_w_uv, fox_f_bias, ret_gn_w, w_branch, w_out, ffn2_norm, ffn2_w1, ffn2_w3, ffn2_w2,
           final_norm):
    b, seq, d = x.shape
    assert d == D_MODEL and seq % BLOCK == 0
    t = seq + BLOCK
    depth = ffn1_norm.shape[0]
    topk = min(INDEX_TOPK, seq // 4)
    h = jnp.concatenate([
        jnp.zeros((b, PAD, d), x.dtype),
        jnp.broadcast_to(meta_tokens.astype(x.dtype)[None], (b, N_META, d)),
        x,
    ], axis=1).reshape(b * t, d)
    cos, sin = _rope_tables(t)
    decay, dq_tab, dk_tab, cd = _decay_tables()
    tables = (_t5_bias_tiles(t5_table), (cos, sin), (dq_tab, dk_tab, decay, cd))
    bf = lambda w, l: w[l].astype(BF16)
    for l in range(depth):
        h = _ffn(h, ffn1_norm[l], bf(ffn1_w1, l), bf(ffn1_w3, l), bf(ffn1_w2, l))
        h = _mixer(h, b, t, mix_norm[l], bf(w_in, l), dsa_kv_norm[l], dsa_w_uk[l], dsa_w_uv[l],
                   fox_f_bias[l], ret_gn_w[l], bf(w_branch, l), bf(w_out, l), tables, topk)
        h = _ffn(h, ffn2_norm[l], bf(ffn2_w1, l), bf(ffn2_w3, l), bf(ffn2_w2, l),
                 final_g=final_norm if l == depth - 1 else None)
    return h.reshape(b, t, d)[:, BLOCK:]
```

```python
import functools
import math

import numpy as np
import jax
import jax.numpy as jnp
from jax import lax
from jax.experimental import pallas as pl
from jax.experimental.pallas import tpu as pltpu

F32 = jnp.float32
BF16 = jnp.bfloat16

LANE = 128
VMEM_LIMIT = 56 * 1024 * 1024

D_MODEL = 2048
N_META = 16
BLOCK = 128
PAD = BLOCK - N_META
RMS_EPS = 1e-6
GN_EPS = 1e-5
NEG = -1e30

DSA_HEADS = 8
DSA_HEAD_DIM = 128
DSA_LATENT = 256
IDX_HEADS = 16
IDX_DIM = 64
INDEX_TOPK = 256
FOX_HEADS = 8
FOX_HEAD_DIM = 128
RET_HEADS = 8
RET_QK_DIM = 64
RET_V_DIM = 128
ROPE_BASE = 10000.0
N_BRANCH = 3
T5_BUCKETS = 32
T5_MAX_EXACT = 16
T5_MAX_DIST = 128

IN_WIDTHS = (1024, 256, 1024, 64, 16, 1024, 1024, 1024, 8, 512, 512, 1024, 1024, 6144)
IN_NAMES = ("dq", "dc", "iq", "ik", "iw", "fq", "fk", "fv", "ff", "rq", "rk", "rv", "rg", "gates")
IN_OFF = dict(zip(IN_NAMES, np.cumsum((0,) + IN_WIDTHS[:-1]).tolist()))
IN_W = dict(zip(IN_NAMES, IN_WIDTHS))

INT_MIN = -2 ** 31
LOG2E = math.log2(math.e)
SCORE_GROUP = 4

BF_COLS = 6 * 1024 + LANE
MISC_COLS = 2560
SMALL_IW = 0
SMALL_FF = IDX_HEADS


def _dot(a, b):
    return jnp.dot(a, b, preferred_element_type=F32)


def _dot_nt(a, b):
    return lax.dot_general(a, b, (((1,), (1,)), ((), ())), preferred_element_type=F32)


def _dot_tn(a, b):
    return lax.dot_general(a, b, (((0,), (0,)), ((), ())), preferred_element_type=F32)


def _pick_tile(n, target, mult=8):
    best = None
    for t in range(mult, min(n, target) + 1, mult):
        if n % t == 0:
            best = t
    assert best is not None, (n, target, mult)
    return best


def _params(sem):
    return pltpu.CompilerParams(dimension_semantics=sem, vmem_limit_bytes=VMEM_LIMIT)


def _rms(x, g):
    return x * lax.rsqrt(jnp.mean(x * x, axis=-1, keepdims=True) + RMS_EPS) * g


def _ffn_kernel(*refs, nj, final):
    if final:
        x_ref, g_ref, w1_ref, w3_ref, w2_ref, fg_ref, o_ref, xn_ref, acc_ref = refs
    else:
        x_ref, g_ref, w1_ref, w3_ref, w2_ref, o_ref, xn_ref, acc_ref = refs
    j = pl.program_id(1)

    @pl.when(j == 0)
    def _():
        xn_ref[...] = _rms(x_ref[...], g_ref[...]).astype(BF16)
        acc_ref[...] = jnp.zeros_like(acc_ref)

    xn = xn_ref[...]
    h1 = _dot(xn, w1_ref[...])
    h3 = _dot(xn, w3_ref[...])
    g = (h1 * jax.nn.sigmoid(h1) * h3).astype(BF16)
    acc_ref[...] += _dot(g, w2_ref[...])

    @pl.when(j == nj - 1)
    def _():
        y = x_ref[...] + 0.5 * acc_ref[...]
        if final:
            y = _rms(y, fg_ref[...])
        o_ref[...] = y


def _ffn(x, g, w1, w3, w2, final_g=None):
    n, d = x.shape
    f = w1.shape[1]
    tm = _pick_tile(n, 640)
    tf = _pick_tile(f, 512, LANE)
    nj = f // tf
    final = final_g is not None
    in_specs = [
        pl.BlockSpec((tm, d), lambda i, j: (i, 0)),
        pl.BlockSpec((1, d), lambda i, j: (0, 0)),
        pl.BlockSpec((d, tf), lambda i, j: (0, j)),
        pl.BlockSpec((d, tf), lambda i, j: (0, j)),
        pl.BlockSpec((tf, d), lambda i, j: (j, 0)),
    ]
    args = [x, g.reshape(1, d), w1, w3, w2]
    if final:
        in_specs.append(pl.BlockSpec((1, d), lambda i, j: (0, 0)))
        args.append(final_g.reshape(1, d))
    return pl.pallas_call(
        functools.partial(_ffn_kernel, nj=nj, final=final),
        grid=(n // tm, nj),
        in_specs=in_specs,
        out_specs=pl.BlockSpec((tm, d), lambda i, j: (i, 0)),
        out_shape=jax.ShapeDtypeStruct((n, d), F32),
        scratch_shapes=[pltpu.VMEM((tm, d), BF16), pltpu.VMEM((tm, d), F32)],
        compiler_params=_params(("parallel", "arbitrary")),
    )(*args)


def _proj_kernel(x_ref, g_ref, wa_ref, sa_ref, wb_ref, kvn_ref, oa_ref, ob_ref, oc_ref, xn_ref, *, na, dc_tile):
    j = pl.program_id(1)

    @pl.when(j == 0)
    def _():
        xn_ref[...] = _rms(x_ref[...], g_ref[...]).astype(BF16)

    @pl.when(j < na)
    def _():
        oa_ref[...] = (_dot(xn_ref[...], wa_ref[...]) * sa_ref[...]).astype(oa_ref.dtype)

    @pl.when(j >= na)
    def _():
        r = _dot(xn_ref[...], wb_ref[...])
        ob_ref[...] = r

        @pl.when(j == na + dc_tile)
        def _():
            oc_ref[...] = _rms(r[:, 0:DSA_LATENT], kvn_ref[...]).astype(oc_ref.dtype)


def _proj(x, g, w_bf, bf_scale, w_misc, kv_norm):
    n, d = x.shape
    ca, cb = w_bf.shape[1], w_misc.shape[1]
    tm = _pick_tile(n, 1280)
    ta = _pick_tile(ca, 1024, LANE)
    tb = _pick_tile(cb, 512, LANE)
    na, nb = ca // ta, cb // tb
    dc_col = IN_W["rq"] + IN_W["rk"] + IN_W["rg"]
    assert dc_col % tb == 0 and tb >= DSA_LATENT
    a_idx = lambda j: jnp.minimum(j, na - 1)
    b_idx = lambda j: jnp.maximum(j - na, 0)
    return pl.pallas_call(
        functools.partial(_proj_kernel, na=na, dc_tile=dc_col // tb),
        grid=(n // tm, na + nb),
        in_specs=[
            pl.BlockSpec((tm, d), lambda i, j: (i, 0)),
            pl.BlockSpec((1, d), lambda i, j: (0, 0)),
            pl.BlockSpec((d, ta), lambda i, j: (0, a_idx(j))),
            pl.BlockSpec((1, ta), lambda i, j: (0, a_idx(j))),
            pl.BlockSpec((d, tb), lambda i, j: (0, b_idx(j))),
            pl.BlockSpec((1, DSA_LATENT), lambda i, j: (0, 0)),
        ],
        out_specs=[
            pl.BlockSpec((tm, ta), lambda i, j: (i, a_idx(j))),
            pl.BlockSpec((tm, tb), lambda i, j: (i, b_idx(j))),
            pl.BlockSpec((tm, DSA_LATENT), lambda i, j: (i, 0)),
        ],
        out_shape=[jax.ShapeDtypeStruct((n, ca), BF16), jax.ShapeDtypeStruct((n, cb), F32),
                   jax.ShapeDtypeStruct((n, DSA_LATENT), BF16)],
        scratch_shapes=[pltpu.VMEM((tm, d), BF16)],
        compiler_params=_params(("parallel", "arbitrary")),
    )(x, g.reshape(1, d), w_bf, bf_scale.reshape(1, ca), w_misc, kv_norm.reshape(1, DSA_LATENT))


def _split3(x):
    hi = x.astype(BF16)
    r = x - hi.astype(F32)
    mid = r.astype(BF16)
    lo = (r - mid.astype(F32)).astype(BF16)
    return hi, mid, lo


def _cum_kernel(s_ref, b_ref, eq_ref, ek_ref, oneq_ref, onek_ref, aq_ref, ak_ref, carry_ref):
    @pl.when(pl.program_id(1) == 0)
    def _():
        carry_ref[...] = jnp.zeros_like(carry_ref)

    z = s_ref[...] + b_ref[...]
    lf = jnp.minimum(z, 0.0) - jnp.log1p(jnp.exp(-jnp.abs(z)))
    r = lax.broadcasted_iota(jnp.int32, (BLOCK, BLOCK), 0)
    c = lax.broadcasted_iota(jnp.int32, (BLOCK, BLOCK), 1)
    tri = jnp.where(c <= r, 1.0, 0.0).astype(BF16)
    hi, mid, lo = _split3(lf)
    cum = (_dot(tri, lo) + _dot(tri, mid)) + _dot(tri, hi) + carry_ref[...]
    carry_ref[...] = cum[BLOCK - 1:BLOCK, :]
    pieces = jnp.concatenate(_split3(cum * LOG2E), axis=1)
    aq_ref[...] = (_dot(pieces, eq_ref[...]) + oneq_ref[...]).astype(aq_ref.dtype)
    ak_ref[...] = (_dot(pieces, ek_ref[...]) + onek_ref[...]).astype(ak_ref.dtype)


def _fox_bias_columns(misc3, f_bias):
    b, t, _ = misc3.shape
    width = FOX_HEADS * FOX_HEAD_DIM
    blk = (IN_W["rq"] + IN_W["rk"] + IN_W["rg"] + IN_W["dc"]) // LANE
    bias = jnp.zeros((1, LANE), F32).at[0, SMALL_FF:SMALL_FF + FOX_HEADS].set(f_bias)
    eq = np.zeros((3 * LANE, width), np.float32)
    ek = np.zeros((3 * LANE, width), np.float32)
    oneq = np.zeros((1, width), np.float32)
    onek = np.zeros((1, width), np.float32)
    for h in range(FOX_HEADS):
        for piece in range(3):
            eq[piece * LANE + SMALL_FF + h, h * FOX_HEAD_DIM + piece] = 1.0
            ek[piece * LANE + SMALL_FF + h, h * FOX_HEAD_DIM + 3 + piece] = -1.0
            oneq[0, h * FOX_HEAD_DIM + 3 + piece] = 1.0
            onek[0, h * FOX_HEAD_DIM + piece] = 1.0
    const = lambda bi, i: (0, 0)
    out = pl.BlockSpec((None, BLOCK, width), lambda bi, i: (bi, i, 0))
    return pl.pallas_call(
        _cum_kernel,
        grid=(b, t // BLOCK),
        in_specs=[
            pl.BlockSpec((None, BLOCK, LANE), lambda bi, i: (bi, i, blk)),
            pl.BlockSpec((1, LANE), const),
            pl.BlockSpec((3 * LANE, width), const),
            pl.BlockSpec((3 * LANE, width), const),
            pl.BlockSpec((1, width), const),
            pl.BlockSpec((1, width), const),
        ],
        out_specs=[out, out],
        out_shape=[jax.ShapeDtypeStruct((b, t, width), BF16)] * 2,
        scratch_shapes=[pltpu.VMEM((1, LANE), F32)],
        compiler_params=_params(("parallel", "arbitrary")),
    )(misc3, bias, jnp.asarray(eq, BF16), jnp.asarray(ek, BF16), jnp.asarray(oneq), jnp.asarray(onek))


def _bit_transpose32(words):
    a = list(words)
    j, m = 16, 0x0000FFFF
    while j:
        mask = jnp.int32(m if m < 2 ** 31 else m - 2 ** 32)
        for k in range(32):
            if not k & j:
                t = (lax.shift_right_logical(a[k], jnp.int32(j)) ^ a[k + j]) & mask
                a[k + j] = a[k + j] ^ t
                a[k] = a[k] ^ (t << j)
        j >>= 1
        m = (m ^ (m << j)) & 0xFFFFFFFF
    return a


FAR_BLOCKS = 4


def _dsa_kernel(dq_ref, iq_ref, sm_ref, ik_ref, c_ref, ct_ref, wuk_ref, wuv_ref, bias_ref,
                o_ref, s_ref, qlat_ref, p_ref, m_ref, acc_ref, lg0_ref, lg1_ref, lgn_ref, iqt_ref, wt_ref,
                planes_ref, alive_ref, p1_ref, *, topk, nblk):
    i = pl.program_id(1)
    key_i = lax.broadcasted_iota(jnp.int32, (BLOCK, BLOCK), 0)
    qry_i = lax.broadcasted_iota(jnp.int32, (BLOCK, BLOCK), 1)
    hs = lambda h: slice(h * BLOCK, (h + 1) * BLOCK)

    iq_t = jnp.transpose(iq_ref[...].astype(F32)).astype(BF16)
    for h in range(IDX_HEADS):
        iqt_ref[0:IDX_DIM, hs(h)] = iq_t[h * IDX_DIM:(h + 1) * IDX_DIM, :]
    iqt_ref[IDX_DIM:LANE, :] = jnp.zeros((LANE - IDX_DIM, IDX_HEADS * BLOCK), BF16)
    wt_ref[...] = jnp.transpose(sm_ref[...])[SMALL_IW:SMALL_IW + IDX_HEADS, :] * (IDX_HEADS ** -0.5 * IDX_DIM ** -0.5)

    for h in range(DSA_HEADS):
        ql = _dot_nt(wuk_ref[h], dq_ref[:, h * DSA_HEAD_DIM:(h + 1) * DSA_HEAD_DIM]) * (DSA_HEAD_DIM ** -0.5 * LOG2E)
        qlat_ref[:, hs(h)] = ql.astype(BF16)

    ngrp = (i + SCORE_GROUP) // SCORE_GROUP
    npair = (ngrp + 1) // 2
    for u in range(1, 2 * SCORE_GROUP):
        s_ref[i + u] = jnp.full((BLOCK, BLOCK), INT_MIN, jnp.int32)

    def score_group(g, _):
        j0 = jnp.minimum(g * SCORE_GROUP, nblk - SCORE_GROUP)
        kt = ik_ref[pl.ds(pl.multiple_of(j0 * BLOCK, BLOCK), SCORE_GROUP * BLOCK), :]
        r = _dot(kt, iqt_ref[...])
        t = i * BLOCK + qry_i
        for u in range(SCORE_GROUP):
            acc = jnp.zeros((BLOCK, BLOCK), F32)
            for h in range(IDX_HEADS):
                acc = acc + jnp.maximum(r[u * BLOCK:(u + 1) * BLOCK, hs(h)], 0.0) * wt_ref[h:h + 1, :]
            s = (j0 + u) * BLOCK + key_i
            ok = (s <= t) & ((s >= PAD) | (s == t))
            bits = lax.bitcast_convert_type(acc, jnp.int32)
            bits = jnp.where(bits == INT_MIN, 0, bits)
            key = bits ^ ((bits >> 31) & 0x7FFFFFFF)
            s_ref[j0 + u] = jnp.where(ok, key, INT_MIN)
        return 0

    lax.fori_loop(0, ngrp, score_group, 0)

    def slice_group(g, _):
        for half in range(2):
            words = []
            for u in (2 * half, 2 * half + 1):
                tile = s_ref[g * SCORE_GROUP + u] ^ INT_MIN
                words += [tile[r * 8:(r + 1) * 8, :] for r in range(BLOCK // 8)]
            planes = _bit_transpose32(words)
            for bit in range(32):
                planes_ref[2 * g + half, bit] = planes[bit]
            planes_ref[2 * g + half, 32] = jnp.full((8, BLOCK), -1, jnp.int32)
        return 0

    lax.fori_loop(0, 2 * npair, slice_group, 0)
    alive_ref[...] = jnp.full(alive_ref.shape, -1, jnp.int32)

    def radix_pass(t, carry):
        thr, need, drop_prev = carry
        bit = 31 - t

        def body(pr, accs):
            accs = list(accs)
            for u in range(4):
                st = 4 * pr + u
                alive = alive_ref[st] & (planes_ref[st, bit + 1] ^ drop_prev)
                alive_ref[st] = alive
                accs[u] = accs[u] + lax.population_count(alive & planes_ref[st, bit])
            return tuple(accs)

        zero = jnp.zeros((8, BLOCK), jnp.int32)
        accs = lax.fori_loop(0, npair, body, (zero,) * 4)
        cnt = ((accs[0] + accs[1]) + (accs[2] + accs[3])).astype(F32)
        cnt = jnp.sum(cnt, axis=0, keepdims=True).astype(jnp.int32)
        take = cnt >= need
        thr = thr | jnp.where(take, jnp.int32(1) << bit, 0)
        need = jnp.where(take, need, need - cnt)
        return thr, need, jnp.where(take, 0, -1)

    zero = jnp.zeros((1, BLOCK), jnp.int32)
    thr, _, _ = lax.fori_loop(0, 32, radix_pass, (zero, zero + topk, zero))
    thr = jnp.maximum(thr ^ INT_MIN, INT_MIN + 1)

    m_ref[...] = jnp.full(m_ref.shape, NEG, F32)
    acc_ref[...] = jnp.zeros_like(acc_ref)

    def logits(j0, nb, dst_ref):
        w = nb * BLOCK
        dst_ref[0:w, :] = _dot(c_ref[pl.ds(pl.multiple_of(j0 * BLOCK, BLOCK), w), :], qlat_ref[...])

    def softmax_pv(j0, nb, lg_ref, pb_ref, bias_off=None):
        w = nb * BLOCK
        ckt = ct_ref[:, pl.ds(pl.multiple_of(j0 * BLOCK, BLOCK), w)]
        masked = jnp.where(s_ref[pl.ds(j0, nb)].reshape(w, BLOCK) >= thr, 0.0, NEG)
        alphas = []
        for h in range(DSA_HEADS):
            x = lg_ref[0:w, hs(h)] + masked
            if bias_off is not None:
                x = x + bias_ref[h, pl.ds(bias_off, w), :]
            m_prev = m_ref[h:h + 1, :]
            m_new = jnp.maximum(m_prev, jnp.max(x, axis=0, keepdims=True))
            alphas.append(jnp.exp2(m_prev - m_new))
            m_ref[h:h + 1, :] = m_new
            pb_ref[0:w, hs(h)] = jnp.exp2(x - m_new).astype(BF16)
        pv = _dot(ckt, pb_ref[0:w, :])
        for h in range(DSA_HEADS):
            acc_ref[:, hs(h)] = acc_ref[:, hs(h)] * alphas[h] + pv[:, hs(h)]

    near0 = jnp.maximum(i - 1, 0)
    logits(near0, 2, lgn_ref)

    n_far = jnp.maximum(i - 1, 0)
    n_big = n_far // FAR_BLOCKS
    big = lambda g: g * FAR_BLOCKS

    @pl.when(n_big > 0)
    def _():
        logits(0, FAR_BLOCKS, lg0_ref)

        def pair_of_tiles(g, _):
            logits(big(2 * g + 1), FAR_BLOCKS, lg1_ref)
            softmax_pv(big(2 * g), FAR_BLOCKS, lg0_ref, p_ref)
            logits(big(jnp.minimum(2 * g + 2, n_big - 1)), FAR_BLOCKS, lg0_ref)
            softmax_pv(big(2 * g + 1), FAR_BLOCKS, lg1_ref, p1_ref)
            return 0

        lax.fori_loop(0, n_big // 2, pair_of_tiles, 0)

        @pl.when(n_big % 2 == 1)
        def _():
            softmax_pv(big(n_big - 1), FAR_BLOCKS, lg0_ref, p_ref)

    def far_one(j, _):
        logits(j, 1, lg0_ref)
        softmax_pv(j, 1, lg0_ref, p_ref)
        return 0

    lax.fori_loop(n_big * FAR_BLOCKS, n_far, far_one, 0)
    softmax_pv(near0, 2, lgn_ref, p1_ref, bias_off=pl.multiple_of(jnp.where(i >= 1, 0, BLOCK), BLOCK))

    for h in range(DSA_HEADS):
        inv_l = 1.0 / acc_ref[DSA_LATENT:DSA_LATENT + 1, hs(h)]
        o_lat = (acc_ref[0:DSA_LATENT, hs(h)] * inv_l).astype(BF16)
        o = _dot_tn(o_lat, wuv_ref[h])
        o_ref[:, h * DSA_HEAD_DIM:(h + 1) * DSA_HEAD_DIM] = o.astype(o_ref.dtype)


def _dsa(pbf3, misc3, c3, ct3, wuk, wuv, bias, topk):
    b, t, _ = pbf3.shape
    nblk = t // BLOCK
    hq = DSA_HEADS * DSA_HEAD_DIM
    ik_blk = 6 * 1024 // LANE
    small_blk = (IN_W["rq"] + IN_W["rk"] + IN_W["rg"] + IN_W["dc"]) // LANE
    n_sets = 4 * (((nblk + SCORE_GROUP - 1) // SCORE_GROUP + 1) // 2)
    n_tiles = nblk + 2 * SCORE_GROUP - 1
    const3 = lambda bi, i: (0, 0, 0)
    return pl.pallas_call(
        functools.partial(_dsa_kernel, topk=topk, nblk=nblk),
        grid=(b, nblk),
        in_specs=[
            pl.BlockSpec((None, BLOCK, hq), lambda bi, i: (bi, i, 0)),
            pl.BlockSpec((None, BLOCK, IDX_HEADS * IDX_DIM), lambda bi, i: (bi, i, 1)),
            pl.BlockSpec((None, BLOCK, LANE), lambda bi, i: (bi, i, small_blk)),
            pl.BlockSpec((None, t, LANE), lambda bi, i: (bi, 0, ik_blk)),
            pl.BlockSpec((None, t, DSA_LATENT), lambda bi, i: (bi, 0, 0)),
            pl.BlockSpec((None, DSA_LATENT + 8, t), lambda bi, i: (bi, 0, 0)),
            pl.BlockSpec((DSA_HEADS, DSA_LATENT, DSA_HEAD_DIM), const3),
            pl.BlockSpec((DSA_HEADS, DSA_LATENT, DSA_HEAD_DIM), const3),
            pl.BlockSpec((DSA_HEADS, 3 * BLOCK, BLOCK), const3),
        ],
        out_specs=pl.BlockSpec((None, BLOCK, hq), lambda bi, i: (bi, i, 0)),
        out_shape=jax.ShapeDtypeStruct((b, t, hq), BF16),
        scratch_shapes=[
            pltpu.VMEM((n_tiles, BLOCK, BLOCK), jnp.int32),
            pltpu.VMEM((DSA_LATENT, DSA_HEADS * BLOCK), BF16),
            pltpu.VMEM((FAR_BLOCKS * BLOCK, DSA_HEADS * BLOCK), BF16),
            pltpu.VMEM((DSA_HEADS, BLOCK), F32),
            pltpu.VMEM((DSA_LATENT + 8, DSA_HEADS * BLOCK), F32),
            pltpu.VMEM((FAR_BLOCKS * BLOCK, DSA_HEADS * BLOCK), F32),
            pltpu.VMEM((FAR_BLOCKS * BLOCK, DSA_HEADS * BLOCK), F32),
            pltpu.VMEM((2 * BLOCK, DSA_HEADS * BLOCK), F32),
            pltpu.VMEM((LANE, IDX_HEADS * BLOCK), BF16),
            pltpu.VMEM((IDX_HEADS, BLOCK), F32),
            pltpu.VMEM((n_sets, 33, 8, BLOCK), jnp.int32),
            pltpu.VMEM((n_sets, 8, BLOCK), jnp.int32),
            pltpu.VMEM((FAR_BLOCKS * BLOCK, DSA_HEADS * BLOCK), BF16),
        ],
        compiler_params=_params(("parallel", "arbitrary")),
    )(pbf3, pbf3, misc3, pbf3, c3, ct3, wuk, wuv, bias)


def _dsa_weights(w_uk, w_uv):
    return jnp.transpose(w_uk, (1, 0, 2)).astype(BF16), jnp.transpose(w_uv, (1, 0, 2)).astype(BF16)


FOX_ROWS = 64


def _fox_kernel(q_ref, qa_ref, k_ref, ka_ref, v_ref, o_ref, m_ref, l_ref, acc_ref, lg_ref, p_ref, *, tq):
    i = pl.program_id(2)
    qq = jnp.concatenate([q_ref[...], qa_ref[...]], axis=1)
    m_ref[...] = jnp.full(m_ref.shape, NEG, F32)
    l_ref[...] = jnp.zeros_like(l_ref)
    acc_ref[...] = jnp.zeros_like(acc_ref)
    rows = min(FOX_ROWS, tq)
    chunks = [slice(c, c + rows) for c in range(0, tq, rows)]

    def qk(j, slot):
        off = pl.multiple_of(j * tq, tq)
        kk = jnp.concatenate([k_ref[pl.ds(off, tq), :], ka_ref[pl.ds(off, tq), :]], axis=1)
        lg_ref[slot] = _dot_nt(kk, qq)

    def softmax_pv(j, slot, mode):
        def logits(rs):
            x = lg_ref[slot, rs, :]
            if mode == "none":
                return x
            s = rs.start + lax.broadcasted_iota(jnp.int32, (rows, 1), 0)
            if mode == "pad":
                return jnp.where(s >= PAD, x, NEG)
            t = lax.broadcasted_iota(jnp.int32, (1, tq), 1)
            return jnp.where((s <= t) & ((s + j * tq >= PAD) | (s == t)), x, NEG)

        fold = lambda a: a.reshape(rows // 8, 8, tq)
        mx = jnp.full((8, tq), NEG, F32)
        for rs in chunks:
            mx = jnp.maximum(mx, jnp.max(fold(logits(rs)), axis=0))
        m_prev = m_ref[...]
        m_new = jnp.maximum(m_prev, jnp.max(mx, axis=0, keepdims=True))
        alpha = jnp.exp2(m_prev - m_new)
        tot = jnp.zeros((8, tq), F32)
        for rs in chunks:
            p = jnp.exp2(logits(rs) - m_new)
            tot = tot + jnp.sum(fold(p), axis=0)
            p_ref[rs, :] = p.astype(BF16)
        l_ref[...] = alpha * l_ref[...] + jnp.sum(tot, axis=0, keepdims=True)
        off = pl.multiple_of(j * tq, tq)
        acc_ref[...] = acc_ref[...] * alpha + _dot_tn(v_ref[pl.ds(off, tq), :], p_ref[...])
        m_ref[...] = m_new

    qk(i, 2)

    @pl.when(i > 0)
    def _():
        qk(0, 0)
        qk(jnp.minimum(1, i), 1)
        softmax_pv(0, 0, "pad")
        n_mid = i - 1

        def pair(g, _):
            a = 2 * g + 1
            qk(jnp.minimum(a + 1, i), 0)
            softmax_pv(a, 1, "none")
            qk(jnp.minimum(a + 2, i), 1)
            softmax_pv(a + 1, 0, "none")
            return 0

        lax.fori_loop(0, n_mid // 2, pair, 0)

        @pl.when(n_mid % 2 == 1)
        def _():
            softmax_pv(i - 1, 1, "none")

    softmax_pv(i, 2, "diag")

    o_ref[...] = jnp.transpose(acc_ref[...] * (1.0 / l_ref[...])).astype(o_ref.dtype)


def _fox(pbf3, aug_q, aug_k):
    b, t, _ = pbf3.shape
    nblk = t // BLOCK
    tq = BLOCK * max(k for k in (1, 5) if nblk % k == 0)
    d = FOX_HEAD_DIM
    qb, kb, vb = 2 * 1024 // d, 3 * 1024 // d, 4 * 1024 // d
    return pl.pallas_call(
        functools.partial(_fox_kernel, tq=tq),
        grid=(b, FOX_HEADS, t // tq),
        in_specs=[
            pl.BlockSpec((None, tq, d), lambda bi, h, i: (bi, i, qb + h)),
            pl.BlockSpec((None, tq, d), lambda bi, h, i: (bi, i, h)),
            pl.BlockSpec((None, t, d), lambda bi, h, i: (bi, 0, kb + h)),
            pl.BlockSpec((None, t, d), lambda bi, h, i: (bi, 0, h)),
            pl.BlockSpec((None, t, d), lambda bi, h, i: (bi, 0, vb + h)),
        ],
        out_specs=pl.BlockSpec((None, tq, d), lambda bi, h, i: (bi, i, h)),
        out_shape=jax.ShapeDtypeStruct((b, t, FOX_HEADS * d), BF16),
        scratch_shapes=[pltpu.VMEM((1, tq), F32), pltpu.VMEM((1, tq), F32), pltpu.VMEM((d, tq), F32),
                        pltpu.VMEM((3, tq, tq), F32), pltpu.VMEM((tq, tq), BF16)],
        compiler_params=_params(("parallel", "parallel", "arbitrary")),
    )(pbf3, aug_q, pbf3, aug_k, pbf3)


def _ret_kernel(rq_ref, rk_ref, rv_ref, rg_ref, cos_ref, sin_ref, dq_ref, dk_ref, decay_ref, cd_ref, gn_ref,
                o_ref, state_ref):
    i = pl.program_id(1)

    @pl.when(i == 0)
    def _():
        state_ref[...] = jnp.zeros_like(state_ref)

    half = RET_HEADS * RET_QK_DIM // 2
    cos = cos_ref[...]
    sin = sin_ref[...]

    def rope(x):
        x1, x2 = x[:, :half], x[:, half:]
        return jnp.concatenate([x1 * cos - x2 * sin, x1 * sin + x2 * cos], axis=1)

    q = rope(rq_ref[...])
    k = rope(rk_ref[...]) * RET_QK_DIM ** -0.5
    pos = i * BLOCK + lax.broadcasted_iota(jnp.int32, (BLOCK, 1), 0)
    k = jnp.where(pos >= PAD, k, 0.0)
    qb = q.astype(BF16)
    kb = k.astype(BF16)
    q_dec = (q * dq_ref[...]).astype(BF16)
    k_dec = (k * dk_ref[...]).astype(BF16)
    head_of_col = (lax.broadcasted_iota(jnp.int32, (1, 2 * half), 1) % half) // (RET_QK_DIM // 2)
    zero = jnp.zeros((), BF16)
    for h in range(RET_HEADS):
        mine = head_of_col == h
        vh = rv_ref[:, h * RET_V_DIM:(h + 1) * RET_V_DIM]
        s = (_dot_nt(jnp.where(mine, qb, zero), kb) * decay_ref[h]).astype(BF16)
        st = state_ref[h]
        o = _dot(s, vh) + _dot(jnp.where(mine, q_dec, zero), st.astype(BF16))
        state_ref[h] = cd_ref[h] * st + _dot_tn(k_dec, vh)
        mu = jnp.mean(o, axis=1, keepdims=True)
        var = jnp.mean(jnp.square(o - mu), axis=1, keepdims=True)
        sl = slice(h * RET_V_DIM, (h + 1) * RET_V_DIM)
        o = (o - mu) * lax.rsqrt(var + GN_EPS) * gn_ref[:, sl]
        g = rg_ref[:, sl]
        o_ref[:, sl] = (g * jax.nn.sigmoid(g) * o).astype(o_ref.dtype)


def _retention(misc3, pbf3, cos, sin, dq_tab, dk_tab, decay, cd, gn_w):
    b, t, _ = misc3.shape
    qk = RET_HEADS * RET_QK_DIM
    vw = RET_HEADS * RET_V_DIM
    const2 = lambda bi, i: (0, 0)
    const3 = lambda bi, i: (0, 0, 0)
    return pl.pallas_call(
        _ret_kernel,
        grid=(b, t // BLOCK),
        in_specs=[
            pl.BlockSpec((None, BLOCK, qk), lambda bi, i: (bi, i, 0)),
            pl.BlockSpec((None, BLOCK, qk), lambda bi, i: (bi, i, 1)),
            pl.BlockSpec((None, BLOCK, vw), lambda bi, i: (bi, i, 5)),
            pl.BlockSpec((None, BLOCK, vw), lambda bi, i: (bi, i, 1)),
            pl.BlockSpec((BLOCK, qk // 2), lambda bi, i: (i, 0)),
            pl.BlockSpec((BLOCK, qk // 2), lambda bi, i: (i, 0)),
            pl.BlockSpec((BLOCK, qk), const2),
            pl.BlockSpec((BLOCK, qk), const2),
            pl.BlockSpec((RET_HEADS, BLOCK, BLOCK), const3),
            pl.BlockSpec((RET_HEADS, 1, RET_V_DIM), const3),
            pl.BlockSpec((1, vw), const2),
        ],
        out_specs=pl.BlockSpec((None, BLOCK, vw), lambda bi, i: (bi, i, 0)),
        out_shape=jax.ShapeDtypeStruct((b, t, vw), BF16),
        scratch_shapes=[pltpu.VMEM((RET_HEADS, qk, RET_V_DIM), F32)],
        compiler_params=_params(("parallel", "arbitrary")),
    )(misc3, misc3, pbf3, misc3, cos, sin, dq_tab, dk_tab, decay, cd, gn_w.reshape(1, vw))


def _merge_kernel(x_ref, g_ref, b0_ref, b1_ref, b2_ref, wg0_ref, wg1_ref, wg2_ref, wb_ref, o_ref, xn_ref):
    @pl.when(pl.program_id(1) == 0)
    def _():
        xn_ref[...] = _rms(x_ref[...], g_ref[...]).astype(BF16)

    xn = xn_ref[...]
    acc = jax.nn.sigmoid(_dot(xn, wg0_ref[...])) * _dot(b0_ref[...], wb_ref[0])
    acc = acc + jax.nn.sigmoid(_dot(xn, wg1_ref[...])) * _dot(b1_ref[...], wb_ref[1])
    acc = acc + jax.nn.sigmoid(_dot(xn, wg2_ref[...])) * _dot(b2_ref[...], wb_ref[2])
    o_ref[...] = acc.astype(o_ref.dtype)


def _merge(x, g, b0, b1, b2, w_gates, w_branch):
    n, bw = b0.shape
    dm = x.shape[1]
    d = w_branch.shape[2]
    tm = _pick_tile(n, 640)
    tn = _pick_tile(d, 512, LANE)
    nj = d // tn
    bspec = pl.BlockSpec((tm, bw), lambda i, j: (i, 0))
    gspec = lambda r: pl.BlockSpec((dm, tn), lambda i, j: (0, r * nj + j))
    return pl.pallas_call(
        _merge_kernel,
        grid=(n // tm, nj),
        in_specs=[pl.BlockSpec((tm, dm), lambda i, j: (i, 0)), pl.BlockSpec((1, dm), lambda i, j: (0, 0)),
                  bspec, bspec, bspec, gspec(0), gspec(1), gspec(2),
                  pl.BlockSpec((N_BRANCH, bw, tn), lambda i, j: (0, 0, j))],
        out_specs=pl.BlockSpec((tm, tn), lambda i, j: (i, j)),
        out_shape=jax.ShapeDtypeStruct((n, d), BF16),
        scratch_shapes=[pltpu.VMEM((tm, dm), BF16)],
        compiler_params=_params(("parallel", "arbitrary")),
    )(x, g.reshape(1, dm), b0, b1, b2, w_gates, w_gates, w_gates, w_branch)


def _resproj_kernel(a_ref, w_ref, r_ref, o_ref):
    o_ref[...] = r_ref[...] + _dot(a_ref[...], w_ref[...])


def _resproj(a, w, res):
    n, k = a.shape
    d = w.shape[1]
    tm = _pick_tile(n, 1280)
    tn = _pick_tile(d, 1024, LANE)
    return pl.pallas_call(
        _resproj_kernel,
        grid=(n // tm, d // tn),
        in_specs=[
            pl.BlockSpec((tm, k), lambda i, j: (i, 0)),
            pl.BlockSpec((k, tn), lambda i, j: (0, j)),
            pl.BlockSpec((tm, tn), lambda i, j: (i, j)),
        ],
        out_specs=pl.BlockSpec((tm, tn), lambda i, j: (i, j)),
        out_shape=jax.ShapeDtypeStruct((n, d), F32),
        compiler_params=_params(("parallel", "arbitrary")),
    )(a, w, res)


def _t5_bucket(dist):
    d = jnp.maximum(dist, 1).astype(F32)
    large = T5_MAX_EXACT + (jnp.log(d / T5_MAX_EXACT) / math.log(T5_MAX_DIST / T5_MAX_EXACT)
                            * (T5_BUCKETS - T5_MAX_EXACT)).astype(jnp.int32)
    large = jnp.minimum(large, T5_BUCKETS - 1)
    return jnp.where(dist < T5_MAX_EXACT, dist, large)


def _t5_bias_tiles(t5_table):
    k = jnp.arange(BLOCK, dtype=jnp.int32)[:, None]
    q = jnp.arange(BLOCK, dtype=jnp.int32)[None, :]
    table = t5_table.astype(F32)
    rel = table - table[T5_BUCKETS - 1]

    def lookup(dist):
        onehot = (_t5_bucket(dist)[..., None] == jnp.arange(T5_BUCKETS, dtype=jnp.int32)).astype(F32)
        return jnp.einsum("kqb,bh->hkq", onehot, rel, precision=lax.Precision.HIGHEST)

    bdiag = lookup(jnp.maximum(q - k, 0))
    bprev = lookup(BLOCK + q - k)
    return jnp.concatenate([bprev, bdiag, jnp.zeros_like(bdiag)], axis=1) * LOG2E


def _rope_tables(t):
    half = RET_QK_DIM // 2
    inv = ROPE_BASE ** (-jnp.arange(half, dtype=F32) / half)
    ang = jnp.arange(t, dtype=jnp.int32).astype(F32)[:, None] * inv[None, :]
    return jnp.tile(jnp.cos(ang), (1, RET_HEADS)), jnp.tile(jnp.sin(ang), (1, RET_HEADS))


def _decay_tables():
    log_gamma = jnp.log1p(-(2.0 ** (-5.0 - jnp.arange(RET_HEADS, dtype=F32))))
    i = jnp.arange(BLOCK, dtype=F32)
    gap = i[:, None] - i[None, :]
    decay = jnp.where(gap[None] >= 0, jnp.exp(jnp.maximum(gap, 0.0)[None] * log_gamma[:, None, None]), 0.0)
    head_of_col = (np.arange(RET_HEADS * RET_QK_DIM) % (RET_HEADS * RET_QK_DIM // 2)) // (RET_QK_DIM // 2)
    lg_col = log_gamma[head_of_col]
    dk_tab = jnp.exp((BLOCK - 1 - i)[:, None] * lg_col[None, :])
    dq_tab = jnp.exp((i + 1.0)[:, None] * lg_col[None, :])
    cd = jnp.broadcast_to(jnp.exp(BLOCK * log_gamma)[:, None, None], (RET_HEADS, 1, RET_V_DIM))
    return decay.astype(F32), dq_tab, dk_tab, cd


def _split_w_in(w16):
    cols = lambda name: w16[:, IN_OFF[name]:IN_OFF[name] + IN_W[name]]
    zeros = lambda w: jnp.zeros((w16.shape[0], w), BF16)
    rope = lambda a: jnp.swapaxes(a.reshape(-1, RET_HEADS, 2, RET_QK_DIM // 2), 1, 2).reshape(a.shape)
    w_bf = jnp.concatenate([cols("dq"), cols("iq"), cols("fq"), cols("fk"), cols("fv"), cols("rv"),
                            cols("ik"), zeros(LANE - IDX_DIM)], axis=1)
    w_misc = jnp.concatenate([rope(cols("rq")), rope(cols("rk")), cols("rg"), cols("dc"),
                              cols("iw"), cols("ff"), zeros(LANE - IDX_HEADS - FOX_HEADS),
                              zeros(LANE)], axis=1)
    assert w_bf.shape[1] == BF_COLS and w_misc.shape[1] == MISC_COLS
    return w_bf, w_misc, cols("gates")


def _bf_col_scale():
    scale = np.ones((BF_COLS,), np.float32)
    scale[2 * 1024:3 * 1024] = FOX_HEAD_DIM ** -0.5 * LOG2E
    return jnp.asarray(scale)


def _mixer(x, b, t, mix_norm, w_in16, kv_norm, w_uk, w_uv, f_bias, gn_w, w_branch16, w_out16, tables, topk):
    n = b * t
    bias_tiles, rope_tabs, decay_tabs = tables
    w_bf, w_misc, w_gates = _split_w_in(w_in16)
    pbf, misc, c = _proj(x, mix_norm, w_bf, _bf_col_scale(), w_misc, kv_norm)
    pbf3 = pbf.reshape(b, t, BF_COLS)
    misc3 = misc.reshape(b, t, MISC_COLS)

    c3 = c.reshape(b, t, DSA_LATENT)
    wuk, wuv = _dsa_weights(w_uk, w_uv)
    ones_rows = jnp.zeros((b, 8, t), BF16).at[:, 0].set(1)
    ct3 = jnp.concatenate([jnp.swapaxes(c3, 1, 2), ones_rows], axis=1)
    dsa_o = _dsa(pbf3, misc3, c3, ct3, wuk, wuv, bias_tiles, topk)

    fox_o = _fox(pbf3, *_fox_bias_columns(misc3, f_bias))

    ret_o = _retention(misc3, pbf3, *rope_tabs, *decay_tabs, gn_w)

    merged = _merge(x, mix_norm, dsa_o.reshape(n, -1), fox_o.reshape(n, -1), ret_o.reshape(n, -1), w_gates,
                    w_branch16)
    return _resproj(merged, w_out16, x)


def kernel(x, meta_tokens, t5_table, ffn1_norm, ffn1_w1, ffn1_w3, ffn1_w2, mix_norm, w_in, dsa_kv_norm,
           dsa_w_uk, dsa_w_uv, fox_f_bias, ret_gn_w, w_branch, w_out, ffn2_norm, ffn2_w1, ffn2_w3, ffn2_w2,
           final_norm):
    b, seq, d = x.shape
    assert d == D_MODEL and seq % BLOCK == 0
    t = seq + BLOCK
    depth = ffn1_norm.shape[0]
    topk = min(INDEX_TOPK, seq // 4)
    h = jnp.concatenate([
        jnp.zeros((b, PAD, d), x.dtype),
        jnp.broadcast_to(meta_tokens.astype(x.dtype)[None], (b, N_META, d)),
        x,
    ], axis=1).reshape(b * t, d)
    cos, sin = _rope_tables(t)
    decay, dq_tab, dk_tab, cd = _decay_tables()
    tables = (_t5_bias_tiles(t5_table), (cos, sin), (dq_tab, dk_tab, decay, cd))
    bf = lambda w, l: w[l].astype(BF16)
    w_in16 = lax.optimization_barrier(w_in.astype(BF16))
    for l in range(depth):
        h = _ffn(h, ffn1_norm[l], bf(ffn1_w1, l), bf(ffn1_w3, l), bf(ffn1_w2, l))
        h = _mixer(h, b, t, mix_norm[l], w_in16[l], dsa_kv_norm[l], dsa_w_uk[l], dsa_w_uv[l],
                   fox_f_bias[l], ret_gn_w[l], bf(w_branch, l), bf(w_out, l), tables, topk)
        h = _ffn(h, ffn2_norm[l], bf(ffn2_w1, l), bf(ffn2_w3, l), bf(ffn2_w2, l),
                 final_g=final_norm if l == depth - 1 else None)
    return h.reshape(b, t, d)[:, BLOCK:]
```

```python
import functools
import math

import numpy as np
import jax
import jax.numpy as jnp
from jax import lax
from jax.experimental import pallas as pl
from jax.experimental.pallas import tpu as pltpu

F32 = jnp.float32
BF16 = jnp.bfloat16

LANE = 128
VMEM_LIMIT = 56 * 1024 * 1024

D_MODEL = 2048
N_META = 16
BLOCK = 128
PAD = BLOCK - N_META
RMS_EPS = 1e-6
GN_EPS = 1e-5
NEG = -1e30

DSA_HEADS = 8
DSA_HEAD_DIM = 128
DSA_LATENT = 256
IDX_HEADS = 16
IDX_DIM = 64
INDEX_TOPK = 256
FOX_HEADS = 8
FOX_HEAD_DIM = 128
RET_HEADS = 8
RET_QK_DIM = 64
RET_V_DIM = 128
ROPE_BASE = 10000.0
N_BRANCH = 3
T5_BUCKETS = 32
T5_MAX_EXACT = 16
T5_MAX_DIST = 128

IN_WIDTHS = (1024, 256, 1024, 64, 16, 1024, 1024, 1024, 8, 512, 512, 1024, 1024, 6144)
IN_NAMES = ("dq", "dc", "iq", "ik", "iw", "fq", "fk", "fv", "ff", "rq", "rk", "rv", "rg", "gates")
IN_OFF = dict(zip(IN_NAMES, np.cumsum((0,) + IN_WIDTHS[:-1]).tolist()))
IN_W = dict(zip(IN_NAMES, IN_WIDTHS))

INT_MIN = -2 ** 31
LOG2E = math.log2(math.e)
SCORE_GROUP = 4

BF_COLS = 6 * 1024 + LANE
MISC_COLS = 2560
SMALL_IW = 0
SMALL_FF = IDX_HEADS


def _dot(a, b):
    return jnp.dot(a, b, preferred_element_type=F32)


def _dot_nt(a, b):
    return lax.dot_general(a, b, (((1,), (1,)), ((), ())), preferred_element_type=F32)


def _dot_tn(a, b):
    return lax.dot_general(a, b, (((0,), (0,)), ((), ())), preferred_element_type=F32)


def _pick_tile(n, target, mult=8):
    best = None
    for t in range(mult, min(n, target) + 1, mult):
        if n % t == 0:
            best = t
    assert best is not None, (n, target, mult)
    return best


def _params(sem):
    return pltpu.CompilerParams(dimension_semantics=sem, vmem_limit_bytes=VMEM_LIMIT)


def _rms(x, g):
    return x * lax.rsqrt(jnp.mean(x * x, axis=-1, keepdims=True) + RMS_EPS) * g


def _ffn_kernel(*refs, nj, final):
    if final:
        x_ref, g_ref, w1_ref, w3_ref, w2_ref, fg_ref, o_ref, xn_ref, acc_ref = refs
    else:
        x_ref, g_ref, w1_ref, w3_ref, w2_ref, o_ref, xn_ref, acc_ref = refs
    j = pl.program_id(1)

    @pl.when(j == 0)
    def _():
        xn_ref[...] = _rms(x_ref[...], g_ref[...]).astype(BF16)
        acc_ref[...] = jnp.zeros_like(acc_ref)

    xn = xn_ref[...]
    h1 = _dot(xn, w1_ref[...])
    h3 = _dot(xn, w3_ref[...])
    g = (h1 * jax.nn.sigmoid(h1) * h3).astype(BF16)
    acc_ref[...] += _dot(g, w2_ref[...])

    @pl.when(j == nj - 1)
    def _():
        y = x_ref[...] + 0.5 * acc_ref[...]
        if final:
            y = _rms(y, fg_ref[...])
        o_ref[...] = y


def _ffn(x, g, w1, w3, w2, final_g=None):
    n, d = x.shape
    f = w1.shape[1]
    tm = _pick_tile(n, 640)
    tf = _pick_tile(f, 512, LANE)
    nj = f // tf
    final = final_g is not None
    in_specs = [
        pl.BlockSpec((tm, d), lambda i, j: (i, 0)),
        pl.BlockSpec((1, d), lambda i, j: (0, 0)),
        pl.BlockSpec((d, tf), lambda i, j: (0, j)),
        pl.BlockSpec((d, tf), lambda i, j: (0, j)),
        pl.BlockSpec((tf, d), lambda i, j: (j, 0)),
    ]
    args = [x, g.reshape(1, d), w1, w3, w2]
    if final:
        in_specs.append(pl.BlockSpec((1, d), lambda i, j: (0, 0)))
        args.append(final_g.reshape(1, d))
    return pl.pallas_call(
        functools.partial(_ffn_kernel, nj=nj, final=final),
        grid=(n // tm, nj),
        in_specs=in_specs,
        out_specs=pl.BlockSpec((tm, d), lambda i, j: (i, 0)),
        out_shape=jax.ShapeDtypeStruct((n, d), F32),
        scratch_shapes=[pltpu.VMEM((tm, d), BF16), pltpu.VMEM((tm, d), F32)],
        compiler_params=_params(("parallel", "arbitrary")),
    )(*args)


def _proj_kernel(x_ref, g_ref, wa_ref, sa_ref, wb_ref, kvn_ref, oa_ref, ob_ref, oc_ref, xn_ref, *, na, dc_tile):
    j = pl.program_id(1)

    @pl.when(j == 0)
    def _():
        xn_ref[...] = _rms(x_ref[...], g_ref[...]).astype(BF16)

    @pl.when(j < na)
    def _():
        oa_ref[...] = (_dot(xn_ref[...], wa_ref[...]) * sa_ref[...]).astype(oa_ref.dtype)

    @pl.when(j >= na)
    def _():
        r = _dot(xn_ref[...], wb_ref[...])
        ob_ref[...] = r

        @pl.when(j == na + dc_tile)
        def _():
            oc_ref[...] = _rms(r[:, 0:DSA_LATENT], kvn_ref[...]).astype(oc_ref.dtype)


def _proj(x, g, w_bf, bf_scale, w_misc, kv_norm):
    n, d = x.shape
    ca, cb = w_bf.shape[1], w_misc.shape[1]
    tm = _pick_tile(n, 1280)
    ta = _pick_tile(ca, 1024, LANE)
    tb = _pick_tile(cb, 512, LANE)
    na, nb = ca // ta, cb // tb
    dc_col = IN_W["rq"] + IN_W["rk"] + IN_W["rg"]
    assert dc_col % tb == 0 and tb >= DSA_LATENT
    a_idx = lambda j: jnp.minimum(j, na - 1)
    b_idx = lambda j: jnp.maximum(j - na, 0)
    return pl.pallas_call(
        functools.partial(_proj_kernel, na=na, dc_tile=dc_col // tb),
        grid=(n // tm, na + nb),
        in_specs=[
            pl.BlockSpec((tm, d), lambda i, j: (i, 0)),
            pl.BlockSpec((1, d), lambda i, j: (0, 0)),
            pl.BlockSpec((d, ta), lambda i, j: (0, a_idx(j))),
            pl.BlockSpec((1, ta), lambda i, j: (0, a_idx(j))),
            pl.BlockSpec((d, tb), lambda i, j: (0, b_idx(j))),
            pl.BlockSpec((1, DSA_LATENT), lambda i, j: (0, 0)),
        ],
        out_specs=[
            pl.BlockSpec((tm, ta), lambda i, j: (i, a_idx(j))),
            pl.BlockSpec((tm, tb), lambda i, j: (i, b_idx(j))),
            pl.BlockSpec((tm, DSA_LATENT), lambda i, j: (i, 0)),
        ],
        out_shape=[jax.ShapeDtypeStruct((n, ca), BF16), jax.ShapeDtypeStruct((n, cb), F32),
                   jax.ShapeDtypeStruct((n, DSA_LATENT), BF16)],
        scratch_shapes=[pltpu.VMEM((tm, d), BF16)],
        compiler_params=_params(("parallel", "arbitrary")),
    )(x, g.reshape(1, d), w_bf, bf_scale.reshape(1, ca), w_misc, kv_norm.reshape(1, DSA_LATENT))


def _split3(x):
    hi = x.astype(BF16)
    r = x - hi.astype(F32)
    mid = r.astype(BF16)
    lo = (r - mid.astype(F32)).astype(BF16)
    return hi, mid, lo


def _cum_kernel(s_ref, b_ref, eq_ref, ek_ref, oneq_ref, onek_ref, aq_ref, ak_ref, carry_ref):
    @pl.when(pl.program_id(1) == 0)
    def _():
        carry_ref[...] = jnp.zeros_like(carry_ref)

    z = s_ref[...] + b_ref[...]
    lf = jnp.minimum(z, 0.0) - jnp.log1p(jnp.exp(-jnp.abs(z)))
    r = lax.broadcasted_iota(jnp.int32, (BLOCK, BLOCK), 0)
    c = lax.broadcasted_iota(jnp.int32, (BLOCK, BLOCK), 1)
    tri = jnp.where(c <= r, 1.0, 0.0).astype(BF16)
    hi, mid, lo = _split3(lf)
    cum = (_dot(tri, lo) + _dot(tri, mid)) + _dot(tri, hi) + carry_ref[...]
    carry_ref[...] = cum[BLOCK - 1:BLOCK, :]
    pieces = jnp.concatenate(_split3(cum * LOG2E), axis=1)
    aq_ref[...] = (_dot(pieces, eq_ref[...]) + oneq_ref[...]).astype(aq_ref.dtype)
    ak_ref[...] = (_dot(pieces, ek_ref[...]) + onek_ref[...]).astype(ak_ref.dtype)


def _fox_bias_columns(misc3, f_bias):
    b, t, _ = misc3.shape
    width = FOX_HEADS * FOX_HEAD_DIM
    blk = (IN_W["rq"] + IN_W["rk"] + IN_W["rg"] + IN_W["dc"]) // LANE
    bias = jnp.zeros((1, LANE), F32).at[0, SMALL_FF:SMALL_FF + FOX_HEADS].set(f_bias)
    eq = np.zeros((3 * LANE, width), np.float32)
    ek = np.zeros((3 * LANE, width), np.float32)
    oneq = np.zeros((1, width), np.float32)
    onek = np.zeros((1, width), np.float32)
    for h in range(FOX_HEADS):
        for piece in range(3):
            eq[piece * LANE + SMALL_FF + h, h * FOX_HEAD_DIM + piece] = 1.0
            ek[piece * LANE + SMALL_FF + h, h * FOX_HEAD_DIM + 3 + piece] = -1.0
            oneq[0, h * FOX_HEAD_DIM + 3 + piece] = 1.0
            onek[0, h * FOX_HEAD_DIM + piece] = 1.0
    const = lambda bi, i: (0, 0)
    out = pl.BlockSpec((None, BLOCK, width), lambda bi, i: (bi, i, 0))
    return pl.pallas_call(
        _cum_kernel,
        grid=(b, t // BLOCK),
        in_specs=[
            pl.BlockSpec((None, BLOCK, LANE), lambda bi, i: (bi, i, blk)),
            pl.BlockSpec((1, LANE), const),
            pl.BlockSpec((3 * LANE, width), const),
            pl.BlockSpec((3 * LANE, width), const),
            pl.BlockSpec((1, width), const),
            pl.BlockSpec((1, width), const),
        ],
        out_specs=[out, out],
        out_shape=[jax.ShapeDtypeStruct((b, t, width), BF16)] * 2,
        scratch_shapes=[pltpu.VMEM((1, LANE), F32)],
        compiler_params=_params(("parallel", "arbitrary")),
    )(misc3, bias, jnp.asarray(eq, BF16), jnp.asarray(ek, BF16), jnp.asarray(oneq), jnp.asarray(onek))


def _bit_transpose32(words):
    a = list(words)
    j, m = 16, 0x0000FFFF
    while j:
        mask = jnp.int32(m if m < 2 ** 31 else m - 2 ** 32)
        for k in range(32):
            if not k & j:
                t = (lax.shift_right_logical(a[k], jnp.int32(j)) ^ a[k + j]) & mask
                a[k + j] = a[k + j] ^ t
                a[k] = a[k] ^ (t << j)
        j >>= 1
        m = (m ^ (m << j)) & 0xFFFFFFFF
    return a


FAR_BLOCKS = 4
DSA_ROWS = 64


def _dsa_kernel(dq_ref, iq_ref, sm_ref, ik_ref, c_ref, ct_ref, wuk_ref, wuv_ref, bias_ref,
                o_ref, s_ref, qlat_ref, p_ref, m_ref, acc_ref, lg0_ref, lg1_ref, lgn_ref, iqt_ref, wt_ref,
                planes_ref, alive_ref, p1_ref, mask_ref, *, topk, nblk):
    i = pl.program_id(1)
    key_i = lax.broadcasted_iota(jnp.int32, (BLOCK, BLOCK), 0)
    qry_i = lax.broadcasted_iota(jnp.int32, (BLOCK, BLOCK), 1)
    hs = lambda h: slice(h * BLOCK, (h + 1) * BLOCK)

    iq_t = jnp.transpose(iq_ref[...].astype(F32)).astype(BF16)
    for h in range(IDX_HEADS):
        iqt_ref[0:IDX_DIM, hs(h)] = iq_t[h * IDX_DIM:(h + 1) * IDX_DIM, :]
    iqt_ref[IDX_DIM:LANE, :] = jnp.zeros((LANE - IDX_DIM, IDX_HEADS * BLOCK), BF16)
    wt_ref[...] = jnp.transpose(sm_ref[...])[SMALL_IW:SMALL_IW + IDX_HEADS, :] * (IDX_HEADS ** -0.5 * IDX_DIM ** -0.5)

    for h in range(DSA_HEADS):
        ql = _dot_nt(wuk_ref[h], dq_ref[:, h * DSA_HEAD_DIM:(h + 1) * DSA_HEAD_DIM]) * (DSA_HEAD_DIM ** -0.5 * LOG2E)
        qlat_ref[:, hs(h)] = ql.astype(BF16)

    ngrp = (i + SCORE_GROUP) // SCORE_GROUP
    npair = (ngrp + 1) // 2
    for u in range(1, 2 * SCORE_GROUP):
        s_ref[i + u] = jnp.full((BLOCK, BLOCK), INT_MIN, jnp.int32)

    def score_group(g, _):
        j0 = jnp.minimum(g * SCORE_GROUP, nblk - SCORE_GROUP)
        kt = ik_ref[pl.ds(pl.multiple_of(j0 * BLOCK, BLOCK), SCORE_GROUP * BLOCK), :]
        r = _dot(kt, iqt_ref[...])
        t = i * BLOCK + qry_i
        for u in range(SCORE_GROUP):
            acc = jnp.zeros((BLOCK, BLOCK), F32)
            for h in range(IDX_HEADS):
                acc = acc + jnp.maximum(r[u * BLOCK:(u + 1) * BLOCK, hs(h)], 0.0) * wt_ref[h:h + 1, :]
            s = (j0 + u) * BLOCK + key_i
            ok = (s <= t) & ((s >= PAD) | (s == t))
            bits = lax.bitcast_convert_type(acc, jnp.int32)
            bits = jnp.where(bits == INT_MIN, 0, bits)
            key = bits ^ ((bits >> 31) & 0x7FFFFFFF)
            s_ref[j0 + u] = jnp.where(ok, key, INT_MIN)
        return 0

    lax.fori_loop(0, ngrp, score_group, 0)

    def slice_group(g, _):
        for half in range(2):
            words = []
            for u in (2 * half, 2 * half + 1):
                tile = s_ref[g * SCORE_GROUP + u] ^ INT_MIN
                words += [tile[r * 8:(r + 1) * 8, :] for r in range(BLOCK // 8)]
            planes = _bit_transpose32(words)
            for bit in range(32):
                planes_ref[2 * g + half, bit] = planes[bit]
            planes_ref[2 * g + half, 32] = jnp.full((8, BLOCK), -1, jnp.int32)
        return 0

    lax.fori_loop(0, 2 * npair, slice_group, 0)
    alive_ref[...] = jnp.full(alive_ref.shape, -1, jnp.int32)

    def radix_pass(t, carry):
        thr, need, drop_prev = carry
        bit = 31 - t

        def body(pr, accs):
            accs = list(accs)
            for u in range(4):
                st = 4 * pr + u
                alive = alive_ref[st] & (planes_ref[st, bit + 1] ^ drop_prev)
                alive_ref[st] = alive
                accs[u] = accs[u] + lax.population_count(alive & planes_ref[st, bit])
            return tuple(accs)

        zero = jnp.zeros((8, BLOCK), jnp.int32)
        accs = lax.fori_loop(0, npair, body, (zero,) * 4)
        cnt = ((accs[0] + accs[1]) + (accs[2] + accs[3])).astype(F32)
        cnt = jnp.sum(cnt, axis=0, keepdims=True).astype(jnp.int32)
        take = cnt >= need
        thr = thr | jnp.where(take, jnp.int32(1) << bit, 0)
        need = jnp.where(take, need, need - cnt)
        return thr, need, jnp.where(take, 0, -1)

    zero = jnp.zeros((1, BLOCK), jnp.int32)
    thr, _, _ = lax.fori_loop(0, 32, radix_pass, (zero, zero + topk, zero))
    thr = jnp.maximum(thr ^ INT_MIN, INT_MIN + 1)

    m_ref[...] = jnp.full(m_ref.shape, NEG, F32)
    acc_ref[...] = jnp.zeros_like(acc_ref)

    def logits(j0, nb, dst_ref):
        w = nb * BLOCK
        dst_ref[0:w, :] = _dot(c_ref[pl.ds(pl.multiple_of(j0 * BLOCK, BLOCK), w), :], qlat_ref[...])

    def softmax_pv(j0, nb, lg_ref, pb_ref, bias_off=None):
        w = nb * BLOCK
        ckt = ct_ref[:, pl.ds(pl.multiple_of(j0 * BLOCK, BLOCK), w)]
        mask_ref[0:w, :] = jnp.where(s_ref[pl.ds(j0, nb)].reshape(w, BLOCK) >= thr, 0.0, NEG)
        chunks = [slice(c, c + DSA_ROWS) for c in range(0, w, DSA_ROWS)]
        fold = lambda a: a.reshape(DSA_ROWS // 8, 8, BLOCK)
        alphas = []
        for h in range(DSA_HEADS):
            def logits_of(rs):
                x = lg_ref[rs, hs(h)] + mask_ref[rs, :]
                if bias_off is not None:
                    x = x + bias_ref[h, pl.ds(bias_off + rs.start, DSA_ROWS), :]
                return x

            mx = jnp.full((8, BLOCK), NEG, F32)
            for rs in chunks:
                mx = jnp.maximum(mx, jnp.max(fold(logits_of(rs)), axis=0))
            m_prev = m_ref[h:h + 1, :]
            m_new = jnp.maximum(m_prev, jnp.max(mx, axis=0, keepdims=True))
            alphas.append(jnp.exp2(m_prev - m_new))
            m_ref[h:h + 1, :] = m_new
            for rs in chunks:
                pb_ref[rs, hs(h)] = jnp.exp2(logits_of(rs) - m_new).astype(BF16)
        pv = _dot(ckt, pb_ref[0:w, :])
        for h in range(DSA_HEADS):
            acc_ref[:, hs(h)] = acc_ref[:, hs(h)] * alphas[h] + pv[:, hs(h)]

    near0 = jnp.maximum(i - 1, 0)
    logits(near0, 2, lgn_ref)

    n_far = jnp.maximum(i - 1, 0)
    n_big = n_far // FAR_BLOCKS
    big = lambda g: g * FAR_BLOCKS

    @pl.when(n_big > 0)
    def _():
        logits(0, FAR_BLOCKS, lg0_ref)

        def pair_of_tiles(g, _):
            logits(big(2 * g + 1), FAR_BLOCKS, lg1_ref)
            softmax_pv(big(2 * g), FAR_BLOCKS, lg0_ref, p_ref)
            logits(big(jnp.minimum(2 * g + 2, n_big - 1)), FAR_BLOCKS, lg0_ref)
            softmax_pv(big(2 * g + 1), FAR_BLOCKS, lg1_ref, p1_ref)
            return 0

        lax.fori_loop(0, n_big // 2, pair_of_tiles, 0)

        @pl.when(n_big % 2 == 1)
        def _():
            softmax_pv(big(n_big - 1), FAR_BLOCKS, lg0_ref, p_ref)

    def far_one(j, _):
        logits(j, 1, lg0_ref)
        softmax_pv(j, 1, lg0_ref, p_ref)
        return 0

    lax.fori_loop(n_big * FAR_BLOCKS, n_far, far_one, 0)
    softmax_pv(near0, 2, lgn_ref, p1_ref, bias_off=pl.multiple_of(jnp.where(i >= 1, 0, BLOCK), BLOCK))

    for h in range(DSA_HEADS):
        inv_l = 1.0 / acc_ref[DSA_LATENT:DSA_LATENT + 1, hs(h)]
        o_lat = (acc_ref[0:DSA_LATENT, hs(h)] * inv_l).astype(BF16)
        o = _dot_tn(o_lat, wuv_ref[h])
        o_ref[:, h * DSA_HEAD_DIM:(h + 1) * DSA_HEAD_DIM] = o.astype(o_ref.dtype)


def _dsa(pbf3, misc3, c3, ct3, wuk, wuv, bias, topk):
    b, t, _ = pbf3.shape
    nblk = t // BLOCK
    hq = DSA_HEADS * DSA_HEAD_DIM
    ik_blk = 6 * 1024 // LANE
    small_blk = (IN_W["rq"] + IN_W["rk"] + IN_W["rg"] + IN_W["dc"]) // LANE
    n_sets = 4 * (((nblk + SCORE_GROUP - 1) // SCORE_GROUP + 1) // 2)
    n_tiles = nblk + 2 * SCORE_GROUP - 1
    const3 = lambda bi, i: (0, 0, 0)
    return pl.pallas_call(
        functools.partial(_dsa_kernel, topk=topk, nblk=nblk),
        grid=(b, nblk),
        in_specs=[
            pl.BlockSpec((None, BLOCK, hq), lambda bi, i: (bi, i, 0)),
            pl.BlockSpec((None, BLOCK, IDX_HEADS * IDX_DIM), lambda bi, i: (bi, i, 1)),
            pl.BlockSpec((None, BLOCK, LANE), lambda bi, i: (bi, i, small_blk)),
            pl.BlockSpec((None, t, LANE), lambda bi, i: (bi, 0, ik_blk)),
            pl.BlockSpec((None, t, DSA_LATENT), lambda bi, i: (bi, 0, 0)),
            pl.BlockSpec((None, DSA_LATENT + 8, t), lambda bi, i: (bi, 0, 0)),
            pl.BlockSpec((DSA_HEADS, DSA_LATENT, DSA_HEAD_DIM), const3),
            pl.BlockSpec((DSA_HEADS, DSA_LATENT, DSA_HEAD_DIM), const3),
            pl.BlockSpec((DSA_HEADS, 3 * BLOCK, BLOCK), const3),
        ],
        out_specs=pl.BlockSpec((None, BLOCK, hq), lambda bi, i: (bi, i, 0)),
        out_shape=jax.ShapeDtypeStruct((b, t, hq), BF16),
        scratch_shapes=[
            pltpu.VMEM((n_tiles, BLOCK, BLOCK), jnp.int32),
            pltpu.VMEM((DSA_LATENT, DSA_HEADS * BLOCK), BF16),
            pltpu.VMEM((FAR_BLOCKS * BLOCK, DSA_HEADS * BLOCK), BF16),
            pltpu.VMEM((DSA_HEADS, BLOCK), F32),
            pltpu.VMEM((DSA_LATENT + 8, DSA_HEADS * BLOCK), F32),
            pltpu.VMEM((FAR_BLOCKS * BLOCK, DSA_HEADS * BLOCK), F32),
            pltpu.VMEM((FAR_BLOCKS * BLOCK, DSA_HEADS * BLOCK), F32),
            pltpu.VMEM((2 * BLOCK, DSA_HEADS * BLOCK), F32),
            pltpu.VMEM((LANE, IDX_HEADS * BLOCK), BF16),
            pltpu.VMEM((IDX_HEADS, BLOCK), F32),
            pltpu.VMEM((n_sets, 33, 8, BLOCK), jnp.int32),
            pltpu.VMEM((n_sets, 8, BLOCK), jnp.int32),
            pltpu.VMEM((FAR_BLOCKS * BLOCK, DSA_HEADS * BLOCK), BF16),
            pltpu.VMEM((FAR_BLOCKS * BLOCK, BLOCK), F32),
        ],
        compiler_params=_params(("parallel", "arbitrary")),
    )(pbf3, pbf3, misc3, pbf3, c3, ct3, wuk, wuv, bias)


def _dsa_weights(w_uk, w_uv):
    return jnp.transpose(w_uk, (1, 0, 2)).astype(BF16), jnp.transpose(w_uv, (1, 0, 2)).astype(BF16)


FOX_ROWS = 64


def _fox_kernel(q_ref, qa_ref, k_ref, ka_ref, v_ref, o_ref, m_ref, l_ref, acc_ref, lg_ref, p_ref, *, tq):
    i = pl.program_id(2)
    qq = jnp.concatenate([q_ref[...], qa_ref[...]], axis=1)
    m_ref[...] = jnp.full(m_ref.shape, NEG, F32)
    l_ref[...] = jnp.zeros_like(l_ref)
    acc_ref[...] = jnp.zeros_like(acc_ref)
    rows = min(FOX_ROWS, tq)
    chunks = [slice(c, c + rows) for c in range(0, tq, rows)]

    def qk(j, slot):
        off = pl.multiple_of(j * tq, tq)
        kk = jnp.concatenate([k_ref[pl.ds(off, tq), :], ka_ref[pl.ds(off, tq), :]], axis=1)
        lg_ref[slot] = _dot_nt(kk, qq)

    def softmax_pv(j, slot, mode):
        def logits(rs):
            x = lg_ref[slot, rs, :]
            if mode == "none":
                return x
            s = rs.start + lax.broadcasted_iota(jnp.int32, (rows, 1), 0)
            if mode == "pad":
                return jnp.where(s >= PAD, x, NEG)
            t = lax.broadcasted_iota(jnp.int32, (1, tq), 1)
            return jnp.where((s <= t) & ((s + j * tq >= PAD) | (s == t)), x, NEG)

        fold = lambda a: a.reshape(rows // 8, 8, tq)
        mx = jnp.full((8, tq), NEG, F32)
        for rs in chunks:
            mx = jnp.maximum(mx, jnp.max(fold(logits(rs)), axis=0))
        m_prev = m_ref[...]
        m_new = jnp.maximum(m_prev, jnp.max(mx, axis=0, keepdims=True))
        alpha = jnp.exp2(m_prev - m_new)
        tot = jnp.zeros((8, tq), F32)
        for rs in chunks:
            p = jnp.exp2(logits(rs) - m_new)
            tot = tot + jnp.sum(fold(p), axis=0)
            p_ref[rs, :] = p.astype(BF16)
        l_ref[...] = alpha * l_ref[...] + jnp.sum(tot, axis=0, keepdims=True)
        off = pl.multiple_of(j * tq, tq)
        acc_ref[...] = acc_ref[...] * alpha + _dot_tn(v_ref[pl.ds(off, tq), :], p_ref[...])
        m_ref[...] = m_new

    qk(i, 2)

    @pl.when(i > 0)
    def _():
        qk(0, 0)
        qk(jnp.minimum(1, i), 1)
        softmax_pv(0, 0, "pad")
        n_mid = i - 1

        def pair(g, _):
            a = 2 * g + 1
            qk(jnp.minimum(a + 1, i), 0)
            softmax_pv(a, 1, "none")
            qk(jnp.minimum(a + 2, i), 1)
            softmax_pv(a + 1, 0, "none")
            return 0

        lax.fori_loop(0, n_mid // 2, pair, 0)

        @pl.when(n_mid % 2 == 1)
        def _():
            softmax_pv(i - 1, 1, "none")

    softmax_pv(i, 2, "diag")

    o_ref[...] = jnp.transpose(acc_ref[...] * (1.0 / l_ref[...])).astype(o_ref.dtype)


def _fox(pbf3, aug_q, aug_k):
    b, t, _ = pbf3.shape
    nblk = t // BLOCK
    tq = BLOCK * max(k for k in (1, 5) if nblk % k == 0)
    d = FOX_HEAD_DIM
    qb, kb, vb = 2 * 1024 // d, 3 * 1024 // d, 4 * 1024 // d
    return pl.pallas_call(
        functools.partial(_fox_kernel, tq=tq),
        grid=(b, FOX_HEADS, t // tq),
        in_specs=[
            pl.BlockSpec((None, tq, d), lambda bi, h, i: (bi, i, qb + h)),
            pl.BlockSpec((None, tq, d), lambda bi, h, i: (bi, i, h)),
            pl.BlockSpec((None, t, d), lambda bi, h, i: (bi, 0, kb + h)),
            pl.BlockSpec((None, t, d), lambda bi, h, i: (bi, 0, h)),
            pl.BlockSpec((None, t, d), lambda bi, h, i: (bi, 0, vb + h)),
        ],
        out_specs=pl.BlockSpec((None, tq, d), lambda bi, h, i: (bi, i, h)),
        out_shape=jax.ShapeDtypeStruct((b, t, FOX_HEADS * d), BF16),
        scratch_shapes=[pltpu.VMEM((1, tq), F32), pltpu.VMEM((1, tq), F32), pltpu.VMEM((d, tq), F32),
                        pltpu.VMEM((3, tq, tq), F32), pltpu.VMEM((tq, tq), BF16)],
        compiler_params=_params(("parallel", "parallel", "arbitrary")),
    )(pbf3, aug_q, pbf3, aug_k, pbf3)


def _ret_kernel(rq_ref, rk_ref, rv_ref, rg_ref, cos_ref, sin_ref, dq_ref, dk_ref, decay_ref, cd_ref, gn_ref,
                o_ref, state_ref):
    i = pl.program_id(1)

    @pl.when(i == 0)
    def _():
        state_ref[...] = jnp.zeros_like(state_ref)

    half = RET_HEADS * RET_QK_DIM // 2
    cos = cos_ref[...]
    sin = sin_ref[...]

    def rope(x):
        x1, x2 = x[:, :half], x[:, half:]
        return jnp.concatenate([x1 * cos - x2 * sin, x1 * sin + x2 * cos], axis=1)

    q = rope(rq_ref[...])
    k = rope(rk_ref[...]) * RET_QK_DIM ** -0.5
    pos = i * BLOCK + lax.broadcasted_iota(jnp.int32, (BLOCK, 1), 0)
    k = jnp.where(pos >= PAD, k, 0.0)
    qb = q.astype(BF16)
    kb = k.astype(BF16)
    q_dec = (q * dq_ref[...]).astype(BF16)
    k_dec = (k * dk_ref[...]).astype(BF16)
    head_of_col = (lax.broadcasted_iota(jnp.int32, (1, 2 * half), 1) % half) // (RET_QK_DIM // 2)
    zero = jnp.zeros((), BF16)
    for h in range(RET_HEADS):
        mine = head_of_col == h
        vh = rv_ref[:, h * RET_V_DIM:(h + 1) * RET_V_DIM]
        s = (_dot_nt(jnp.where(mine, qb, zero), kb) * decay_ref[h]).astype(BF16)
        st = state_ref[h]
        o = _dot(s, vh) + _dot(jnp.where(mine, q_dec, zero), st.astype(BF16))
        state_ref[h] = cd_ref[h] * st + _dot_tn(k_dec, vh)
        mu = jnp.mean(o, axis=1, keepdims=True)
        var = jnp.mean(jnp.square(o - mu), axis=1, keepdims=True)
        sl = slice(h * RET_V_DIM, (h + 1) * RET_V_DIM)
        o = (o - mu) * lax.rsqrt(var + GN_EPS) * gn_ref[:, sl]
        g = rg_ref[:, sl]
        o_ref[:, sl] = (g * jax.nn.sigmoid(g) * o).astype(o_ref.dtype)


def _retention(misc3, pbf3, cos, sin, dq_tab, dk_tab, decay, cd, gn_w):
    b, t, _ = misc3.shape
    qk = RET_HEADS * RET_QK_DIM
    vw = RET_HEADS * RET_V_DIM
    const2 = lambda bi, i: (0, 0)
    const3 = lambda bi, i: (0, 0, 0)
    return pl.pallas_call(
        _ret_kernel,
        grid=(b, t // BLOCK),
        in_specs=[
            pl.BlockSpec((None, BLOCK, qk), lambda bi, i: (bi, i, 0)),
            pl.BlockSpec((None, BLOCK, qk), lambda bi, i: (bi, i, 1)),
            pl.BlockSpec((None, BLOCK, vw), lambda bi, i: (bi, i, 5)),
            pl.BlockSpec((None, BLOCK, vw), lambda bi, i: (bi, i, 1)),
            pl.BlockSpec((BLOCK, qk // 2), lambda bi, i: (i, 0)),
            pl.BlockSpec((BLOCK, qk // 2), lambda bi, i: (i, 0)),
            pl.BlockSpec((BLOCK, qk), const2),
            pl.BlockSpec((BLOCK, qk), const2),
            pl.BlockSpec((RET_HEADS, BLOCK, BLOCK), const3),
            pl.BlockSpec((RET_HEADS, 1, RET_V_DIM), const3),
            pl.BlockSpec((1, vw), const2),
        ],
        out_specs=pl.BlockSpec((None, BLOCK, vw), lambda bi, i: (bi, i, 0)),
        out_shape=jax.ShapeDtypeStruct((b, t, vw), BF16),
        scratch_shapes=[pltpu.VMEM((RET_HEADS, qk, RET_V_DIM), F32)],
        compiler_params=_params(("parallel", "arbitrary")),
    )(misc3, misc3, pbf3, misc3, cos, sin, dq_tab, dk_tab, decay, cd, gn_w.reshape(1, vw))


def _merge_kernel(x_ref, g_ref, b0_ref, b1_ref, b2_ref, wg0_ref, wg1_ref, wg2_ref, wb_ref, o_ref, xn_ref):
    @pl.when(pl.program_id(1) == 0)
    def _():
        xn_ref[...] = _rms(x_ref[...], g_ref[...]).astype(BF16)

    xn = xn_ref[...]
    acc = jax.nn.sigmoid(_dot(xn, wg0_ref[...])) * _dot(b0_ref[...], wb_ref[0])
    acc = acc + jax.nn.sigmoid(_dot(xn, wg1_ref[...])) * _dot(b1_ref[...], wb_ref[1])
    acc = acc + jax.nn.sigmoid(_dot(xn, wg2_ref[...])) * _dot(b2_ref[...], wb_ref[2])
    o_ref[...] = acc.astype(o_ref.dtype)


def _merge(x, g, b0, b1, b2, w_gates, w_branch):
    n, bw = b0.shape
    dm = x.shape[1]
    d = w_branch.shape[2]
    tm = _pick_tile(n, 640)
    tn = _pick_tile(d, 512, LANE)
    nj = d // tn
    bspec = pl.BlockSpec((tm, bw), lambda i, j: (i, 0))
    gspec = lambda r: pl.BlockSpec((dm, tn), lambda i, j: (0, r * nj + j))
    return pl.pallas_call(
        _merge_kernel,
        grid=(n // tm, nj),
        in_specs=[pl.BlockSpec((tm, dm), lambda i, j: (i, 0)), pl.BlockSpec((1, dm), lambda i, j: (0, 0)),
                  bspec, bspec, bspec, gspec(0), gspec(1), gspec(2),
                  pl.BlockSpec((N_BRANCH, bw, tn), lambda i, j: (0, 0, j))],
        out_specs=pl.BlockSpec((tm, tn), lambda i, j: (i, j)),
        out_shape=jax.ShapeDtypeStruct((n, d), BF16),
        scratch_shapes=[pltpu.VMEM((tm, dm), BF16)],
        compiler_params=_params(("parallel", "arbitrary")),
    )(x, g.reshape(1, dm), b0, b1, b2, w_gates, w_gates, w_gates, w_branch)


def _resproj_kernel(a_ref, w_ref, r_ref, o_ref):
    o_ref[...] = r_ref[...] + _dot(a_ref[...], w_ref[...])


def _resproj(a, w, res):
    n, k = a.shape
    d = w.shape[1]
    tm = _pick_tile(n, 1280)
    tn = _pick_tile(d, 1024, LANE)
    return pl.pallas_call(
        _resproj_kernel,
        grid=(n // tm, d // tn),
        in_specs=[
            pl.BlockSpec((tm, k), lambda i, j: (i, 0)),
            pl.BlockSpec((k, tn), lambda i, j: (0, j)),
            pl.BlockSpec((tm, tn), lambda i, j: (i, j)),
        ],
        out_specs=pl.BlockSpec((tm, tn), lambda i, j: (i, j)),
        out_shape=jax.ShapeDtypeStruct((n, d), F32),
        compiler_params=_params(("parallel", "arbitrary")),
    )(a, w, res)


def _t5_bucket(dist):
    d = jnp.maximum(dist, 1).astype(F32)
    large = T5_MAX_EXACT + (jnp.log(d / T5_MAX_EXACT) / math.log(T5_MAX_DIST / T5_MAX_EXACT)
                            * (T5_BUCKETS - T5_MAX_EXACT)).astype(jnp.int32)
    large = jnp.minimum(large, T5_BUCKETS - 1)
    return jnp.where(dist < T5_MAX_EXACT, dist, large)


def _t5_bias_tiles(t5_table):
    k = jnp.arange(BLOCK, dtype=jnp.int32)[:, None]
    q = jnp.arange(BLOCK, dtype=jnp.int32)[None, :]
    table = t5_table.astype(F32)
    rel = table - table[T5_BUCKETS - 1]

    def lookup(dist):
        onehot = (_t5_bucket(dist)[..., None] == jnp.arange(T5_BUCKETS, dtype=jnp.int32)).astype(F32)
        return jnp.einsum("kqb,bh->hkq", onehot, rel, precision=lax.Precision.HIGHEST)

    bdiag = lookup(jnp.maximum(q - k, 0))
    bprev = lookup(BLOCK + q - k)
    return jnp.concatenate([bprev, bdiag, jnp.zeros_like(bdiag)], axis=1) * LOG2E


def _rope_tables(t):
    half = RET_QK_DIM // 2
    inv = ROPE_BASE ** (-jnp.arange(half, dtype=F32) / half)
    ang = jnp.arange(t, dtype=jnp.int32).astype(F32)[:, None] * inv[None, :]
    return jnp.tile(jnp.cos(ang), (1, RET_HEADS)), jnp.tile(jnp.sin(ang), (1, RET_HEADS))


def _decay_tables():
    log_gamma = jnp.log1p(-(2.0 ** (-5.0 - jnp.arange(RET_HEADS, dtype=F32))))
    i = jnp.arange(BLOCK, dtype=F32)
    gap = i[:, None] - i[None, :]
    decay = jnp.where(gap[None] >= 0, jnp.exp(jnp.maximum(gap, 0.0)[None] * log_gamma[:, None, None]), 0.0)
    head_of_col = (np.arange(RET_HEADS * RET_QK_DIM) % (RET_HEADS * RET_QK_DIM // 2)) // (RET_QK_DIM // 2)
    lg_col = log_gamma[head_of_col]
    dk_tab = jnp.exp((BLOCK - 1 - i)[:, None] * lg_col[None, :])
    dq_tab = jnp.exp((i + 1.0)[:, None] * lg_col[None, :])
    cd = jnp.broadcast_to(jnp.exp(BLOCK * log_gamma)[:, None, None], (RET_HEADS, 1, RET_V_DIM))
    return decay.astype(F32), dq_tab, dk_tab, cd


def _split_w_in(w16):
    cols = lambda name: w16[:, IN_OFF[name]:IN_OFF[name] + IN_W[name]]
    zeros = lambda w: jnp.zeros((w16.shape[0], w), BF16)
    rope = lambda a: jnp.swapaxes(a.reshape(-1, RET_HEADS, 2, RET_QK_DIM // 2), 1, 2).reshape(a.shape)
    w_bf = jnp.concatenate([cols("dq"), cols("iq"), cols("fq"), cols("fk"), cols("fv"), cols("rv"),
                            cols("ik"), zeros(LANE - IDX_DIM)], axis=1)
    w_misc = jnp.concatenate([rope(cols("rq")), rope(cols("rk")), cols("rg"), cols("dc"),
                              cols("iw"), cols("ff"), zeros(LANE - IDX_HEADS - FOX_HEADS),
                              zeros(LANE)], axis=1)
    assert w_bf.shape[1] == BF_COLS and w_misc.shape[1] == MISC_COLS
    return w_bf, w_misc, cols("gates")


def _bf_col_scale():
    scale = np.ones((BF_COLS,), np.float32)
    scale[2 * 1024:3 * 1024] = FOX_HEAD_DIM ** -0.5 * LOG2E
    return jnp.asarray(scale)


def _mixer(x, b, t, mix_norm, w_in16, kv_norm, w_uk, w_uv, f_bias, gn_w, w_branch16, w_out16, tables, topk):
    n = b * t
    bias_tiles, rope_tabs, decay_tabs = tables
    w_bf, w_misc, w_gates = _split_w_in(w_in16)
    pbf, misc, c = _proj(x, mix_norm, w_bf, _bf_col_scale(), w_misc, kv_norm)
    pbf3 = pbf.reshape(b, t, BF_COLS)
    misc3 = misc.reshape(b, t, MISC_COLS)

    c3 = c.reshape(b, t, DSA_LATENT)
    wuk, wuv = _dsa_weights(w_uk, w_uv)
    ones_rows = jnp.zeros((b, 8, t), BF16).at[:, 0].set(1)
    ct3 = jnp.concatenate([jnp.swapaxes(c3, 1, 2), ones_rows], axis=1)
    dsa_o = _dsa(pbf3, misc3, c3, ct3, wuk, wuv, bias_tiles, topk)

    fox_o = _fox(pbf3, *_fox_bias_columns(misc3, f_bias))

    ret_o = _retention(misc3, pbf3, *rope_tabs, *decay_tabs, gn_w)

    merged = _merge(x, mix_norm, dsa_o.reshape(n, -1), fox_o.reshape(n, -1), ret_o.reshape(n, -1), w_gates,
                    w_branch16)
    return _resproj(merged, w_out16, x)


def kernel(x, meta_tokens, t5_table, ffn1_norm, ffn1_w1, ffn1_w3, ffn1_w2, mix_norm, w_in, dsa_kv_norm,
           dsa_w_uk, dsa_w_uv, fox_f_bias, ret_gn_w, w_branch, w_out, ffn2_norm, ffn2_w1, ffn2_w3, ffn2_w2,
           final_norm):
    b, seq, d = x.shape
    assert d == D_MODEL and seq % BLOCK == 0
    t = seq + BLOCK
    depth = ffn1_norm.shape[0]
    topk = min(INDEX_TOPK, seq // 4)
    h = jnp.concatenate([
        jnp.zeros((b, PAD, d), x.dtype),
        jnp.broadcast_to(meta_tokens.astype(x.dtype)[None], (b, N_META, d)),
        x,
    ], axis=1).reshape(b * t, d)
    cos, sin = _rope_tables(t)
    decay, dq_tab, dk_tab, cd = _decay_tables()
    tables = (_t5_bias_tiles(t5_table), (cos, sin), (dq_tab, dk_tab, decay, cd))
    bf = lambda w, l: w[l].astype(BF16)
    w_in16 = lax.optimization_barrier(w_in.astype(BF16))
    for l in range(depth):
        h = _ffn(h, ffn1_norm[l], bf(ffn1_w1, l), bf(ffn1_w3, l), bf(ffn1_w2, l))
        h = _mixer(h, b, t, mix_norm[l], w_in16[l], dsa_kv_norm[l], dsa_w_uk[l], dsa_w_uv[l],
                   fox_f_bias[l], ret_gn_w[l], bf(w_branch, l), bf(w_out, l), tables, topk)
        h = _ffn(h, ffn2_norm[l], bf(ffn2_w1, l), bf(ffn2_w3, l), bf(ffn2_w2, l),
                 final_g=final_norm if l == depth - 1 else None)
    return h.reshape(b, t, d)[:, BLOCK:]
```
